```python
import jax, jax.numpy as jnp
from jax import lax
import numpy as np

D_MODEL = 4096
BATCH = 4
SEQ = 2048
DEPTH = 4
DEC_BATCH = 8
DEC_SEQ = 1
PAST_LEN = 8192
PAGE_SIZE = 128

N_A_LAYERS = DEPTH // 2
N_B_LAYERS = DEPTH - N_A_LAYERS
D_RNN = D_MODEL
LRU_HEADS = 16
LRU_BLOCK = D_RNN // LRU_HEADS
CONV_W = 4
LRU_C = 8.0
HEAD_DIM = 128
N_SLOTS = 16
KV_HEADS = 4
GQA = N_SLOTS // KV_HEADS
WINDOWS = (128, 512, 2048)
DILATIONS = (1, 4, 16)
N_GROUPS = 3
BAND = 128
ATTN_W = N_SLOTS * HEAD_DIM
Q_W = N_GROUPS * ATTN_W
KV_W = N_GROUPS * 2 * KV_HEADS * HEAD_DIM
N_BUCKETS = 32
MAX_EXACT = 16
MAX_DIST = 2048
EPS = 1e-6

kernel_name = "yoco_rglru_dilated_swa_decoder_step"


def rmsnorm(x, g):
    xf = x.astype(jnp.float32)
    y = xf * lax.rsqrt(jnp.mean(xf * xf, axis=-1, keepdims=True) + EPS)
    return (y * g.astype(jnp.float32)).astype(x.dtype)


def rel_bucket(dist):
    dist = np.asarray(dist)
    d = np.maximum(dist, 1).astype(np.float32)
    large = MAX_EXACT + (np.log(d / MAX_EXACT) / np.log(MAX_DIST / MAX_EXACT)
                         * (N_BUCKETS - MAX_EXACT)).astype(np.int32)
    large = np.minimum(large, N_BUCKETS - 1)
    return np.where(dist < MAX_EXACT, dist, large).astype(np.int32)


def _lru_combine(left, right):
    a1, b1 = left
    a2, b2 = right
    return a1 * a2, a2 * b1 + b2


def rglru_mixer(u, conv_buf, h0, w_in, conv_w, conv_b, w_ga, b_ga, w_gx, b_gx, lam, w_out):
    n, t, _ = u.shape
    f32 = jnp.float32
    proj = u @ w_in
    xb, gate = proj[..., :D_RNN], proj[..., D_RNN:]
    xc = jnp.concatenate([conv_buf.astype(xb.dtype), xb], axis=1)
    xconv = conv_b
    for k in range(CONV_W):
        xconv = xconv + xc[:, k:k + t] * conv_w[k]
    new_conv = xc[:, xc.shape[1] - (CONV_W - 1):]
    xf = xconv.astype(f32)
    xh = xf.reshape(n, t, LRU_HEADS, LRU_BLOCK)
    r = jax.nn.sigmoid(jnp.einsum('nthi,hij->nthj', xh, w_ga.astype(f32)) + b_ga.astype(f32)).reshape(n, t, D_RNN)
    i = jax.nn.sigmoid(jnp.einsum('nthi,hij->nthj', xh, w_gx.astype(f32)) + b_gx.astype(f32)).reshape(n, t, D_RNN)
    log_a = -LRU_C * jax.nn.softplus(-lam.astype(f32)) * r
    a = jnp.exp(log_a)
    b = jnp.sqrt(-jnp.expm1(2.0 * log_a)) * (i * xf)
    b = b.at[:, 0].add(a[:, 0] * h0.astype(f32))
    _, h = lax.associative_scan(_lru_combine, (a, b), axis=1)
    y = (h.astype(u.dtype) * jax.nn.silu(gate)) @ w_out
    return y, new_conv, h[:, -1]


def dilated_prompt(q, k, v, r, table_g):
    n, t = q.shape[:2]
    f32 = jnp.float32
    win = r * BAND
    t_pad = -(-t // win) * win
    j = t_pad // r
    nb = j // BAND

    def to_band(a):
        h = a.shape[2]
        a = jnp.pad(a, ((0, 0), (0, t_pad - t), (0, 0), (0, 0))).reshape(n, j, r, h, HEAD_DIM)
        return a.transpose(0, 2, 1, 3, 4).reshape(n, r, nb, BAND, h, HEAD_DIM)

    def with_prev(a):
        prev = jnp.pad(a, ((0, 0), (0, 0), (1, 0), (0, 0), (0, 0), (0, 0)))[:, :, :-1]
        return jnp.concatenate([prev, a], axis=3)

    qb = to_band(q.astype(f32)).reshape(n, r, nb, BAND, KV_HEADS, GQA, HEAD_DIM)
    kb = with_prev(to_band(k.astype(f32)))
    vb = with_prev(to_band(v.astype(f32)))
    s = jnp.einsum('bcnqkgd,bcnpkd->bcnkgqp', qb, kb) * (HEAD_DIM ** -0.5)
    qq = np.arange(BAND)[:, None]
    kk = np.arange(2 * BAND)[None, :]
    diff = BAND + qq - kk
    in_band = (diff >= 0) & (diff <= BAND)
    bucket = rel_bucket(np.clip(diff, 0, BAND) * r)
    bias = table_g[bucket].transpose(2, 0, 1).reshape(KV_HEADS, GQA, BAND, 2 * BAND)
    mask = in_band[None] & ((np.arange(nb)[:, None, None] > 0) | (kk[None] >= BAND))
    s = jnp.where(mask[:, None, None], s + bias, -jnp.inf)
    m = jnp.max(s, axis=-1)
    p = jnp.exp(s - m[..., None])
    den = jnp.sum(p, axis=-1)
    o = jnp.einsum('bcnkgqp,bcnpkd->bcnqkgd', p, vb)
    o = o.reshape(n, r, j, N_SLOTS, HEAD_DIM).transpose(0, 2, 1, 3, 4).reshape(n, t_pad, N_SLOTS, HEAD_DIM)[:, :t]

    def stat(a):
        a = a.transpose(0, 1, 2, 5, 3, 4).reshape(n, r, j, N_SLOTS)
        return a.transpose(0, 2, 1, 3).reshape(n, t_pad, N_SLOTS)[:, :t]

    return o, stat(m), stat(den)


def dilated_sample(q, k_ext, v_ext, r, table_g):
    n, s_len = q.shape[:2]
    f32 = jnp.float32
    lb = k_ext.shape[1] - s_len
    steps = np.arange(BAND + 1)
    idx = lb + np.arange(s_len)[:, None] - steps[None, :] * r
    valid = idx >= 0
    idx = np.maximum(idx, 0)
    kg = k_ext[:, idx].astype(f32)
    vg = v_ext[:, idx].astype(f32)
    qh = q.astype(f32).reshape(n, s_len, KV_HEADS, GQA, HEAD_DIM)
    s = jnp.einsum('nskgd,nspkd->nkgsp', qh, kg) * (HEAD_DIM ** -0.5)
    bias = table_g[rel_bucket(steps * r)].T.reshape(KV_HEADS, GQA, 1, BAND + 1)
    s = jnp.where(valid[None, None, None], s + bias, -jnp.inf)
    m = jnp.max(s, axis=-1)
    p = jnp.exp(s - m[..., None])
    den = jnp.sum(p, axis=-1)
    o = jnp.einsum('nkgsp,nspkd->nskgd', p, vg).reshape(n, s_len, N_SLOTS, HEAD_DIM)
    m = m.transpose(0, 3, 1, 2).reshape(n, s_len, N_SLOTS)
    den = den.transpose(0, 3, 1, 2).reshape(n, s_len, N_SLOTS)
    return o, m, den


def combine_groups(parts):
    ms = jnp.stack([p[1] for p in parts])
    mx = jnp.max(ms, axis=0)
    e = jnp.exp(ms - mx[None])
    num = sum(e[g][..., None] * parts[g][0] for g in range(len(parts)))
    den = sum(e[g] * parts[g][2] for g in range(len(parts)))
    return num / den[..., None]


def trunk(x, conv_state, h_state, kv_bufs, a_pre_g, a_w_in, a_conv_w, a_conv_b, a_w_gate_a, a_b_gate_a,
          a_w_gate_x, a_b_gate_x, a_lambda, a_w_out, a_post_g, kv_norm_g, w_kv, rel_bias,
          b_pre_g, b_w_in, b_w_out, b_post_g):
    n, t, _ = x.shape
    new_conv, new_h, new_kv, kv_groups = [], [], [], []
    for layer in range(DEPTH):
        if layer < N_A_LAYERS:
            l = layer
            y, c, h = rglru_mixer(rmsnorm(x, a_pre_g[l]), conv_state[l], h_state[l], a_w_in[l], a_conv_w[l],
                                  a_conv_b[l], a_w_gate_a[l], a_b_gate_a[l], a_w_gate_x[l], a_b_gate_x[l],
                                  a_lambda[l], a_w_out[l])
            x = x + rmsnorm(y, a_post_g[l])
            new_conv.append(c)
            new_h.append(h)
            continue
        if layer == N_A_LAYERS:
            kv = (rmsnorm(x, kv_norm_g) @ w_kv).reshape(n, t, N_GROUPS, 2, KV_HEADS, HEAD_DIM)
            for g in range(N_GROUPS):
                kvg = kv[:, :, g]
                if kv_bufs is None:
                    keep = min(WINDOWS[g], t)
                    new_kv.append(kvg[:, t - keep:])
                    kv_groups.append(kvg)
                else:
                    lb = kv_bufs[g].shape[1]
                    ext = jnp.concatenate([kv_bufs[g].astype(kvg.dtype), kvg], axis=1)
                    new_kv.append(ext[:, ext.shape[1] - lb:])
                    kv_groups.append(ext)
        l = layer - N_A_LAYERS
        proj = rmsnorm(x, b_pre_g[l]) @ b_w_in[l]
        q = proj[..., :Q_W].reshape(n, t, N_GROUPS, N_SLOTS, HEAD_DIM)
        gate = proj[..., Q_W:]
        parts = []
        for g in range(N_GROUPS):
            table_g = rel_bias[:, g * N_SLOTS:(g + 1) * N_SLOTS].astype(jnp.float32)
            kvg = kv_groups[g]
            if kv_bufs is None:
                parts.append(dilated_prompt(q[:, :, g], kvg[:, :, 0], kvg[:, :, 1], DILATIONS[g], table_g))
            else:
                parts.append(dilated_sample(q[:, :, g], kvg[:, :, 0], kvg[:, :, 1], DILATIONS[g], table_g))
        o = combine_groups(parts)
        y = (o.reshape(n, t, ATTN_W).astype(x.dtype) * jax.nn.silu(gate)) @ b_w_out[l]
        x = x + rmsnorm(y, b_post_g[l])
    return x, jnp.stack(new_conv), jnp.stack(new_h), new_kv[0], new_kv[1], new_kv[2]


def setup_inputs(seed: int = 0) -> dict:
    key = jax.random.key(seed)
    ks = jax.random.split(key, 32)
    f32 = jnp.float32
    nrm = lambda k, shape, s=1.0: (jax.random.normal(k, shape, f32) * s)
    lb = [min(w, PAST_LEN) for w in WINDOWS]
    lam_u = jax.random.uniform(ks[10], (N_A_LAYERS, D_RNN), f32, minval=0.9, maxval=0.999)
    sig = lam_u ** (1.0 / LRU_C)
    return {
        "x_prompt": nrm(ks[0], (BATCH, SEQ, D_MODEL)),
        "x_sample": nrm(ks[1], (DEC_BATCH, DEC_SEQ, D_MODEL)),
        "state_conv": nrm(ks[2], (N_A_LAYERS, DEC_BATCH, CONV_W - 1, D_RNN)),
        "state_h": nrm(ks[3], (N_A_LAYERS, DEC_BATCH, D_RNN), 0.5),
        "state_kv_w128": nrm(ks[4], (DEC_BATCH, lb[0], 2, KV_HEADS, HEAD_DIM)),
        "state_kv_w512": nrm(ks[5], (DEC_BATCH, lb[1], 2, KV_HEADS, HEAD_DIM)),
        "state_kv_w2048": nrm(ks[6], (DEC_BATCH, lb[2], 2, KV_HEADS, HEAD_DIM)),
        "a_pre_g": 1.0 + nrm(ks[7], (N_A_LAYERS, D_MODEL), 0.01),
        "a_w_in": nrm(ks[8], (N_A_LAYERS, D_MODEL, 2 * D_RNN), D_MODEL ** -0.5),
        "a_conv_w": nrm(ks[9], (N_A_LAYERS, CONV_W, D_RNN), CONV_W ** -0.5),
        "a_conv_b": nrm(ks[11], (N_A_LAYERS, D_RNN), 0.01),
        "a_w_gate_a": nrm(ks[12], (N_A_LAYERS, LRU_HEADS, LRU_BLOCK, LRU_BLOCK), LRU_BLOCK ** -0.5),
        "a_b_gate_a": nrm(ks[13], (N_A_LAYERS, LRU_HEADS, LRU_BLOCK), 0.01),
        "a_w_gate_x": nrm(ks[14], (N_A_LAYERS, LRU_HEADS, LRU_BLOCK, LRU_BLOCK), LRU_BLOCK ** -0.5),
        "a_b_gate_x": nrm(ks[15], (N_A_LAYERS, LRU_HEADS, LRU_BLOCK), 0.01),
        "a_lambda": jnp.log(sig) - jnp.log1p(-sig),
        "a_w_out": nrm(ks[16], (N_A_LAYERS, D_RNN, D_MODEL), D_RNN ** -0.5),
        "a_post_g": 1.0 + nrm(ks[17], (N_A_LAYERS, D_MODEL), 0.01),
        "kv_norm_g": 1.0 + nrm(ks[18], (D_MODEL,), 0.01),
        "w_kv": nrm(ks[19], (D_MODEL, KV_W), D_MODEL ** -0.5),
        "rel_bias": nrm(ks[20], (N_BUCKETS, N_GROUPS * N_SLOTS), 0.2),
        "b_pre_g": 1.0 + nrm(ks[21], (N_B_LAYERS, D_MODEL), 0.01),
        "b_w_in": nrm(ks[22], (N_B_LAYERS, D_MODEL, Q_W + ATTN_W), D_MODEL ** -0.5),
        "b_w_out": nrm(ks[23], (N_B_LAYERS, ATTN_W, D_MODEL), ATTN_W ** -0.5),
        "b_post_g": 1.0 + nrm(ks[24], (N_B_LAYERS, D_MODEL), 0.01),
    }


def reference(x_prompt, x_sample, state_conv, state_h, state_kv_w128, state_kv_w512, state_kv_w2048,
              a_pre_g, a_w_in, a_conv_w, a_conv_b, a_w_gate_a, a_b_gate_a, a_w_gate_x, a_b_gate_x,
              a_lambda, a_w_out, a_post_g, kv_norm_g, w_kv, rel_bias, b_pre_g, b_w_in, b_w_out, b_post_g):
    nb_p = x_prompt.shape[0]
    conv0 = jnp.zeros((N_A_LAYERS, nb_p, CONV_W - 1, D_RNN), x_prompt.dtype)
    h0 = jnp.zeros((N_A_LAYERS, nb_p, D_RNN), jnp.float32)
    y_prompt, p_conv, p_h, p_kv128, p_kv512, p_kv2048 = trunk(
        x_prompt, conv0, h0, None, a_pre_g, a_w_in, a_conv_w, a_conv_b, a_w_gate_a, a_b_gate_a,
        a_w_gate_x, a_b_gate_x, a_lambda, a_w_out, a_post_g, kv_norm_g, w_kv, rel_bias,
        b_pre_g, b_w_in, b_w_out, b_post_g)
    y_sample, s_conv, s_h, s_kv128, s_kv512, s_kv2048 = trunk(
        x_sample, state_conv, state_h, (state_kv_w128, state_kv_w512, state_kv_w2048), a_pre_g, a_w_in,
        a_conv_w, a_conv_b, a_w_gate_a, a_b_gate_a, a_w_gate_x, a_b_gate_x, a_lambda, a_w_out, a_post_g,
        kv_norm_g, w_kv, rel_bias, b_pre_g, b_w_in, b_w_out, b_post_g)
    return (y_prompt, y_sample, p_conv, p_h, p_kv128, p_kv512, p_kv2048, s_conv, s_h, s_kv128, s_kv512, s_kv2048)
```

```python
import functools
import math

import numpy as np
import jax
import jax.numpy as jnp
from jax import lax
from jax.experimental import pallas as pl
from jax.experimental.pallas import tpu as pltpu

F32 = jnp.float32
BF16 = jnp.bfloat16

EPS = 1e-6
LRU_C = 8.0
CONV_W = 4
HEAD_DIM = 128
N_SLOTS = 16
KV_HEADS = 4
GQA = N_SLOTS // KV_HEADS
WINDOWS = (128, 512, 2048)
DILATIONS = (1, 4, 16)
N_GROUPS = 3
BAND = 128
ATTN_W = N_SLOTS * HEAD_DIM
Q_W = N_GROUPS * ATTN_W
KV_GROUP_W = 2 * KV_HEADS * HEAD_DIM
N_BUCKETS = 32
MAX_EXACT = 16
MAX_DIST = 2048
SM_SCALE = HEAD_DIM ** -0.5

LANES = 128
VMEM_LIMIT = 56 * 1024 * 1024


def _params(*sem):
    return pltpu.CompilerParams(dimension_semantics=sem, vmem_limit_bytes=VMEM_LIMIT)


def _tile(dim, pref):
    if dim <= pref:
        return dim
    t = pref
    while dim % t:
        t //= 2
    return t


def _rms(x, g):
    return x * lax.rsqrt(jnp.mean(x * x, axis=-1, keepdims=True) + EPS) * g


def _norm_kernel(x_ref, g_ref, *o_refs):
    x = x_ref[...]
    xhat = x * lax.rsqrt(jnp.mean(x * x, axis=-1, keepdims=True) + EPS)
    for k, o_ref in enumerate(o_refs):
        o_ref[...] = (xhat * g_ref[k:k + 1, :]).astype(o_ref.dtype)


def _norm_call(x, gains):
    m, d = x.shape
    tm = _tile(m, 256)
    g = jnp.stack(gains)
    ng = len(gains)
    row = pl.BlockSpec((tm, d), lambda i: (i, 0))
    return pl.pallas_call(
        _norm_kernel,
        grid=(m // tm,),
        in_specs=[row, pl.BlockSpec((ng, d), lambda i: (0, 0))],
        out_specs=[row] * ng,
        out_shape=[jax.ShapeDtypeStruct((m, d), BF16)] * ng,
        compiler_params=_params("parallel"),
        name="rmsnorm",
    )(x, g)


def _resnorm_kernel(y_ref, x_ref, gp_ref, g_ref, xo_ref, *o_refs):
    xn = x_ref[...] + _rms(y_ref[...], gp_ref[...])
    xo_ref[...] = xn
    if o_refs:
        xhat = xn * lax.rsqrt(jnp.mean(xn * xn, axis=-1, keepdims=True) + EPS)
        for k, o_ref in enumerate(o_refs):
            o_ref[...] = (xhat * g_ref[k:k + 1, :]).astype(o_ref.dtype)


def _resnorm_call(y, x, g_post, gains):
    m, d = x.shape
    tm = _tile(m, 256)
    ng = len(gains)
    g = jnp.stack(gains) if gains else jnp.ones((1, d), F32)
    row = pl.BlockSpec((tm, d), lambda i: (i, 0))
    outs = pl.pallas_call(
        _resnorm_kernel,
        grid=(m // tm,),
        in_specs=[row, row, pl.BlockSpec((1, d), lambda i: (0, 0)),
                  pl.BlockSpec((g.shape[0], d), lambda i: (0, 0))],
        out_specs=[row] * (1 + ng),
        out_shape=[jax.ShapeDtypeStruct((m, d), F32)] + [jax.ShapeDtypeStruct((m, d), BF16)] * ng,
        compiler_params=_params("parallel"),
        name="residual_rmsnorm",
    )(y, x, g_post.reshape(1, d), g)
    return outs[0], list(outs[1:])


def _mm_kernel(a_ref, w_ref, *o_refs, natural, heads):
    y = jnp.dot(a_ref[...], w_ref[...], preferred_element_type=F32)
    k = 0
    if natural:
        o_refs[k][...] = y
        k += 1
    if heads:
        for h in range(y.shape[1] // LANES):
            o_refs[k][h] = y[:, h * LANES:(h + 1) * LANES]


def _mm_call(a, w, *, natural=True, heads=False, tm=1024, tn=512, name="matmul"):
    m, kdim = a.shape
    n = w.shape[1]
    tm = _tile(m, tm)
    tn = _tile(n, tn)
    out_specs, out_shape = [], []
    if natural:
        out_specs.append(pl.BlockSpec((tm, tn), lambda i, j: (i, j)))
        out_shape.append(jax.ShapeDtypeStruct((m, n), F32))
    if heads:
        out_specs.append(pl.BlockSpec((tn // LANES, tm, LANES), lambda i, j: (j, i, 0)))
        out_shape.append(jax.ShapeDtypeStruct((n // LANES, m, LANES), F32))
    outs = pl.pallas_call(
        functools.partial(_mm_kernel, natural=natural, heads=heads),
        grid=(m // tm, n // tn),
        in_specs=[pl.BlockSpec((tm, kdim), lambda i, j: (i, 0)),
                  pl.BlockSpec((kdim, tn), lambda i, j: (0, j))],
        out_specs=out_specs,
        out_shape=out_shape,
        compiler_params=_params("parallel", "arbitrary"),
        name=name,
    )(a, w)
    return outs if len(outs) > 1 else outs[0]


def _rglru_kernel(xb_ref, gate_ref, cs_ref, h0_ref, cw_ref, cb_ref, wga_ref, bga_ref, wgx_ref, bgx_ref,
                  lam_ref, hg_ref, nc_ref, nh_ref, ext_ref, hc_ref, a_ref, b_ref, h_ref, *, tt, blk):
    ti = pl.program_id(2)
    tail = CONV_W - 1
    base = 8

    @pl.when(ti == 0)
    def _():
        ext_ref[base - tail:base, :] = cs_ref[...]
        hc_ref[...] = h0_ref[...]

    ext_ref[base:base + tt, :] = xb_ref[...]
    xconv = cb_ref[...]
    for k in range(CONV_W):
        xconv = xconv + ext_ref[base - tail + k:base - tail + k + tt, :] * cw_ref[k:k + 1, :]
    new_tail = ext_ref[base + tt - tail:base + tt, :]
    ext_ref[base - tail:base, :] = new_tail

    r_parts, i_parts = [], []
    for hh in range(xconv.shape[1] // blk):
        xh = xconv[:, hh * blk:(hh + 1) * blk].astype(BF16)
        r_parts.append(jnp.dot(xh, wga_ref[hh], preferred_element_type=F32))
        i_parts.append(jnp.dot(xh, wgx_ref[hh], preferred_element_type=F32))
    r = jax.nn.sigmoid(jnp.concatenate(r_parts, axis=1) + bga_ref[...])
    i = jax.nn.sigmoid(jnp.concatenate(i_parts, axis=1) + bgx_ref[...])

    neg_lam = -lam_ref[...]
    softplus = jnp.maximum(neg_lam, 0.0) + jnp.log1p(jnp.exp(-jnp.abs(neg_lam)))
    a = jnp.exp((-LRU_C * softplus) * r)
    a_ref[...] = a
    b_ref[...] = jnp.sqrt(1.0 - a * a) * (i * xconv)

    def step(t, h):
        h = a_ref[pl.ds(t, 1), :] * h + b_ref[pl.ds(t, 1), :]
        h_ref[pl.ds(t, 1), :] = h
        return h

    h_last = lax.fori_loop(0, tt, step, hc_ref[...], unroll=min(tt, 8))
    hc_ref[...] = h_last
    hg_ref[...] = (h_ref[...] * jax.nn.silu(gate_ref[...])).astype(hg_ref.dtype)

    @pl.when(ti == pl.num_programs(2) - 1)
    def _():
        nc_ref[...] = new_tail
        nh_ref[...] = h_last


def _rglru_call(proj, conv_state, h0, conv_w, conv_b, w_ga, b_ga, w_gx, b_gx, lam):
    n, t, c2 = proj.shape
    c = c2 // 2
    heads, blk, _ = w_ga.shape
    cb = _tile(c, max(512, blk))
    assert cb % blk == 0 and blk % LANES == 0
    tt = _tile(t, 256)
    ncb = c // cb
    row = lambda a: a.reshape(1, c)
    vec = pl.BlockSpec((1, cb), lambda j, b, i: (0, j))
    wspec = pl.BlockSpec((cb // blk, blk, blk), lambda j, b, i: (j, 0, 0))
    hg, nc, nh = pl.pallas_call(
        functools.partial(_rglru_kernel, tt=tt, blk=blk),
        grid=(ncb, n, t // tt),
        in_specs=[
            pl.BlockSpec((None, tt, cb), lambda j, b, i: (b, i, j)),
            pl.BlockSpec((None, tt, cb), lambda j, b, i: (b, i, ncb + j)),
            pl.BlockSpec((None, CONV_W - 1, cb), lambda j, b, i: (b, 0, j)),
            pl.BlockSpec((None, 1, cb), lambda j, b, i: (b, 0, j)),
            pl.BlockSpec((CONV_W, cb), lambda j, b, i: (0, j)),
            vec, wspec, vec, wspec, vec, vec,
        ],
        out_specs=[
            pl.BlockSpec((None, tt, cb), lambda j, b, i: (b, i, j)),
            pl.BlockSpec((None, CONV_W - 1, cb), lambda j, b, i: (b, 0, j)),
            pl.BlockSpec((None, 1, cb), lambda j, b, i: (b, 0, j)),
        ],
        out_shape=[
            jax.ShapeDtypeStruct((n, t, c), BF16),
            jax.ShapeDtypeStruct((n, CONV_W - 1, c), F32),
            jax.ShapeDtypeStruct((n, 1, c), F32),
        ],
        scratch_shapes=[
            pltpu.VMEM((8 + tt, cb), F32),
            pltpu.VMEM((1, cb), F32),
            pltpu.VMEM((tt, cb), F32),
            pltpu.VMEM((tt, cb), F32),
            pltpu.VMEM((tt, cb), F32),
        ],
        compiler_params=_params("parallel", "parallel", "arbitrary"),
        name="rglru_core",
    )(proj, proj, conv_state, h0.reshape(n, 1, c), conv_w, row(conv_b), w_ga.astype(BF16), row(b_ga),
      w_gx.astype(BF16), row(b_gx), row(lam))
    return hg, nc, nh.reshape(n, c)


def _rel_bucket(dist):
    dist = np.asarray(dist)
    d = np.maximum(dist, 1).astype(np.float32)
    large = MAX_EXACT + (np.log(d / MAX_EXACT) / np.log(MAX_DIST / MAX_EXACT)
                         * (N_BUCKETS - MAX_EXACT)).astype(np.int32)
    large = np.minimum(large, N_BUCKETS - 1)
    return np.where(dist < MAX_EXACT, dist, large).astype(np.int32)


def _prompt_bias(rel_bias):
    qq = np.arange(BAND)[:, None]
    kk = np.arange(2 * BAND)[None, :]
    diff = BAND + qq - kk
    in_band = (diff >= 0) & (diff <= BAND)
    tables = []
    for g, r in enumerate(DILATIONS):
        bucket = _rel_bucket(np.clip(diff, 0, BAND) * r)
        tg = rel_bias[:, g * N_SLOTS:(g + 1) * N_SLOTS].astype(F32)
        bias = jnp.transpose(tg[bucket], (2, 0, 1))
        tables.append(jnp.where(in_band[None], bias, -jnp.inf))
    return jnp.stack(tables)


def _sample_bias(rel_bias):
    tables = []
    for g, r in enumerate(DILATIONS):
        steps = np.concatenate([BAND - np.arange(BAND), np.zeros(BAND, np.int64)])
        tg = rel_bias[:, g * N_SLOTS:(g + 1) * N_SLOTS].astype(F32)
        tables.append(tg[_rel_bucket(steps * r)].T)
    return jnp.stack(tables)


def _attn_prompt_kernel(q_ref, k_ref, v_ref, bias_ref, gate_ref, o_ref, acc_ref, m_ref, l_ref, *, t):
    g = pl.program_id(2)

    def unit(start, r, first, has_prev):
        nk = 2 * BAND if has_prev else BAND
        kstart = start - BAND * r if has_prev else start
        rows = lambda s0, cnt: pl.ds(s0, cnt, stride=r) if r > 1 else pl.ds(s0, cnt)
        k = k_ref[rows(kstart, nk), :].astype(BF16)
        v = v_ref[rows(kstart, nk), :].astype(BF16)
        for s in range(GQA):
            q = q_ref[s, rows(start, BAND), :].astype(BF16)
            sc = lax.dot_general(q, k, (((1,), (1,)), ((), ())), preferred_element_type=F32) * SM_SCALE
            sc = sc + bias_ref[s, :, 2 * BAND - nk:]
            m_cur = jnp.max(sc, axis=-1, keepdims=True)
            if first:
                m_new = jnp.broadcast_to(m_cur, (BAND, LANES))
            else:
                m_old = m_ref[s, rows(start, BAND), :]
                m_new = jnp.maximum(m_old, m_cur)
                alpha = jnp.exp(m_old - m_new)
            p = jnp.exp(sc - jnp.concatenate([m_new] * (nk // LANES), axis=1))
            l_cur = jnp.sum(p, axis=-1, keepdims=True)
            pv = jnp.dot(p.astype(BF16), v, preferred_element_type=F32)
            if first:
                l_new = jnp.broadcast_to(l_cur, (BAND, LANES))
                acc_new = pv
            else:
                l_new = alpha * l_ref[s, rows(start, BAND), :] + l_cur
                acc_new = alpha * acc_ref[s, rows(start, BAND), :] + pv
            m_ref[s, rows(start, BAND), :] = m_new
            l_ref[s, rows(start, BAND), :] = l_new
            acc_ref[s, rows(start, BAND), :] = acc_new

    def group_pass(r, first):
        nb = t // (BAND * r)

        def head_block(cls, carry):
            unit(cls, r, first, False)
            return carry

        lax.fori_loop(0, r, head_block, 0)

        def later_block(u, carry):
            cls = u % r
            blk = 1 + u // r
            unit(blk * (BAND * r) + cls, r, first, True)
            return carry

        if nb > 1:
            lax.fori_loop(0, r * (nb - 1), later_block, 0)

    for gi, r in enumerate(DILATIONS):
        pl.when(g == gi)(functools.partial(group_pass, r, gi == 0))

    @pl.when(g == N_GROUPS - 1)
    def _():
        for s in range(GQA):
            o = acc_ref[s] / l_ref[s]
            cols = slice(s * HEAD_DIM, (s + 1) * HEAD_DIM)
            o_ref[:, cols] = (o * jax.nn.silu(gate_ref[:, cols])).astype(o_ref.dtype)


def _attn_prompt_call(q_heads, kv_heads, bias, gate, n, t):
    assert all(t % (BAND * r) == 0 for r in DILATIONS)
    q5 = q_heads.reshape(N_GROUPS, KV_HEADS, GQA, n, t, HEAD_DIM)
    kv6 = kv_heads.reshape(N_GROUPS, 2, KV_HEADS, n, t, HEAD_DIM)
    gate3 = gate.reshape(n, t, ATTN_W)
    kv_spec = lambda which: pl.BlockSpec((None, None, None, None, t, HEAD_DIM),
                                         lambda b, h, g: (g, which, h, b, 0, 0))
    out = pl.pallas_call(
        functools.partial(_attn_prompt_kernel, t=t),
        grid=(n, KV_HEADS, N_GROUPS),
        in_specs=[
            pl.BlockSpec((None, None, GQA, None, t, HEAD_DIM), lambda b, h, g: (g, h, 0, b, 0, 0)),
            kv_spec(0), kv_spec(1),
            pl.BlockSpec((None, GQA, BAND, 2 * BAND), lambda b, h, g: (g, h, 0, 0)),
            pl.BlockSpec((None, t, GQA * HEAD_DIM), lambda b, h, g: (b, 0, h)),
        ],
        out_specs=pl.BlockSpec((None, t, GQA * HEAD_DIM), lambda b, h, g: (b, 0, h)),
        out_shape=jax.ShapeDtypeStruct((n, t, ATTN_W), BF16),
        scratch_shapes=[pltpu.VMEM((GQA, t, HEAD_DIM), F32)] * 3,
        compiler_params=_params("parallel", "parallel", "arbitrary"),
        name="attn_prompt",
    )(q5, kv6, kv6, bias, gate3)
    return out.reshape(n * t, ATTN_W)


def _attn_sample_kernel(q_ref, kvn_ref, s0_ref, s1_ref, s2_ref, bias_ref, gate_ref, o_ref):
    st_refs = (s0_ref, s1_ref, s2_ref)
    bf = lambda x: x.astype(BF16)
    rnd = lambda x: x.astype(BF16).astype(F32)
    for h in range(KV_HEADS):
        sl = slice(h * GQA, (h + 1) * GQA)
        s_old, s_new, v_old, v_new = [], [], [], []
        for g in range(N_GROUPS):
            q = q_ref[g * N_SLOTS + h * GQA:g * N_SLOTS + (h + 1) * GQA, :]
            k_old = st_refs[g][:, h * HEAD_DIM:(h + 1) * HEAD_DIM]
            v_old.append(st_refs[g][:, (KV_HEADS + h) * HEAD_DIM:(KV_HEADS + h + 1) * HEAD_DIM])
            row = g * 2 * KV_HEADS + h
            k_new = kvn_ref[row:row + 1, :]
            v_new.append(kvn_ref[row + KV_HEADS:row + KV_HEADS + 1, :])
            so = lax.dot_general(bf(q), bf(k_old), (((1,), (1,)), ((), ())), preferred_element_type=F32)
            s_old.append(so * SM_SCALE + bias_ref[g, sl, :BAND])
            sn = jnp.sum(rnd(q) * rnd(k_new), axis=-1, keepdims=True)
            s_new.append(sn * SM_SCALE + bias_ref[g, sl, BAND:BAND + 1])
        m = functools.reduce(jnp.maximum, [jnp.max(x, axis=-1, keepdims=True) for x in s_old] + s_new)
        num = jnp.zeros((GQA, HEAD_DIM), F32)
        den = jnp.zeros((GQA, 1), F32)
        for g in range(N_GROUPS):
            p_old = jnp.exp(s_old[g] - m)
            p_new = jnp.exp(s_new[g] - m)
            den = den + jnp.sum(p_old, axis=-1, keepdims=True) + p_new
            num = num + jnp.dot(bf(p_old), bf(v_old[g]), preferred_element_type=F32) + rnd(p_new) * rnd(v_new[g])
        o_ref[sl, :] = ((num / den) * jax.nn.silu(gate_ref[sl, :])).astype(o_ref.dtype)


def _attn_sample_call(q, kv_new, states, bias, gate):
    n = q.shape[0]
    st_views, st_specs = [], []
    for g, r in enumerate(DILATIONS):
        assert states[g].shape[1] == BAND * r
        st_views.append(states[g].reshape(n, BAND, r * KV_GROUP_W))
        st_specs.append(pl.BlockSpec((None, BAND, KV_GROUP_W), lambda b: (b, 0, 0)))
    per_b = lambda rows: pl.BlockSpec((None, rows, HEAD_DIM), lambda b: (b, 0, 0))
    out = pl.pallas_call(
        _attn_sample_kernel,
        grid=(n,),
        in_specs=[per_b(N_GROUPS * N_SLOTS), per_b(N_GROUPS * 2 * KV_HEADS)] + st_specs + [
            pl.BlockSpec((N_GROUPS, N_SLOTS, 2 * BAND), lambda b: (0, 0, 0)), per_b(N_SLOTS)],
        out_specs=per_b(N_SLOTS),
        out_shape=jax.ShapeDtypeStruct((n, N_SLOTS, HEAD_DIM), BF16),
        compiler_params=_params("parallel"),
        name="attn_sample",
    )(q.reshape(n, N_GROUPS * N_SLOTS, HEAD_DIM), kv_new.reshape(n, N_GROUPS * 2 * KV_HEADS, HEAD_DIM),
      *st_views, bias, gate.reshape(n, N_SLOTS, HEAD_DIM))
    return out.reshape(n, ATTN_W)


def _trunk(x, conv_state, h_state, kv_bufs, w):
    n, t, d = x.shape
    m = n * t
    n_a = w["a_w_in"].shape[0]
    n_b = w["b_wq"].shape[0]
    x2 = x.reshape(m, d)
    (xn,) = _norm_call(x2, [w["a_pre_g"][0]])
    new_conv, new_h = [], []
    for l in range(n_a):
        proj = _mm_call(xn, w["a_w_in"][l], name="a_in_proj")
        hg, nc, nh = _rglru_call(proj.reshape(n, t, -1), conv_state[l], h_state[l], w["a_conv_w"][l],
                                 w["a_conv_b"][l], w["a_w_gate_a"][l], w["a_b_gate_a"][l],
                                 w["a_w_gate_x"][l], w["a_b_gate_x"][l], w["a_lambda"][l])
        new_conv.append(nc)
        new_h.append(nh)
        y = _mm_call(hg.reshape(m, -1), w["a_w_out"][l], name="a_out_proj")
        gains = [w["a_pre_g"][l + 1]] if l + 1 < n_a else [w["kv_norm_g"], w["b_pre_g"][0]]
        x2, xns = _resnorm_call(y, x2, w["a_post_g"][l], gains)
        xn = xns[0]
    prompt = kv_bufs is None
    if prompt:
        kv, kv_heads = _mm_call(xn, w["w_kv"], heads=True, name="kv_proj")
        kv4 = kv.reshape(n, t, N_GROUPS, 2, KV_HEADS, HEAD_DIM)
        new_kv = [kv4[:, t - min(WINDOWS[g], t):, g] for g in range(N_GROUPS)]
        bias = _prompt_bias(w["rel_bias"])
    else:
        kv = _mm_call(xn, w["w_kv"], name="kv_proj")
        kv4 = kv.reshape(n, t, N_GROUPS, 2, KV_HEADS, HEAD_DIM)
        new_kv = [jnp.concatenate([kv_bufs[g][:, t:], kv4[:, :, g]], axis=1) for g in range(N_GROUPS)]
        bias = _sample_bias(w["rel_bias"])
    xn = xns[1]
    for l in range(n_b):
        if prompt:
            q_heads = _mm_call(xn, w["b_wq"][l], natural=False, heads=True, name="q_proj")
            gate = _mm_call(xn, w["b_wg"][l], name="gate_proj")
            og = _attn_prompt_call(q_heads, kv_heads, bias, gate, n, t)
        else:
            assert t == 1
            q = _mm_call(xn, w["b_wq"][l], name="q_proj")
            gate = _mm_call(xn, w["b_wg"][l], name="gate_proj")
            og = _attn_sample_call(q, kv, kv_bufs, bias, gate)
        y = _mm_call(og, w["b_w_out"][l], name="b_out_proj")
        gains = [w["b_pre_g"][l + 1]] if l + 1 < n_b else []
        x2, xns = _resnorm_call(y, x2, w["b_post_g"][l], gains)
        xn = xns[0] if xns else None
    return (x2.reshape(n, t, d), jnp.stack(new_conv), jnp.stack(new_h), *new_kv)


def kernel(x_prompt, x_sample, state_conv, state_h, state_kv_w128, state_kv_w512, state_kv_w2048,
           a_pre_g, a_w_in, a_conv_w, a_conv_b, a_w_gate_a, a_b_gate_a, a_w_gate_x, a_b_gate_x,
           a_lambda, a_w_out, a_post_g, kv_norm_g, w_kv, rel_bias, b_pre_g, b_w_in, b_w_out, b_post_g):
    w = dict(
        a_pre_g=a_pre_g, a_w_in=a_w_in.astype(BF16), a_conv_w=a_conv_w, a_conv_b=a_conv_b,
        a_w_gate_a=a_w_gate_a, a_b_gate_a=a_b_gate_a, a_w_gate_x=a_w_gate_x, a_b_gate_x=a_b_gate_x,
        a_lambda=a_lambda, a_w_out=a_w_out.astype(BF16), a_post_g=a_post_g, kv_norm_g=kv_norm_g,
        w_kv=w_kv.astype(BF16), rel_bias=rel_bias, b_pre_g=b_pre_g,
        b_wq=b_w_in[:, :, :Q_W].astype(BF16), b_wg=b_w_in[:, :, Q_W:].astype(BF16),
        b_w_out=b_w_out.astype(BF16), b_post_g=b_post_g)
    nb_p = x_prompt.shape[0]
    n_a = a_w_in.shape[0]
    d_rnn = a_conv_w.shape[-1]
    conv0 = jnp.zeros((n_a, nb_p, CONV_W - 1, d_rnn), F32)
    h0 = jnp.zeros((n_a, nb_p, d_rnn), F32)
    prompt_out = _trunk(x_prompt, conv0, h0, None, w)
    sample_out = _trunk(x_sample, state_conv, state_h, (state_kv_w128, state_kv_w512, state_kv_w2048), w)
    return (prompt_out[0], sample_out[0], *prompt_out[1:], *sample_out[1:])
```

```python
import functools
import math

import numpy as np
import jax
import jax.numpy as jnp
from jax import lax
from jax.experimental import pallas as pl
from jax.experimental.pallas import tpu as pltpu

F32 = jnp.float32
BF16 = jnp.bfloat16

EPS = 1e-6
LRU_C = 8.0
CONV_W = 4
HEAD_DIM = 128
N_SLOTS = 16
KV_HEADS = 4
GQA = N_SLOTS // KV_HEADS
WINDOWS = (128, 512, 2048)
DILATIONS = (1, 4, 16)
N_GROUPS = 3
BAND = 128
ATTN_W = N_SLOTS * HEAD_DIM
Q_W = N_GROUPS * ATTN_W
KV_GROUP_W = 2 * KV_HEADS * HEAD_DIM
N_BUCKETS = 32
MAX_EXACT = 16
MAX_DIST = 2048
SM_SCALE = HEAD_DIM ** -0.5

ATTN_UNROLL = 4
LANES = 128
VMEM_LIMIT = 56 * 1024 * 1024


def _params(*sem):
    return pltpu.CompilerParams(dimension_semantics=sem, vmem_limit_bytes=VMEM_LIMIT)


def _tile(dim, pref):
    if dim <= pref:
        return dim
    t = pref
    while dim % t:
        t //= 2
    return t


def _rms(x, g):
    return x * lax.rsqrt(jnp.mean(x * x, axis=-1, keepdims=True) + EPS) * g


def _norm_kernel(x_ref, g_ref, *o_refs):
    x = x_ref[...]
    xhat = x * lax.rsqrt(jnp.mean(x * x, axis=-1, keepdims=True) + EPS)
    for k, o_ref in enumerate(o_refs):
        o_ref[...] = (xhat * g_ref[k:k + 1, :]).astype(o_ref.dtype)


def _norm_call(x, gains):
    m, d = x.shape
    tm = _tile(m, 256)
    g = jnp.stack(gains)
    ng = len(gains)
    row = pl.BlockSpec((tm, d), lambda i: (i, 0))
    return pl.pallas_call(
        _norm_kernel,
        grid=(m // tm,),
        in_specs=[row, pl.BlockSpec((ng, d), lambda i: (0, 0))],
        out_specs=[row] * ng,
        out_shape=[jax.ShapeDtypeStruct((m, d), BF16)] * ng,
        compiler_params=_params("parallel"),
        name="rmsnorm",
    )(x, g)


def _resnorm_kernel(y_ref, x_ref, gp_ref, g_ref, xo_ref, *o_refs):
    xn = x_ref[...] + _rms(y_ref[...], gp_ref[...])
    xo_ref[...] = xn
    if o_refs:
        xhat = xn * lax.rsqrt(jnp.mean(xn * xn, axis=-1, keepdims=True) + EPS)
        for k, o_ref in enumerate(o_refs):
            o_ref[...] = (xhat * g_ref[k:k + 1, :]).astype(o_ref.dtype)


def _resnorm_call(y, x, g_post, gains):
    m, d = x.shape
    tm = _tile(m, 256)
    ng = len(gains)
    g = jnp.stack(gains) if gains else jnp.ones((1, d), F32)
    row = pl.BlockSpec((tm, d), lambda i: (i, 0))
    outs = pl.pallas_call(
        _resnorm_kernel,
        grid=(m // tm,),
        in_specs=[row, row, pl.BlockSpec((1, d), lambda i: (0, 0)),
                  pl.BlockSpec((g.shape[0], d), lambda i: (0, 0))],
        out_specs=[row] * (1 + ng),
        out_shape=[jax.ShapeDtypeStruct((m, d), F32)] + [jax.ShapeDtypeStruct((m, d), BF16)] * ng,
        compiler_params=_params("parallel"),
        name="residual_rmsnorm",
    )(y, x, g_post.reshape(1, d), g)
    return outs[0], list(outs[1:])


def _mm_kernel(a_ref, w_ref, *o_refs, natural, heads):
    y = jnp.dot(a_ref[...], w_ref[...], preferred_element_type=F32)
    k = 0
    if natural:
        o_refs[k][...] = y
        k += 1
    if heads:
        for h in range(y.shape[1] // LANES):
            o_refs[k][h] = y[:, h * LANES:(h + 1) * LANES]


def _mm_call(a, w, *, natural=True, heads=False, tm=1024, tn=512, name="matmul"):
    m, kdim = a.shape
    n = w.shape[1]
    tm = _tile(m, tm)
    tn = _tile(n, tn)
    out_specs, out_shape = [], []
    if natural:
        out_specs.append(pl.BlockSpec((tm, tn), lambda i, j: (i, j)))
        out_shape.append(jax.ShapeDtypeStruct((m, n), F32))
    if heads:
        out_specs.append(pl.BlockSpec((tn // LANES, tm, LANES), lambda i, j: (j, i, 0)))
        out_shape.append(jax.ShapeDtypeStruct((n // LANES, m, LANES), F32))
    outs = pl.pallas_call(
        functools.partial(_mm_kernel, natural=natural, heads=heads),
        grid=(m // tm, n // tn),
        in_specs=[pl.BlockSpec((tm, kdim), lambda i, j: (i, 0)),
                  pl.BlockSpec((kdim, tn), lambda i, j: (0, j))],
        out_specs=out_specs,
        out_shape=out_shape,
        compiler_params=_params("parallel", "arbitrary"),
        name=name,
    )(a, w)
    return outs if len(outs) > 1 else outs[0]


def _rglru_kernel(xb_ref, gate_ref, cs_ref, h0_ref, cw_ref, cb_ref, wga_ref, bga_ref, wgx_ref, bgx_ref,
                  lam_ref, hg_ref, nc_ref, nh_ref, ext_ref, hc_ref, a_ref, b_ref, h_ref, *, tt, blk):
    ti = pl.program_id(2)
    tail = CONV_W - 1
    base = 8

    @pl.when(ti == 0)
    def _():
        ext_ref[base - tail:base, :] = cs_ref[...]
        hc_ref[...] = h0_ref[...]

    ext_ref[base:base + tt, :] = xb_ref[...]
    xconv = cb_ref[...]
    for k in range(CONV_W):
        xconv = xconv + ext_ref[base - tail + k:base - tail + k + tt, :] * cw_ref[k:k + 1, :]
    new_tail = ext_ref[base + tt - tail:base + tt, :]
    ext_ref[base - tail:base, :] = new_tail

    r_parts, i_parts = [], []
    for hh in range(xconv.shape[1] // blk):
        xh = xconv[:, hh * blk:(hh + 1) * blk].astype(BF16)
        r_parts.append(jnp.dot(xh, wga_ref[hh], preferred_element_type=F32))
        i_parts.append(jnp.dot(xh, wgx_ref[hh], preferred_element_type=F32))
    r = jax.nn.sigmoid(jnp.concatenate(r_parts, axis=1) + bga_ref[...])
    i = jax.nn.sigmoid(jnp.concatenate(i_parts, axis=1) + bgx_ref[...])

    neg_lam = -lam_ref[...]
    softplus = jnp.maximum(neg_lam, 0.0) + jnp.log1p(jnp.exp(-jnp.abs(neg_lam)))
    a = jnp.exp((-LRU_C * softplus) * r)
    a_ref[...] = a
    b_ref[...] = jnp.sqrt(1.0 - a * a) * (i * xconv)

    def step(t, h):
        h = a_ref[pl.ds(t, 1), :] * h + b_ref[pl.ds(t, 1), :]
        h_ref[pl.ds(t, 1), :] = h
        return h

    h_last = lax.fori_loop(0, tt, step, hc_ref[...], unroll=min(tt, 8))
    hc_ref[...] = h_last
    hg_ref[...] = (h_ref[...] * jax.nn.silu(gate_ref[...])).astype(hg_ref.dtype)

    @pl.when(ti == pl.num_programs(2) - 1)
    def _():
        nc_ref[...] = new_tail
        nh_ref[...] = h_last


def _rglru_call(proj, conv_state, h0, conv_w, conv_b, w_ga, b_ga, w_gx, b_gx, lam):
    n, t, c2 = proj.shape
    c = c2 // 2
    heads, blk, _ = w_ga.shape
    cb = _tile(c, max(512, blk))
    assert cb % blk == 0 and blk % LANES == 0
    tt = _tile(t, 256)
    ncb = c // cb
    row = lambda a: a.reshape(1, c)
    vec = pl.BlockSpec((1, cb), lambda j, b, i: (0, j))
    wspec = pl.BlockSpec((cb // blk, blk, blk), lambda j, b, i: (j, 0, 0))
    hg, nc, nh = pl.pallas_call(
        functools.partial(_rglru_kernel, tt=tt, blk=blk),
        grid=(ncb, n, t // tt),
        in_specs=[
            pl.BlockSpec((None, tt, cb), lambda j, b, i: (b, i, j)),
            pl.BlockSpec((None, tt, cb), lambda j, b, i: (b, i, ncb + j)),
            pl.BlockSpec((None, CONV_W - 1, cb), lambda j, b, i: (b, 0, j)),
            pl.BlockSpec((None, 1, cb), lambda j, b, i: (b, 0, j)),
            pl.BlockSpec((CONV_W, cb), lambda j, b, i: (0, j)),
            vec, wspec, vec, wspec, vec, vec,
        ],
        out_specs=[
            pl.BlockSpec((None, tt, cb), lambda j, b, i: (b, i, j)),
            pl.BlockSpec((None, CONV_W - 1, cb), lambda j, b, i: (b, 0, j)),
            pl.BlockSpec((None, 1, cb), lambda j, b, i: (b, 0, j)),
        ],
        out_shape=[
            jax.ShapeDtypeStruct((n, t, c), BF16),
            jax.ShapeDtypeStruct((n, CONV_W - 1, c), F32),
            jax.ShapeDtypeStruct((n, 1, c), F32),
        ],
        scratch_shapes=[
            pltpu.VMEM((8 + tt, cb), F32),
            pltpu.VMEM((1, cb), F32),
            pltpu.VMEM((tt, cb), F32),
            pltpu.VMEM((tt, cb), F32),
            pltpu.VMEM((tt, cb), F32),
        ],
        compiler_params=_params("parallel", "parallel", "arbitrary"),
        name="rglru_core",
    )(proj, proj, conv_state, h0.reshape(n, 1, c), conv_w, row(conv_b), w_ga.astype(BF16), row(b_ga),
      w_gx.astype(BF16), row(b_gx), row(lam))
    return hg, nc, nh.reshape(n, c)


def _rel_bucket(dist):
    dist = np.asarray(dist)
    d = np.maximum(dist, 1).astype(np.float32)
    large = MAX_EXACT + (np.log(d / MAX_EXACT) / np.log(MAX_DIST / MAX_EXACT)
                         * (N_BUCKETS - MAX_EXACT)).astype(np.int32)
    large = np.minimum(large, N_BUCKETS - 1)
    return np.where(dist < MAX_EXACT, dist, large).astype(np.int32)


def _prompt_bias(rel_bias):
    span = 3 * BAND - 1
    neg = jnp.full((N_SLOTS, BAND - 1), -jnp.inf, F32)
    tables = []
    for g, r in enumerate(DILATIONS):
        tg = rel_bias[:, g * N_SLOTS:(g + 1) * N_SLOTS].astype(F32)
        tb = tg[_rel_bucket(np.arange(BAND + 1) * r)].T
        wr = jnp.concatenate([neg, tb[:, ::-1], neg, neg[:, :1]], axis=1)
        skew = jnp.tile(wr, (1, BAND))[:, :BAND * span].reshape(N_SLOTS, BAND, span)
        tables.append(skew[:, :, BAND - 1:])
    return jnp.stack(tables)


def _sample_bias(rel_bias):
    tables = []
    for g, r in enumerate(DILATIONS):
        steps = np.concatenate([BAND - np.arange(BAND), np.zeros(BAND, np.int64)])
        tg = rel_bias[:, g * N_SLOTS:(g + 1) * N_SLOTS].astype(F32)
        tables.append(tg[_rel_bucket(steps * r)].T)
    return jnp.stack(tables)


def _attn_prompt_kernel(q_ref, k_ref, v_ref, bias_ref, gate_ref, o_ref, acc_ref, m_ref, l_ref, *, t):
    g = pl.program_id(2)

    def unit(start, gi, r, has_prev):
        nk = 2 * BAND if has_prev else BAND
        kstart = start - BAND * r if has_prev else start
        rows = lambda s0, cnt: pl.ds(s0, cnt, stride=r) if r > 1 else pl.ds(s0, cnt)
        k = k_ref[rows(kstart, nk), :].astype(BF16)
        v = v_ref[rows(kstart, nk), :].astype(BF16)
        q = jnp.concatenate([q_ref[s, rows(start, BAND), :] for s in range(GQA)], axis=0).astype(BF16)
        sc = lax.dot_general(q, k, (((1,), (1,)), ((), ())), preferred_element_type=F32) * SM_SCALE
        sc = sc + bias_ref[:, :, 2 * BAND - nk:].reshape(GQA * BAND, nk)
        m = jnp.max(sc, axis=-1, keepdims=True)
        p = jnp.exp(sc - m)
        l = jnp.sum(p, axis=-1, keepdims=True)
        pv = jnp.dot(p.astype(BF16), v, preferred_element_type=F32)
        lane = lax.broadcasted_iota(jnp.int32, (BAND, LANES), 1)
        m_tile = jnp.zeros((BAND, LANES), F32)
        l_tile = jnp.zeros((BAND, LANES), F32)
        for s in range(GQA):
            blk = slice(s * BAND, (s + 1) * BAND)
            acc_ref[gi, s, rows(start, BAND), :] = pv[blk]
            m_tile = jnp.where(lane == s, m[blk], m_tile)
            l_tile = jnp.where(lane == s, l[blk], l_tile)
        m_ref[gi, rows(start, BAND), :] = m_tile
        l_ref[gi, rows(start, BAND), :] = l_tile

    def group_pass(gi, r):
        nb = t // (BAND * r)

        def head_block(cls, carry):
            unit(cls, gi, r, False)
            return carry

        lax.fori_loop(0, r, head_block, 0, unroll=min(r, ATTN_UNROLL))

        def later_block(u, carry):
            cls = u % r
            blk = 1 + u // r
            unit(blk * (BAND * r) + cls, gi, r, True)
            return carry

        if nb > 1:
            lax.fori_loop(0, r * (nb - 1), later_block, 0, unroll=ATTN_UNROLL)

    for gi, r in enumerate(DILATIONS):
        pl.when(g == gi)(functools.partial(group_pass, gi, r))

    @pl.when(g == N_GROUPS - 1)
    def _():
        def combine(c, carry):
            rows = pl.ds(pl.multiple_of(c * BAND, BAND), BAND)
            ms = [m_ref[gi, rows, :] for gi in range(N_GROUPS)]
            mx = functools.reduce(jnp.maximum, ms)
            es = [jnp.exp(mg - mx) for mg in ms]
            den = sum(es[gi] * l_ref[gi, rows, :] for gi in range(N_GROUPS))
            for s in range(GQA):
                num = sum(es[gi][:, s:s + 1] * acc_ref[gi, s, rows, :] for gi in range(N_GROUPS))
                cols = slice(s * HEAD_DIM, (s + 1) * HEAD_DIM)
                o = num / den[:, s:s + 1]
                o_ref[rows, cols] = (o * jax.nn.silu(gate_ref[rows, cols])).astype(o_ref.dtype)
            return carry

        lax.fori_loop(0, t // BAND, combine, 0)


def _attn_prompt_call(q_heads, kv_heads, bias, gate, n, t):
    assert all(t % (BAND * r) == 0 for r in DILATIONS)
    q5 = q_heads.reshape(N_GROUPS, KV_HEADS, GQA, n, t, HEAD_DIM)
    kv6 = kv_heads.reshape(N_GROUPS, 2, KV_HEADS, n, t, HEAD_DIM)
    gate3 = gate.reshape(n, t, ATTN_W)
    kv_spec = lambda which: pl.BlockSpec((None, None, None, None, t, HEAD_DIM),
                                         lambda b, h, g: (g, which, h, b, 0, 0))
    out = pl.pallas_call(
        functools.partial(_attn_prompt_kernel, t=t),
        grid=(n, KV_HEADS, N_GROUPS),
        in_specs=[
            pl.BlockSpec((None, None, GQA, None, t, HEAD_DIM), lambda b, h, g: (g, h, 0, b, 0, 0)),
            kv_spec(0), kv_spec(1),
            pl.BlockSpec((None, GQA, BAND, 2 * BAND), lambda b, h, g: (g, h, 0, 0)),
            pl.BlockSpec((None, t, GQA * HEAD_DIM), lambda b, h, g: (b, 0, h)),
        ],
        out_specs=pl.BlockSpec((None, t, GQA * HEAD_DIM), lambda b, h, g: (b, 0, h)),
        out_shape=jax.ShapeDtypeStruct((n, t, ATTN_W), BF16),
        scratch_shapes=[pltpu.VMEM((N_GROUPS, GQA, t, HEAD_DIM), F32),
                        pltpu.VMEM((N_GROUPS, t, LANES), F32),
                        pltpu.VMEM((N_GROUPS, t, LANES), F32)],
        compiler_params=_params("parallel", "parallel", "arbitrary"),
        name="attn_prompt",
    )(q5, kv6, kv6, bias, gate3)
    return out.reshape(n * t, ATTN_W)


def _attn_sample_kernel(q_ref, kvn_ref, s0_ref, s1_ref, s2_ref, bias_ref, gate_ref, o_ref):
    st_refs = (s0_ref, s1_ref, s2_ref)
    bf = lambda x: x.astype(BF16)
    rnd = lambda x: x.astype(BF16).astype(F32)
    for h in range(KV_HEADS):
        sl = slice(h * GQA, (h + 1) * GQA)
        s_old, s_new, v_old, v_new = [], [], [], []
        for g in range(N_GROUPS):
            q = q_ref[g * N_SLOTS + h * GQA:g * N_SLOTS + (h + 1) * GQA, :]
            k_old = st_refs[g][:, h * HEAD_DIM:(h + 1) * HEAD_DIM]
            v_old.append(st_refs[g][:, (KV_HEADS + h) * HEAD_DIM:(KV_HEADS + h + 1) * HEAD_DIM])
            row = g * 2 * KV_HEADS + h
            k_new = kvn_ref[row:row + 1, :]
            v_new.append(kvn_ref[row + KV_HEADS:row + KV_HEADS + 1, :])
            so = lax.dot_general(bf(q), bf(k_old), (((1,), (1,)), ((), ())), preferred_element_type=F32)
            s_old.append(so * SM_SCALE + bias_ref[g, sl, :BAND])
            sn = jnp.sum(rnd(q) * rnd(k_new), axis=-1, keepdims=True)
            s_new.append(sn * SM_SCALE + bias_ref[g, sl, BAND:BAND + 1])
        m = functools.reduce(jnp.maximum, [jnp.max(x, axis=-1, keepdims=True) for x in s_old] + s_new)
        num = jnp.zeros((GQA, HEAD_DIM), F32)
        den = jnp.zeros((GQA, 1), F32)
        for g in range(N_GROUPS):
            p_old = jnp.exp(s_old[g] - m)
            p_new = jnp.exp(s_new[g] - m)
            den = den + jnp.sum(p_old, axis=-1, keepdims=True) + p_new
            num = num + jnp.dot(bf(p_old), bf(v_old[g]), preferred_element_type=F32) + rnd(p_new) * rnd(v_new[g])
        o_ref[sl, :] = ((num / den) * jax.nn.silu(gate_ref[sl, :])).astype(o_ref.dtype)


def _attn_sample_call(q, kv_new, states, bias, gate):
    n = q.shape[0]
    st_views, st_specs = [], []
    for g, r in enumerate(DILATIONS):
        assert states[g].shape[1] == BAND * r
        st_views.append(states[g].reshape(n, BAND, r * KV_GROUP_W))
        st_specs.append(pl.BlockSpec((None, BAND, KV_GROUP_W), lambda b: (b, 0, 0)))
    per_b = lambda rows: pl.BlockSpec((None, rows, HEAD_DIM), lambda b: (b, 0, 0))
    out = pl.pallas_call(
        _attn_sample_kernel,
        grid=(n,),
        in_specs=[per_b(N_GROUPS * N_SLOTS), per_b(N_GROUPS * 2 * KV_HEADS)] + st_specs + [
            pl.BlockSpec((N_GROUPS, N_SLOTS, 2 * BAND), lambda b: (0, 0, 0)), per_b(N_SLOTS)],
        out_specs=per_b(N_SLOTS),
        out_shape=jax.ShapeDtypeStruct((n, N_SLOTS, HEAD_DIM), BF16),
        compiler_params=_params("parallel"),
        name="attn_sample",
    )(q.reshape(n, N_GROUPS * N_SLOTS, HEAD_DIM), kv_new.reshape(n, N_GROUPS * 2 * KV_HEADS, HEAD_DIM),
      *st_views, bias, gate.reshape(n, N_SLOTS, HEAD_DIM))
    return out.reshape(n, ATTN_W)


def _trunk(x, conv_state, h_state, kv_bufs, w):
    n, t, d = x.shape
    m = n * t
    n_a = w["a_w_in"].shape[0]
    n_b = w["b_wq"].shape[0]
    x2 = x.reshape(m, d)
    (xn,) = _norm_call(x2, [w["a_pre_g"][0]])
    new_conv, new_h = [], []
    for l in range(n_a):
        proj = _mm_call(xn, w["a_w_in"][l], name="a_in_proj")
        hg, nc, nh = _rglru_call(proj.reshape(n, t, -1), conv_state[l], h_state[l], w["a_conv_w"][l],
                                 w["a_conv_b"][l], w["a_w_gate_a"][l], w["a_b_gate_a"][l],
                                 w["a_w_gate_x"][l], w["a_b_gate_x"][l], w["a_lambda"][l])
        new_conv.append(nc)
        new_h.append(nh)
        y = _mm_call(hg.reshape(m, -1), w["a_w_out"][l], name="a_out_proj")
        gains = [w["a_pre_g"][l + 1]] if l + 1 < n_a else [w["kv_norm_g"], w["b_pre_g"][0]]
        x2, xns = _resnorm_call(y, x2, w["a_post_g"][l], gains)
        xn = xns[0]
    prompt = kv_bufs is None
    if prompt:
        kv, kv_heads = _mm_call(xn, w["w_kv"], heads=True, name="kv_proj")
        kv4 = kv.reshape(n, t, N_GROUPS, 2, KV_HEADS, HEAD_DIM)
        new_kv = [kv4[:, t - min(WINDOWS[g], t):, g] for g in range(N_GROUPS)]
        bias = _prompt_bias(w["rel_bias"])
    else:
        kv = _mm_call(xn, w["w_kv"], name="kv_proj")
        kv4 = kv.reshape(n, t, N_GROUPS, 2, KV_HEADS, HEAD_DIM)
        new_kv = [jnp.concatenate([kv_bufs[g][:, t:], kv4[:, :, g]], axis=1) for g in range(N_GROUPS)]
        bias = _sample_bias(w["rel_bias"])
    xn = xns[1]
    for l in range(n_b):
        if prompt:
            q_heads = _mm_call(xn, w["b_wq"][l], natural=False, heads=True, name="q_proj")
            gate = _mm_call(xn, w["b_wg"][l], name="gate_proj")
            og = _attn_prompt_call(q_heads, kv_heads, bias, gate, n, t)
        else:
            assert t == 1
            q = _mm_call(xn, w["b_wq"][l], name="q_proj")
            gate = _mm_call(xn, w["b_wg"][l], name="gate_proj")
            og = _attn_sample_call(q, kv, kv_bufs, bias, gate)
        y = _mm_call(og, w["b_w_out"][l], name="b_out_proj")
        gains = [w["b_pre_g"][l + 1]] if l + 1 < n_b else []
        x2, xns = _resnorm_call(y, x2, w["b_post_g"][l], gains)
        xn = xns[0] if xns else None
    return (x2.reshape(n, t, d), jnp.stack(new_conv), jnp.stack(new_h), *new_kv)


def kernel(x_prompt, x_sample, state_conv, state_h, state_kv_w128, state_kv_w512, state_kv_w2048,
           a_pre_g, a_w_in, a_conv_w, a_conv_b, a_w_gate_a, a_b_gate_a, a_w_gate_x, a_b_gate_x,
           a_lambda, a_w_out, a_post_g, kv_norm_g, w_kv, rel_bias, b_pre_g, b_w_in, b_w_out, b_post_g):
    w = dict(
        a_pre_g=a_pre_g, a_w_in=a_w_in.astype(BF16), a_conv_w=a_conv_w, a_conv_b=a_conv_b,
        a_w_gate_a=a_w_gate_a, a_b_gate_a=a_b_gate_a, a_w_gate_x=a_w_gate_x, a_b_gate_x=a_b_gate_x,
        a_lambda=a_lambda, a_w_out=a_w_out.astype(BF16), a_post_g=a_post_g, kv_norm_g=kv_norm_g,
        w_kv=w_kv.astype(BF16), rel_bias=rel_bias, b_pre_g=b_pre_g,
        b_wq=b_w_in[:, :, :Q_W].astype(BF16), b_wg=b_w_in[:, :, Q_W:].astype(BF16),
        b_w_out=b_w_out.astype(BF16), b_post_g=b_post_g)
    nb_p = x_prompt.shape[0]
    n_a = a_w_in.shape[0]
    d_rnn = a_conv_w.shape[-1]
    conv0 = jnp.zeros((n_a, nb_p, CONV_W - 1, d_rnn), F32)
    h0 = jnp.zeros((n_a, nb_p, d_rnn), F32)
    prompt_out = _trunk(x_prompt, conv0, h0, None, w)
    sample_out = _trunk(x_sample, state_conv, state_h, (state_kv_w128, state_kv_w512, state_kv_w2048), w)
    return (prompt_out[0], sample_out[0], *prompt_out[1:], *sample_out[1:])
```

```python
import functools
import math

import numpy as np
import jax
import jax.numpy as jnp
from jax import lax
from jax.experimental import pallas as pl
from jax.experimental.pallas import tpu as pltpu

F32 = jnp.float32
BF16 = jnp.bfloat16

EPS = 1e-6
LRU_C = 8.0
CONV_W = 4
HEAD_DIM = 128
N_SLOTS = 16
KV_HEADS = 4
GQA = N_SLOTS // KV_HEADS
WINDOWS = (128, 512, 2048)
DILATIONS = (1, 4, 16)
N_GROUPS = 3
BAND = 128
ATTN_W = N_SLOTS * HEAD_DIM
Q_W = N_GROUPS * ATTN_W
KV_GROUP_W = 2 * KV_HEADS * HEAD_DIM
N_BUCKETS = 32
MAX_EXACT = 16
MAX_DIST = 2048
SM_SCALE = HEAD_DIM ** -0.5

ATTN_UNROLL = 4
LANES = 128
VMEM_LIMIT = 56 * 1024 * 1024


def _params(*sem):
    return pltpu.CompilerParams(dimension_semantics=sem, vmem_limit_bytes=VMEM_LIMIT)


def _tile(dim, pref):
    if dim <= pref:
        return dim
    t = pref
    while dim % t:
        t //= 2
    return t


def _rms(x, g):
    return x * lax.rsqrt(jnp.mean(x * x, axis=-1, keepdims=True) + EPS) * g


def _norm_kernel(x_ref, g_ref, *o_refs):
    x = x_ref[...]
    xhat = x * lax.rsqrt(jnp.mean(x * x, axis=-1, keepdims=True) + EPS)
    for k, o_ref in enumerate(o_refs):
        o_ref[...] = (xhat * g_ref[k:k + 1, :]).astype(o_ref.dtype)


def _norm_call(x, gains):
    m, d = x.shape
    tm = _tile(m, 256)
    g = jnp.stack(gains)
    ng = len(gains)
    row = pl.BlockSpec((tm, d), lambda i: (i, 0))
    return pl.pallas_call(
        _norm_kernel,
        grid=(m // tm,),
        in_specs=[row, pl.BlockSpec((ng, d), lambda i: (0, 0))],
        out_specs=[row] * ng,
        out_shape=[jax.ShapeDtypeStruct((m, d), BF16)] * ng,
        compiler_params=_params("parallel"),
        name="rmsnorm",
    )(x, g)


def _resnorm_kernel(y_ref, x_ref, gp_ref, g_ref, xo_ref, *o_refs):
    xn = x_ref[...] + _rms(y_ref[...], gp_ref[...])
    xo_ref[...] = xn
    if o_refs:
        xhat = xn * lax.rsqrt(jnp.mean(xn * xn, axis=-1, keepdims=True) + EPS)
        for k, o_ref in enumerate(o_refs):
            o_ref[...] = (xhat * g_ref[k:k + 1, :]).astype(o_ref.dtype)


def _resnorm_call(y, x, g_post, gains):
    m, d = x.shape
    tm = _tile(m, 256)
    ng = len(gains)
    g = jnp.stack(gains) if gains else jnp.ones((1, d), F32)
    row = pl.BlockSpec((tm, d), lambda i: (i, 0))
    outs = pl.pallas_call(
        _resnorm_kernel,
        grid=(m // tm,),
        in_specs=[row, row, pl.BlockSpec((1, d), lambda i: (0, 0)),
                  pl.BlockSpec((g.shape[0], d), lambda i: (0, 0))],
        out_specs=[row] * (1 + ng),
        out_shape=[jax.ShapeDtypeStruct((m, d), F32)] + [jax.ShapeDtypeStruct((m, d), BF16)] * ng,
        compiler_params=_params("parallel"),
        name="residual_rmsnorm",
    )(y, x, g_post.reshape(1, d), g)
    return outs[0], list(outs[1:])


def _mm_kernel(a_ref, w_ref, *refs, natural, heads):
    *o_refs, wb_ref = refs

    @pl.when(pl.program_id(1) == 0)
    def _():
        wb_ref[...] = w_ref[...].astype(BF16)

    y = jnp.dot(a_ref[...], wb_ref[...], preferred_element_type=F32)
    k = 0
    if natural:
        o_refs[k][...] = y
        k += 1
    if heads:
        for h in range(y.shape[1] // LANES):
            o_refs[k][h] = y[:, h * LANES:(h + 1) * LANES]


def _mm_call(a, w, *, layer=0, cols=None, natural=True, heads=False, tm=1024, tn=512, name="matmul"):
    if w.ndim == 2:
        w = w[None]
    m, kdim = a.shape
    col0, n = cols if cols is not None else (0, w.shape[2])
    tm = _tile(m, tm)
    tn = _tile(n, tn)
    assert col0 % tn == 0
    out_specs, out_shape = [], []
    if natural:
        out_specs.append(pl.BlockSpec((tm, tn), lambda j, i: (i, j)))
        out_shape.append(jax.ShapeDtypeStruct((m, n), F32))
    if heads:
        out_specs.append(pl.BlockSpec((tn // LANES, tm, LANES), lambda j, i: (j, i, 0)))
        out_shape.append(jax.ShapeDtypeStruct((n // LANES, m, LANES), F32))
    outs = pl.pallas_call(
        functools.partial(_mm_kernel, natural=natural, heads=heads),
        grid=(n // tn, m // tm),
        in_specs=[pl.BlockSpec((tm, kdim), lambda j, i: (i, 0)),
                  pl.BlockSpec((None, kdim, tn), lambda j, i: (layer, 0, col0 // tn + j))],
        out_specs=out_specs,
        out_shape=out_shape,
        scratch_shapes=[pltpu.VMEM((kdim, tn), BF16)],
        compiler_params=_params("parallel", "arbitrary"),
        name=name,
    )(a, w)
    return outs if len(outs) > 1 else outs[0]


def _rglru_kernel(xb_ref, gate_ref, cs_ref, h0_ref, cw_ref, cb_ref, wga_ref, bga_ref, wgx_ref, bgx_ref,
                  lam_ref, hg_ref, nc_ref, nh_ref, ext_ref, hc_ref, a_ref, b_ref, h_ref, *, tt, blk):
    ti = pl.program_id(2)
    tail = CONV_W - 1
    base = 8

    @pl.when(ti == 0)
    def _():
        ext_ref[base - tail:base, :] = cs_ref[...]
        hc_ref[...] = h0_ref[...]

    ext_ref[base:base + tt, :] = xb_ref[...]
    xconv = cb_ref[...]
    for k in range(CONV_W):
        xconv = xconv + ext_ref[base - tail + k:base - tail + k + tt, :] * cw_ref[k:k + 1, :]
    new_tail = ext_ref[base + tt - tail:base + tt, :]
    ext_ref[base - tail:base, :] = new_tail

    r_parts, i_parts = [], []
    for hh in range(xconv.shape[1] // blk):
        xh = xconv[:, hh * blk:(hh + 1) * blk].astype(BF16)
        r_parts.append(jnp.dot(xh, wga_ref[hh], preferred_element_type=F32))
        i_parts.append(jnp.dot(xh, wgx_ref[hh], preferred_element_type=F32))
    r = jax.nn.sigmoid(jnp.concatenate(r_parts, axis=1) + bga_ref[...])
    i = jax.nn.sigmoid(jnp.concatenate(i_parts, axis=1) + bgx_ref[...])

    neg_lam = -lam_ref[...]
    softplus = jnp.maximum(neg_lam, 0.0) + jnp.log1p(jnp.exp(-jnp.abs(neg_lam)))
    a = jnp.exp((-LRU_C * softplus) * r)
    a_ref[...] = a
    b_ref[...] = jnp.sqrt(1.0 - a * a) * (i * xconv)

    def step(t, h):
        h = a_ref[pl.ds(t, 1), :] * h + b_ref[pl.ds(t, 1), :]
        h_ref[pl.ds(t, 1), :] = h
        return h

    h_last = lax.fori_loop(0, tt, step, hc_ref[...], unroll=min(tt, 8))
    hc_ref[...] = h_last
    hg_ref[...] = (h_ref[...] * jax.nn.silu(gate_ref[...])).astype(hg_ref.dtype)

    @pl.when(ti == pl.num_programs(2) - 1)
    def _():
        nc_ref[...] = new_tail
        nh_ref[...] = h_last


def _rglru_call(proj, conv_state, h0, conv_w, conv_b, w_ga, b_ga, w_gx, b_gx, lam):
    n, t, c2 = proj.shape
    c = c2 // 2
    heads, blk, _ = w_ga.shape
    cb = _tile(c, max(512, blk))
    assert cb % blk == 0 and blk % LANES == 0
    tt = _tile(t, 256)
    ncb = c // cb
    row = lambda a: a.reshape(1, c)
    vec = pl.BlockSpec((1, cb), lambda j, b, i: (0, j))
    wspec = pl.BlockSpec((cb // blk, blk, blk), lambda j, b, i: (j, 0, 0))
    hg, nc, nh = pl.pallas_call(
        functools.partial(_rglru_kernel, tt=tt, blk=blk),
        grid=(ncb, n, t // tt),
        in_specs=[
            pl.BlockSpec((None, tt, cb), lambda j, b, i: (b, i, j)),
            pl.BlockSpec((None, tt, cb), lambda j, b, i: (b, i, ncb + j)),
            pl.BlockSpec((None, CONV_W - 1, cb), lambda j, b, i: (b, 0, j)),
            pl.BlockSpec((None, 1, cb), lambda j, b, i: (b, 0, j)),
            pl.BlockSpec((CONV_W, cb), lambda j, b, i: (0, j)),
            vec, wspec, vec, wspec, vec, vec,
        ],
        out_specs=[
            pl.BlockSpec((None, tt, cb), lambda j, b, i: (b, i, j)),
            pl.BlockSpec((None, CONV_W - 1, cb), lambda j, b, i: (b, 0, j)),
            pl.BlockSpec((None, 1, cb), lambda j, b, i: (b, 0, j)),
        ],
        out_shape=[
            jax.ShapeDtypeStruct((n, t, c), BF16),
            jax.ShapeDtypeStruct((n, CONV_W - 1, c), F32),
            jax.ShapeDtypeStruct((n, 1, c), F32),
        ],
        scratch_shapes=[
            pltpu.VMEM((8 + tt, cb), F32),
            pltpu.VMEM((1, cb), F32),
            pltpu.VMEM((tt, cb), F32),
            pltpu.VMEM((tt, cb), F32),
            pltpu.VMEM((tt, cb), F32),
        ],
        compiler_params=_params("parallel", "parallel", "arbitrary"),
        name="rglru_core",
    )(proj, proj, conv_state, h0.reshape(n, 1, c), conv_w, row(conv_b), w_ga.astype(BF16), row(b_ga),
      w_gx.astype(BF16), row(b_gx), row(lam))
    return hg, nc, nh.reshape(n, c)


def _scan_rows(a, b, h):
    t, c = a.shape
    a3 = a.reshape(t // 8, 8, c)
    b3 = b.reshape(t // 8, 8, c)
    row = lax.broadcasted_iota(jnp.int32, a3.shape, 1)
    for d in (1, 2, 4):
        keep = row >= d
        a_prev = jnp.where(keep, pltpu.roll(a3, d, axis=1), 1.0)
        b_prev = jnp.where(keep, pltpu.roll(b3, d, axis=1), 0.0)
        b3 = a3 * b_prev + b3
        a3 = a3 * a_prev
    out = []
    for g in range(t // 8):
        hg = a3[g] * h + b3[g]
        out.append(hg)
        h = hg[7:8, :]
    return jnp.concatenate(out, axis=0), h


def _a_front_kernel(xn_ref, wx_ref, wg_ref, cs_ref, h0_ref, cw_ref, cb_ref, wga_ref, bga_ref, wgx_ref, bgx_ref,
                    lam_ref, hg_ref, nc_ref, nh_ref, wxb_ref, wgb_ref, tail_ref, hc_ref, px_ref, pg_ref,
                    xc_ref, xcb_ref, rp_ref, ip_ref, *, tt, sub, blk):
    b = pl.program_id(1)
    ti = pl.program_id(2)
    tail = CONV_W - 1
    n_sub = tt // sub
    cb = px_ref.shape[2]
    n_slab = cb // LANES
    piece = cb // 2

    @pl.when((b == 0) & (ti == 0))
    def _():
        wxb_ref[...] = wx_ref[...].astype(BF16)
        wgb_ref[...] = wg_ref[...].astype(BF16)

    @pl.when(ti == 0)
    def _():
        tail_ref[...] = jnp.concatenate([jnp.zeros((8 - tail, cb), F32), cs_ref[...]], axis=0)
        hc_ref[...] = h0_ref[...]

    neg_lam = -lam_ref[...]
    softplus = jnp.maximum(neg_lam, 0.0) + jnp.log1p(jnp.exp(-jnp.abs(neg_lam)))
    log_a_scale = -LRU_C * softplus
    slabs = [slice(s * LANES, (s + 1) * LANES) for s in range(n_slab)]
    prev8 = [tail_ref[:, cols] for cols in slabs]
    hs = [hc_ref[:, cols] for cols in slabs]

    def project(k, p):
        dst, wsrc = (px_ref, wxb_ref) if p < 2 else (pg_ref, wgb_ref)
        cols = slice((p % 2) * piece, (p % 2 + 1) * piece)
        dst[k, :, cols] = jnp.dot(xn_ref[k * sub:(k + 1) * sub, :], wsrc[:, cols], preferred_element_type=F32)

    for p in range(4):
        project(0, p)
    for k in range(n_sub):
        rows = slice(k * sub, (k + 1) * sub)
        more = k + 1 < n_sub
        if more:
            project(k + 1, 0)
        for s, cols in enumerate(slabs):
            xb = px_ref[k, :, cols]
            ext = jnp.concatenate([prev8[s], xb], axis=0)
            xconv = cb_ref[:, cols]
            for j in range(CONV_W):
                shifted = xb if j == tail else pltpu.roll(ext, tail - j, axis=0)[8:, :]
                xconv = xconv + shifted * cw_ref[j:j + 1, cols]
            prev8[s] = xb[sub - 8:, :]
            xc_ref[:, cols] = xconv
            xcb_ref[:, cols] = xconv.astype(BF16)
        if more:
            project(k + 1, 1)
        for hh in range(cb // blk):
            hcols = slice(hh * blk, (hh + 1) * blk)
            rp_ref[:, hcols] = jnp.dot(xcb_ref[:, hcols], wga_ref[hh], preferred_element_type=F32)
            ip_ref[:, hcols] = jnp.dot(xcb_ref[:, hcols], wgx_ref[hh], preferred_element_type=F32)
        if more:
            project(k + 1, 2)
        for s, cols in enumerate(slabs):
            r = jax.nn.sigmoid(rp_ref[:, cols] + bga_ref[:, cols])
            i = jax.nn.sigmoid(ip_ref[:, cols] + bgx_ref[:, cols])
            a = jnp.exp(log_a_scale[:, cols] * r)
            bb = jnp.sqrt(1.0 - a * a) * (i * xc_ref[:, cols])
            h_all, hs[s] = _scan_rows(a, bb, hs[s])
            hg_ref[rows, cols] = (h_all * jax.nn.silu(pg_ref[k, :, cols])).astype(hg_ref.dtype)
            if more and s == n_slab // 2 - 1:
                project(k + 1, 3)
    for s, cols in enumerate(slabs):
        tail_ref[:, cols] = prev8[s]
        hc_ref[:, cols] = hs[s]

    @pl.when(ti == pl.num_programs(2) - 1)
    def _():
        for s, cols in enumerate(slabs):
            nc_ref[:, cols] = prev8[s][8 - tail:, :]
            nh_ref[:, cols] = hs[s]


def _a_front_call(xn, w_in, layer, conv_state, h0, conv_w, conv_b, w_ga, b_ga, w_gx, b_gx, lam, n, t):
    d = xn.shape[1]
    c = w_in.shape[2] // 2
    heads, blk, _ = w_ga.shape
    cb = max(512, blk)
    tt, sub = 512, 128
    assert c % cb == 0 and cb % blk == 0 and blk % LANES == 0 and t % tt == 0
    ncb = c // cb
    row = lambda a: a.reshape(1, c)
    vec = pl.BlockSpec((1, cb), lambda j, b, i: (0, j))
    wspec = pl.BlockSpec((cb // blk, blk, blk), lambda j, b, i: (j, 0, 0))
    hg, nc, nh = pl.pallas_call(
        functools.partial(_a_front_kernel, tt=tt, sub=sub, blk=blk),
        grid=(ncb, n, t // tt),
        in_specs=[
            pl.BlockSpec((None, tt, d), lambda j, b, i: (b, i, 0)),
            pl.BlockSpec((None, d, cb), lambda j, b, i: (layer, 0, j)),
            pl.BlockSpec((None, d, cb), lambda j, b, i: (layer, 0, ncb + j)),
            pl.BlockSpec((None, CONV_W - 1, cb), lambda j, b, i: (b, 0, j)),
            pl.BlockSpec((None, 1, cb), lambda j, b, i: (b, 0, j)),
            pl.BlockSpec((CONV_W, cb), lambda j, b, i: (0, j)),
            vec, wspec, vec, wspec, vec, vec,
        ],
        out_specs=[
            pl.BlockSpec((None, tt, cb), lambda j, b, i: (b, i, j)),
            pl.BlockSpec((None, CONV_W - 1, cb), lambda j, b, i: (b, 0, j)),
            pl.BlockSpec((None, 1, cb), lambda j, b, i: (b, 0, j)),
        ],
        out_shape=[
            jax.ShapeDtypeStruct((n, t, c), BF16),
            jax.ShapeDtypeStruct((n, CONV_W - 1, c), F32),
            jax.ShapeDtypeStruct((n, 1, c), F32),
        ],
        scratch_shapes=[
            pltpu.VMEM((d, cb), BF16),
            pltpu.VMEM((d, cb), BF16),
            pltpu.VMEM((8, cb), F32),
            pltpu.VMEM((1, cb), F32),
            pltpu.VMEM((tt // sub, sub, cb), F32),
            pltpu.VMEM((tt // sub, sub, cb), F32),
            pltpu.VMEM((sub, cb), F32),
            pltpu.VMEM((sub, cb), BF16),
            pltpu.VMEM((sub, cb), F32),
            pltpu.VMEM((sub, cb), F32),
        ],
        compiler_params=_params("parallel", "arbitrary", "arbitrary"),
        name="a_in_rglru",
    )(xn.reshape(n, t, d), w_in, w_in, conv_state, h0.reshape(n, 1, c), conv_w, row(conv_b),
      w_ga.astype(BF16), row(b_ga), w_gx.astype(BF16), row(b_gx), row(lam))
    return hg.reshape(n * t, c), nc, nh.reshape(n, c)


def _rel_bucket(dist):
    dist = np.asarray(dist)
    d = np.maximum(dist, 1).astype(np.float32)
    large = MAX_EXACT + (np.log(d / MAX_EXACT) / np.log(MAX_DIST / MAX_EXACT)
                         * (N_BUCKETS - MAX_EXACT)).astype(np.int32)
    large = np.minimum(large, N_BUCKETS - 1)
    return np.where(dist < MAX_EXACT, dist, large).astype(np.int32)


def _prompt_bias(rel_bias):
    span = 3 * BAND - 1
    neg = jnp.full((N_SLOTS, BAND - 1), -jnp.inf, F32)
    tables = []
    for g, r in enumerate(DILATIONS):
        tg = rel_bias[:, g * N_SLOTS:(g + 1) * N_SLOTS].astype(F32)
        tb = tg[_rel_bucket(np.arange(BAND + 1) * r)].T
        wr = jnp.concatenate([neg, tb[:, ::-1], neg, neg[:, :1]], axis=1)
        skew = jnp.tile(wr, (1, BAND))[:, :BAND * span].reshape(N_SLOTS, BAND, span)
        tables.append(skew[:, :, BAND - 1:])
    return jnp.stack(tables)


def _sample_bias(rel_bias):
    tables = []
    for g, r in enumerate(DILATIONS):
        steps = np.concatenate([BAND - np.arange(BAND), np.zeros(BAND, np.int64)])
        tg = rel_bias[:, g * N_SLOTS:(g + 1) * N_SLOTS].astype(F32)
        tables.append(tg[_rel_bucket(steps * r)].T)
    return jnp.stack(tables)


def _attn_prompt_kernel(q_ref, k_ref, v_ref, bias_ref, gate_ref, o_ref, acc_ref, m_ref, l_ref, *, t):
    g = pl.program_id(2)

    def unit(start, gi, r, has_prev):
        nk = 2 * BAND if has_prev else BAND
        kstart = start - BAND * r if has_prev else start
        rows = lambda s0, cnt: pl.ds(s0, cnt, stride=r) if r > 1 else pl.ds(s0, cnt)
        k = k_ref[rows(kstart, nk), :].astype(BF16)
        v = v_ref[rows(kstart, nk), :].astype(BF16)
        q = jnp.concatenate([q_ref[s, rows(start, BAND), :] for s in range(GQA)], axis=0).astype(BF16)
        sc = lax.dot_general(q, k, (((1,), (1,)), ((), ())), preferred_element_type=F32) * SM_SCALE
        sc = sc + bias_ref[:, :, 2 * BAND - nk:].reshape(GQA * BAND, nk)
        m = jnp.max(sc, axis=-1, keepdims=True)
        p = jnp.exp(sc - m)
        l = jnp.sum(p, axis=-1, keepdims=True)
        pv = jnp.dot(p.astype(BF16), v, preferred_element_type=F32)
        lane = lax.broadcasted_iota(jnp.int32, (BAND, LANES), 1)
        m_tile = jnp.zeros((BAND, LANES), F32)
        l_tile = jnp.zeros((BAND, LANES), F32)
        for s in range(GQA):
            blk = slice(s * BAND, (s + 1) * BAND)
            acc_ref[gi, s, rows(start, BAND), :] = pv[blk]
            m_tile = jnp.where(lane == s, m[blk], m_tile)
            l_tile = jnp.where(lane == s, l[blk], l_tile)
        m_ref[gi, rows(start, BAND), :] = m_tile
        l_ref[gi, rows(start, BAND), :] = l_tile

    def group_pass(gi, r):
        nb = t // (BAND * r)

        def head_block(cls, carry):
            unit(cls, gi, r, False)
            return carry

        lax.fori_loop(0, r, head_block, 0, unroll=min(r, ATTN_UNROLL))

        def later_block(u, carry):
            cls = u % r
            blk = 1 + u // r
            unit(blk * (BAND * r) + cls, gi, r, True)
            return carry

        if nb > 1:
            lax.fori_loop(0, r * (nb - 1), later_block, 0, unroll=ATTN_UNROLL)

    for gi, r in enumerate(DILATIONS):
        pl.when(g == gi)(functools.partial(group_pass, gi, r))

    @pl.when(g == N_GROUPS - 1)
    def _():
        def combine(c, carry):
            rows = pl.ds(pl.multiple_of(c * BAND, BAND), BAND)
            ms = [m_ref[gi, rows, :] for gi in range(N_GROUPS)]
            mx = functools.reduce(jnp.maximum, ms)
            es = [jnp.exp(mg - mx) for mg in ms]
            den = sum(es[gi] * l_ref[gi, rows, :] for gi in range(N_GROUPS))
            for s in range(GQA):
                num = sum(es[gi][:, s:s + 1] * acc_ref[gi, s, rows, :] for gi in range(N_GROUPS))
                cols = slice(s * HEAD_DIM, (s + 1) * HEAD_DIM)
                o = num / den[:, s:s + 1]
                o_ref[rows, cols] = (o * jax.nn.silu(gate_ref[rows, cols])).astype(o_ref.dtype)
            return carry

        lax.fori_loop(0, t // BAND, combine, 0)


def _attn_prompt_call(q_heads, kv_heads, bias, gate, n, t):
    assert all(t % (BAND * r) == 0 for r in DILATIONS)
    q5 = q_heads.reshape(N_GROUPS, KV_HEADS, GQA, n, t, HEAD_DIM)
    kv6 = kv_heads.reshape(N_GROUPS, 2, KV_HEADS, n, t, HEAD_DIM)
    gate3 = gate.reshape(n, t, ATTN_W)
    kv_spec = lambda which: pl.BlockSpec((None, None, None, None, t, HEAD_DIM),
                                         lambda b, h, g: (g, which, h, b, 0, 0))
    out = pl.pallas_call(
        functools.partial(_attn_prompt_kernel, t=t),
        grid=(n, KV_HEADS, N_GROUPS),
        in_specs=[
            pl.BlockSpec((None, None, GQA, None, t, HEAD_DIM), lambda b, h, g: (g, h, 0, b, 0, 0)),
            kv_spec(0), kv_spec(1),
            pl.BlockSpec((None, GQA, BAND, 2 * BAND), lambda b, h, g: (g, h, 0, 0)),
            pl.BlockSpec((None, t, GQA * HEAD_DIM), lambda b, h, g: (b, 0, h)),
        ],
        out_specs=pl.BlockSpec((None, t, GQA * HEAD_DIM), lambda b, h, g: (b, 0, h)),
        out_shape=jax.ShapeDtypeStruct((n, t, ATTN_W), BF16),
        scratch_shapes=[pltpu.VMEM((N_GROUPS, GQA, t, HEAD_DIM), F32),
                        pltpu.VMEM((N_GROUPS, t, LANES), F32),
                        pltpu.VMEM((N_GROUPS, t, LANES), F32)],
        compiler_params=_params("parallel", "parallel", "arbitrary"),
        name="attn_prompt",
    )(q5, kv6, kv6, bias, gate3)
    return out.reshape(n * t, ATTN_W)


def _attn_sample_kernel(q_ref, kvn_ref, s0_ref, s1_ref, s2_ref, bias_ref, gate_ref, o_ref):
    st_refs = (s0_ref, s1_ref, s2_ref)
    bf = lambda x: x.astype(BF16)
    rnd = lambda x: x.astype(BF16).astype(F32)
    for h in range(KV_HEADS):
        sl = slice(h * GQA, (h + 1) * GQA)
        s_old, s_new, v_old, v_new = [], [], [], []
        for g in range(N_GROUPS):
            q = q_ref[g * N_SLOTS + h * GQA:g * N_SLOTS + (h + 1) * GQA, :]
            k_old = st_refs[g][:, h * HEAD_DIM:(h + 1) * HEAD_DIM]
            v_old.append(st_refs[g][:, (KV_HEADS + h) * HEAD_DIM:(KV_HEADS + h + 1) * HEAD_DIM])
            row = g * 2 * KV_HEADS + h
            k_new = kvn_ref[row:row + 1, :]
            v_new.append(kvn_ref[row + KV_HEADS:row + KV_HEADS + 1, :])
            so = lax.dot_general(bf(q), bf(k_old), (((1,), (1,)), ((), ())), preferred_element_type=F32)
            s_old.append(so * SM_SCALE + bias_ref[g, sl, :BAND])
            sn = jnp.sum(rnd(q) * rnd(k_new), axis=-1, keepdims=True)
            s_new.append(sn * SM_SCALE + bias_ref[g, sl, BAND:BAND + 1])
        m = functools.reduce(jnp.maximum, [jnp.max(x, axis=-1, keepdims=True) for x in s_old] + s_new)
        num = jnp.zeros((GQA, HEAD_DIM), F32)
        den = jnp.zeros((GQA, 1), F32)
        for g in range(N_GROUPS):
            p_old = jnp.exp(s_old[g] - m)
            p_new = jnp.exp(s_new[g] - m)
            den = den + jnp.sum(p_old, axis=-1, keepdims=True) + p_new
            num = num + jnp.dot(bf(p_old), bf(v_old[g]), preferred_element_type=F32) + rnd(p_new) * rnd(v_new[g])
        o_ref[sl, :] = ((num / den) * jax.nn.silu(gate_ref[sl, :])).astype(o_ref.dtype)


def _attn_sample_call(q, kv_new, states, bias, gate):
    n = q.shape[0]
    st_views, st_specs = [], []
    for g, r in enumerate(DILATIONS):
        assert states[g].shape[1] == BAND * r
        st_views.append(states[g].reshape(n, BAND, r * KV_GROUP_W))
        st_specs.append(pl.BlockSpec((None, BAND, KV_GROUP_W), lambda b: (b, 0, 0)))
    per_b = lambda rows: pl.BlockSpec((None, rows, HEAD_DIM), lambda b: (b, 0, 0))
    out = pl.pallas_call(
        _attn_sample_kernel,
        grid=(n,),
        in_specs=[per_b(N_GROUPS * N_SLOTS), per_b(N_GROUPS * 2 * KV_HEADS)] + st_specs + [
            pl.BlockSpec((N_GROUPS, N_SLOTS, 2 * BAND), lambda b: (0, 0, 0)), per_b(N_SLOTS)],
        out_specs=per_b(N_SLOTS),
        out_shape=jax.ShapeDtypeStruct((n, N_SLOTS, HEAD_DIM), BF16),
        compiler_params=_params("parallel"),
        name="attn_sample",
    )(q.reshape(n, N_GROUPS * N_SLOTS, HEAD_DIM), kv_new.reshape(n, N_GROUPS * 2 * KV_HEADS, HEAD_DIM),
      *st_views, bias, gate.reshape(n, N_SLOTS, HEAD_DIM))
    return out.reshape(n, ATTN_W)


def _trunk(x, conv_state, h_state, kv_bufs, w):
    n, t, d = x.shape
    m = n * t
    n_a = w["a_w_in"].shape[0]
    n_b = w["b_w_in"].shape[0]
    prompt = kv_bufs is None
    x2 = x.reshape(m, d)
    (xn,) = _norm_call(x2, [w["a_pre_g"][0]])
    new_conv, new_h = [], []
    for l in range(n_a):
        mixer = (w["a_conv_w"][l], w["a_conv_b"][l], w["a_w_gate_a"][l], w["a_b_gate_a"][l],
                 w["a_w_gate_x"][l], w["a_b_gate_x"][l], w["a_lambda"][l])
        if prompt:
            hg, nc, nh = _a_front_call(xn, w["a_w_in"], l, conv_state[l], h_state[l], *mixer, n, t)
        else:
            proj = _mm_call(xn, w["a_w_in"], layer=l, name="a_in_proj")
            hg, nc, nh = _rglru_call(proj.reshape(n, t, -1), conv_state[l], h_state[l], *mixer)
            hg = hg.reshape(m, -1)
        new_conv.append(nc)
        new_h.append(nh)
        y = _mm_call(hg, w["a_w_out"], layer=l, name="a_out_proj")
        gains = [w["a_pre_g"][l + 1]] if l + 1 < n_a else [w["kv_norm_g"], w["b_pre_g"][0]]
        x2, xns = _resnorm_call(y, x2, w["a_post_g"][l], gains)
        xn = xns[0]
    if prompt:
        kv, kv_heads = _mm_call(xn, w["w_kv"], heads=True, name="kv_proj")
        kv4 = kv.reshape(n, t, N_GROUPS, 2, KV_HEADS, HEAD_DIM)
        new_kv = [kv4[:, t - min(WINDOWS[g], t):, g] for g in range(N_GROUPS)]
        bias = _prompt_bias(w["rel_bias"])
    else:
        kv = _mm_call(xn, w["w_kv"], name="kv_proj")
        kv4 = kv.reshape(n, t, N_GROUPS, 2, KV_HEADS, HEAD_DIM)
        new_kv = [jnp.concatenate([kv_bufs[g][:, t:], kv4[:, :, g]], axis=1) for g in range(N_GROUPS)]
        bias = _sample_bias(w["rel_bias"])
    xn = xns[1]
    for l in range(n_b):
        gate = _mm_call(xn, w["b_w_in"], layer=l, cols=(Q_W, ATTN_W), name="gate_proj")
        if prompt:
            q_heads = _mm_call(xn, w["b_w_in"], layer=l, cols=(0, Q_W), natural=False, heads=True, name="q_proj")
            og = _attn_prompt_call(q_heads, kv_heads, bias, gate, n, t)
        else:
            assert t == 1
            q = _mm_call(xn, w["b_w_in"], layer=l, cols=(0, Q_W), name="q_proj")
            og = _attn_sample_call(q, kv, kv_bufs, bias, gate)
        y = _mm_call(og, w["b_w_out"], layer=l, name="b_out_proj")
        gains = [w["b_pre_g"][l + 1]] if l + 1 < n_b else []
        x2, xns = _resnorm_call(y, x2, w["b_post_g"][l], gains)
        xn = xns[0] if xns else None
    return (x2.reshape(n, t, d), jnp.stack(new_conv), jnp.stack(new_h), *new_kv)


def kernel(x_prompt, x_sample, state_conv, state_h, state_kv_w128, state_kv_w512, state_kv_w2048,
           a_pre_g, a_w_in, a_conv_w, a_conv_b, a_w_gate_a, a_b_gate_a, a_w_gate_x, a_b_gate_x,
           a_lambda, a_w_out, a_post_g, kv_norm_g, w_kv, rel_bias, b_pre_g, b_w_in, b_w_out, b_post_g):
    w = dict(
        a_pre_g=a_pre_g, a_w_in=a_w_in, a_conv_w=a_conv_w, a_conv_b=a_conv_b,
        a_w_gate_a=a_w_gate_a, a_b_gate_a=a_b_gate_a, a_w_gate_x=a_w_gate_x, a_b_gate_x=a_b_gate_x,
        a_lambda=a_lambda, a_w_out=a_w_out, a_post_g=a_post_g, kv_norm_g=kv_norm_g,
        w_kv=w_kv, rel_bias=rel_bias, b_pre_g=b_pre_g, b_w_in=b_w_in, b_w_out=b_w_out, b_post_g=b_post_g)
    nb_p = x_prompt.shape[0]
    n_a = a_w_in.shape[0]
    d_rnn = a_conv_w.shape[-1]
    conv0 = jnp.zeros((n_a, nb_p, CONV_W - 1, d_rnn), F32)
    h0 = jnp.zeros((n_a, nb_p, d_rnn), F32)
    prompt_out = _trunk(x_prompt, conv0, h0, None, w)
    sample_out = _trunk(x_sample, state_conv, state_h, (state_kv_w128, state_kv_w512, state_kv_w2048), w)
    return (prompt_out[0], sample_out[0], *prompt_out[1:], *sample_out[1:])
```

```python
import functools
import math

import numpy as np
import jax
import jax.numpy as jnp
from jax import lax
from jax.experimental import pallas as pl
from jax.experimental.pallas import tpu as pltpu

F32 = jnp.float32
BF16 = jnp.bfloat16

EPS = 1e-6
LRU_C = 8.0
CONV_W = 4
HEAD_DIM = 128
N_SLOTS = 16
KV_HEADS = 4
GQA = N_SLOTS // KV_HEADS
WINDOWS = (128, 512, 2048)
DILATIONS = (1, 4, 16)
N_GROUPS = 3
BAND = 128
ATTN_W = N_SLOTS * HEAD_DIM
Q_W = N_GROUPS * ATTN_W
KV_GROUP_W = 2 * KV_HEADS * HEAD_DIM
N_BUCKETS = 32
MAX_EXACT = 16
MAX_DIST = 2048
SM_SCALE = HEAD_DIM ** -0.5

ATTN_UNROLL = 4
LANES = 128
VMEM_LIMIT = 56 * 1024 * 1024


def _params(*sem):
    return pltpu.CompilerParams(dimension_semantics=sem, vmem_limit_bytes=VMEM_LIMIT)


def _tile(dim, pref):
    if dim <= pref:
        return dim
    t = pref
    while dim % t:
        t //= 2
    return t


def _rms(x, g):
    return x * lax.rsqrt(jnp.mean(x * x, axis=-1, keepdims=True) + EPS) * g


def _norm_kernel(x_ref, g_ref, *o_refs):
    x = x_ref[...]
    xhat = x * lax.rsqrt(jnp.mean(x * x, axis=-1, keepdims=True) + EPS)
    for k, o_ref in enumerate(o_refs):
        o_ref[...] = (xhat * g_ref[k:k + 1, :]).astype(o_ref.dtype)


def _norm_call(x, gains):
    m, d = x.shape
    tm = _tile(m, 256)
    g = jnp.stack(gains)
    ng = len(gains)
    row = pl.BlockSpec((tm, d), lambda i: (i, 0))
    return pl.pallas_call(
        _norm_kernel,
        grid=(m // tm,),
        in_specs=[row, pl.BlockSpec((ng, d), lambda i: (0, 0))],
        out_specs=[row] * ng,
        out_shape=[jax.ShapeDtypeStruct((m, d), BF16)] * ng,
        compiler_params=_params("parallel"),
        name="rmsnorm",
    )(x, g)


def _resnorm_kernel(y_ref, x_ref, gp_ref, g_ref, xo_ref, *o_refs):
    xn = x_ref[...] + _rms(y_ref[...], gp_ref[...])
    xo_ref[...] = xn
    if o_refs:
        xhat = xn * lax.rsqrt(jnp.mean(xn * xn, axis=-1, keepdims=True) + EPS)
        for k, o_ref in enumerate(o_refs):
            o_ref[...] = (xhat * g_ref[k:k + 1, :]).astype(o_ref.dtype)


def _resnorm_call(y, x, g_post, gains):
    m, d = x.shape
    tm = _tile(m, 256)
    ng = len(gains)
    g = jnp.stack(gains) if gains else jnp.ones((1, d), F32)
    row = pl.BlockSpec((tm, d), lambda i: (i, 0))
    outs = pl.pallas_call(
        _resnorm_kernel,
        grid=(m // tm,),
        in_specs=[row, row, pl.BlockSpec((1, d), lambda i: (0, 0)),
                  pl.BlockSpec((g.shape[0], d), lambda i: (0, 0))],
        out_specs=[row] * (1 + ng),
        out_shape=[jax.ShapeDtypeStruct((m, d), F32)] + [jax.ShapeDtypeStruct((m, d), BF16)] * ng,
        compiler_params=_params("parallel"),
        name="residual_rmsnorm",
    )(y, x, g_post.reshape(1, d), g)
    return outs[0], list(outs[1:])


def _mm_kernel(a_ref, w_ref, *refs, natural, heads):
    *o_refs, wb_ref = refs

    @pl.when(pl.program_id(1) == 0)
    def _():
        wb_ref[...] = w_ref[...].astype(BF16)

    y = jnp.dot(a_ref[...], wb_ref[...], preferred_element_type=F32)
    k = 0
    if natural:
        o_refs[k][...] = y
        k += 1
    if heads:
        for h in range(y.shape[1] // LANES):
            o_refs[k][h] = y[:, h * LANES:(h + 1) * LANES]


def _mm_call(a, w, *, layer=0, cols=None, natural=True, heads=False, split=1, tm=1024, tn=512, name="matmul"):
    if w.ndim == 2:
        w = w[None]
    m, kdim = a.shape
    col0, n = cols if cols is not None else (0, w.shape[2])
    tm = _tile(m, tm)
    tn = _tile(n // split, tn)
    assert col0 % tn == 0
    out_specs, out_shape = [], []
    if natural and split == 1:
        out_specs.append(pl.BlockSpec((tm, tn), lambda j, i: (i, j)))
        out_shape.append(jax.ShapeDtypeStruct((m, n), F32))
    elif natural:
        per = n // split // tn
        out_specs.append(pl.BlockSpec((None, tm, tn), lambda j, i: (j // per, i, j % per)))
        out_shape.append(jax.ShapeDtypeStruct((split, m, n // split), F32))
    if heads:
        out_specs.append(pl.BlockSpec((tn // LANES, tm, LANES), lambda j, i: (j, i, 0)))
        out_shape.append(jax.ShapeDtypeStruct((n // LANES, m, LANES), F32))
    outs = pl.pallas_call(
        functools.partial(_mm_kernel, natural=natural, heads=heads),
        grid=(n // tn, m // tm),
        in_specs=[pl.BlockSpec((tm, kdim), lambda j, i: (i, 0)),
                  pl.BlockSpec((None, kdim, tn), lambda j, i: (layer, 0, col0 // tn + j))],
        out_specs=out_specs,
        out_shape=out_shape,
        scratch_shapes=[pltpu.VMEM((kdim, tn), BF16)],
        compiler_params=_params("parallel", "arbitrary"),
        name=name,
    )(a, w)
    return outs if len(outs) > 1 else outs[0]


def _rglru_kernel(xb_ref, gate_ref, cs_ref, h0_ref, cw_ref, cb_ref, wga_ref, bga_ref, wgx_ref, bgx_ref,
                  lam_ref, hg_ref, nc_ref, nh_ref, *, blk):
    tail = CONV_W - 1
    xb = xb_ref[...]
    xconv = cb_ref[...]
    for k in range(tail):
        xconv = xconv + cs_ref[:, k, :] * cw_ref[k:k + 1, :]
        if k:
            nc_ref[:, k - 1, :] = cs_ref[:, k, :]
    xconv = xconv + xb * cw_ref[tail:tail + 1, :]
    nc_ref[:, tail - 1, :] = xb

    r_parts, i_parts = [], []
    for hh in range(xconv.shape[1] // blk):
        xh = xconv[:, hh * blk:(hh + 1) * blk].astype(BF16)
        r_parts.append(jnp.dot(xh, wga_ref[hh], preferred_element_type=F32))
        i_parts.append(jnp.dot(xh, wgx_ref[hh], preferred_element_type=F32))
    r = jax.nn.sigmoid(jnp.concatenate(r_parts, axis=1) + bga_ref[...])
    i = jax.nn.sigmoid(jnp.concatenate(i_parts, axis=1) + bgx_ref[...])

    neg_lam = -lam_ref[...]
    softplus = jnp.maximum(neg_lam, 0.0) + jnp.log1p(jnp.exp(-jnp.abs(neg_lam)))
    a = jnp.exp((-LRU_C * softplus) * r)
    h = a * h0_ref[...] + jnp.sqrt(1.0 - a * a) * (i * xconv)
    nh_ref[...] = h
    hg_ref[...] = (h * jax.nn.silu(gate_ref[...])).astype(hg_ref.dtype)


def _rglru_step_call(proj, conv_state, h0, conv_w, conv_b, w_ga, b_ga, w_gx, b_gx, lam):
    n, c2 = proj.shape
    c = c2 // 2
    heads, blk, _ = w_ga.shape
    cb = _tile(c, max(1024, blk))
    assert cb % blk == 0 and blk % LANES == 0
    ncb = c // cb
    row = lambda a: a.reshape(1, c)
    vec = pl.BlockSpec((1, cb), lambda j: (0, j))
    rows = pl.BlockSpec((n, cb), lambda j: (0, j))
    conv_rows = pl.BlockSpec((n, CONV_W - 1, cb), lambda j: (0, 0, j))
    wspec = pl.BlockSpec((cb // blk, blk, blk), lambda j: (j, 0, 0))
    return pl.pallas_call(
        functools.partial(_rglru_kernel, blk=blk),
        grid=(ncb,),
        in_specs=[rows, pl.BlockSpec((n, cb), lambda j: (0, ncb + j)), conv_rows, rows,
                  pl.BlockSpec((CONV_W, cb), lambda j: (0, j)), vec, wspec, vec, wspec, vec, vec],
        out_specs=[rows, conv_rows, rows],
        out_shape=[
            jax.ShapeDtypeStruct((n, c), BF16),
            jax.ShapeDtypeStruct((n, CONV_W - 1, c), F32),
            jax.ShapeDtypeStruct((n, c), F32),
        ],
        compiler_params=_params("parallel"),
        name="rglru_step",
    )(proj, proj, conv_state, h0, conv_w, row(conv_b), w_ga.astype(BF16), row(b_ga),
      w_gx.astype(BF16), row(b_gx), row(lam))


def _scan_rows(a, b, h):
    t, c = a.shape
    a3 = a.reshape(t // 8, 8, c)
    b3 = b.reshape(t // 8, 8, c)
    row = lax.broadcasted_iota(jnp.int32, a3.shape, 1)
    for d in (1, 2, 4):
        keep = row >= d
        a_prev = jnp.where(keep, pltpu.roll(a3, d, axis=1), 1.0)
        b_prev = jnp.where(keep, pltpu.roll(b3, d, axis=1), 0.0)
        b3 = a3 * b_prev + b3
        a3 = a3 * a_prev
    out = []
    for g in range(t // 8):
        hg = a3[g] * h + b3[g]
        out.append(hg)
        h = hg[7:8, :]
    return jnp.concatenate(out, axis=0), h


def _a_front_kernel(xn_ref, wx_ref, wg_ref, cs_ref, h0_ref, cw_ref, cb_ref, wga_ref, bga_ref, wgx_ref, bgx_ref,
                    lam_ref, hg_ref, nc_ref, nh_ref, wb_ref, tail_ref, hc_ref, pr_ref, *, tt, sub, blk):
    b = pl.program_id(1)
    ti = pl.program_id(2)
    tail = CONV_W - 1
    n_sub = tt // sub
    cb = cs_ref.shape[1]

    @pl.when((b == 0) & (ti == 0))
    def _():
        wb_ref[:, :cb] = wx_ref[...].astype(BF16)
        wb_ref[:, cb:] = wg_ref[...].astype(BF16)

    @pl.when(ti == 0)
    def _():
        tail_ref[...] = jnp.concatenate([jnp.zeros((8 - tail, cb), F32), cs_ref[...]], axis=0)
        hc_ref[...] = h0_ref[...]

    neg_lam = -lam_ref[...]
    softplus = jnp.maximum(neg_lam, 0.0) + jnp.log1p(jnp.exp(-jnp.abs(neg_lam)))
    log_a_scale = -LRU_C * softplus
    prev8 = tail_ref[...]
    h = hc_ref[...]

    def project(k):
        pr_ref[k] = jnp.dot(xn_ref[k * sub:(k + 1) * sub, :], wb_ref[...], preferred_element_type=F32)

    project(0)
    for k in range(n_sub):
        if k + 1 < n_sub:
            project(k + 1)
        xb = pr_ref[k, :, :cb]
        ext = jnp.concatenate([prev8, xb], axis=0)
        prev8 = xb[sub - 8:, :]
        ext1 = pltpu.roll(ext, 1, axis=0)
        pair = ext * cw_ref[1:2, :] + ext1 * cw_ref[0:1, :]
        xconv = (cb_ref[...] + pltpu.roll(pair, 2, axis=0)[8:, :]) + (ext1[8:, :] * cw_ref[2:3, :] + xb * cw_ref[3:4, :])
        r_parts, i_parts = [], []
        for hh in range(cb // blk):
            xh = xconv[:, hh * blk:(hh + 1) * blk].astype(BF16)
            r_parts.append(jnp.dot(xh, wga_ref[hh], preferred_element_type=F32))
            i_parts.append(jnp.dot(xh, wgx_ref[hh], preferred_element_type=F32))
        r = jax.nn.sigmoid(jnp.concatenate(r_parts, axis=1) + bga_ref[...])
        i = jax.nn.sigmoid(jnp.concatenate(i_parts, axis=1) + bgx_ref[...])
        a = jnp.exp(log_a_scale * r)
        bb = jnp.sqrt(1.0 - a * a) * (i * xconv)
        h_all, h = _scan_rows(a, bb, h)
        hg_ref[k * sub:(k + 1) * sub, :] = (h_all * jax.nn.silu(pr_ref[k, :, cb:])).astype(hg_ref.dtype)
    tail_ref[...] = prev8
    hc_ref[...] = h

    @pl.when(ti == pl.num_programs(2) - 1)
    def _():
        nc_ref[...] = prev8[8 - tail:, :]
        nh_ref[...] = h


def _a_front_call(xn, w_in, layer, conv_state, h0, conv_w, conv_b, w_ga, b_ga, w_gx, b_gx, lam, n, t):
    d = xn.shape[1]
    c = w_in.shape[2] // 2
    heads, blk, _ = w_ga.shape
    cb = max(512, blk)
    tt, sub = 512, 128
    assert c % cb == 0 and cb % blk == 0 and blk % LANES == 0 and t % tt == 0
    ncb = c // cb
    row = lambda a: a.reshape(1, c)
    vec = pl.BlockSpec((1, cb), lambda j, b, i: (0, j))
    wspec = pl.BlockSpec((cb // blk, blk, blk), lambda j, b, i: (j, 0, 0))
    hg, nc, nh = pl.pallas_call(
        functools.partial(_a_front_kernel, tt=tt, sub=sub, blk=blk),
        grid=(ncb, n, t // tt),
        in_specs=[
            pl.BlockSpec((None, tt, d), lambda j, b, i: (b, i, 0)),
            pl.BlockSpec((None, d, cb), lambda j, b, i: (layer, 0, j)),
            pl.BlockSpec((None, d, cb), lambda j, b, i: (layer, 0, ncb + j)),
            pl.BlockSpec((None, CONV_W - 1, cb), lambda j, b, i: (b, 0, j)),
            pl.BlockSpec((None, 1, cb), lambda j, b, i: (b, 0, j)),
            pl.BlockSpec((CONV_W, cb), lambda j, b, i: (0, j)),
            vec, wspec, vec, wspec, vec, vec,
        ],
        out_specs=[
            pl.BlockSpec((None, tt, cb), lambda j, b, i: (b, i, j)),
            pl.BlockSpec((None, CONV_W - 1, cb), lambda j, b, i: (b, 0, j)),
            pl.BlockSpec((None, 1, cb), lambda j, b, i: (b, 0, j)),
        ],
        out_shape=[
            jax.ShapeDtypeStruct((n, t, c), BF16),
            jax.ShapeDtypeStruct((n, CONV_W - 1, c), F32),
            jax.ShapeDtypeStruct((n, 1, c), F32),
        ],
        scratch_shapes=[
            pltpu.VMEM((d, 2 * cb), BF16),
            pltpu.VMEM((8, cb), F32),
            pltpu.VMEM((1, cb), F32),
            pltpu.VMEM((tt // sub, sub, 2 * cb), F32),
        ],
        compiler_params=_params("parallel", "arbitrary", "arbitrary"),
        name="a_in_rglru",
    )(xn.reshape(n, t, d), w_in, w_in, conv_state, h0.reshape(n, 1, c), conv_w, row(conv_b),
      w_ga.astype(BF16), row(b_ga), w_gx.astype(BF16), row(b_gx), row(lam))
    return hg.reshape(n * t, c), nc, nh.reshape(n, c)


def _rel_bucket(dist):
    dist = np.asarray(dist)
    d = np.maximum(dist, 1).astype(np.float32)
    large = MAX_EXACT + (np.log(d / MAX_EXACT) / np.log(MAX_DIST / MAX_EXACT)
                         * (N_BUCKETS - MAX_EXACT)).astype(np.int32)
    large = np.minimum(large, N_BUCKETS - 1)
    return np.where(dist < MAX_EXACT, dist, large).astype(np.int32)


def _prompt_bias(rel_bias):
    span = 3 * BAND - 1
    neg = jnp.full((N_SLOTS, BAND - 1), -jnp.inf, F32)
    tables = []
    for g, r in enumerate(DILATIONS):
        tg = rel_bias[:, g * N_SLOTS:(g + 1) * N_SLOTS].astype(F32)
        tb = tg[_rel_bucket(np.arange(BAND + 1) * r)].T
        wr = jnp.concatenate([neg, tb[:, ::-1], neg, neg[:, :1]], axis=1)
        skew = jnp.tile(wr, (1, BAND))[:, :BAND * span].reshape(N_SLOTS, BAND, span)
        tables.append(skew[:, :, BAND - 1:])
    return jnp.stack(tables)


def _sample_bias(rel_bias):
    tables = []
    for g, r in enumerate(DILATIONS):
        steps = np.concatenate([BAND - np.arange(BAND), np.zeros(BAND, np.int64)])
        tg = rel_bias[:, g * N_SLOTS:(g + 1) * N_SLOTS].astype(F32)
        tables.append(tg[_rel_bucket(steps * r)].T)
    return jnp.stack(tables)


def _attn_prompt_kernel(q_ref, k_ref, v_ref, bias_ref, gate_ref, o_ref, acc_ref, m_ref, l_ref, *, t):
    g = pl.program_id(2)

    def unit(start, gi, r, has_prev):
        nk = 2 * BAND if has_prev else BAND
        kstart = start - BAND * r if has_prev else start
        rows = lambda s0, cnt: pl.ds(s0, cnt, stride=r) if r > 1 else pl.ds(s0, cnt)
        k = k_ref[rows(kstart, nk), :].astype(BF16)
        v = v_ref[rows(kstart, nk), :].astype(BF16)
        q = jnp.concatenate([q_ref[s, rows(start, BAND), :] for s in range(GQA)], axis=0).astype(BF16)
        sc = lax.dot_general(q, k, (((1,), (1,)), ((), ())), preferred_element_type=F32) * SM_SCALE
        sc = sc + bias_ref[:, :, 2 * BAND - nk:].reshape(GQA * BAND, nk)
        m = jnp.max(sc, axis=-1, keepdims=True)
        p = jnp.exp(sc - m)
        l = jnp.sum(p, axis=-1, keepdims=True)
        pv = jnp.dot(p.astype(BF16), v, preferred_element_type=F32)
        lane = lax.broadcasted_iota(jnp.int32, (BAND, LANES), 1)
        m_tile = jnp.zeros((BAND, LANES), F32)
        l_tile = jnp.zeros((BAND, LANES), F32)
        for s in range(GQA):
            blk = slice(s * BAND, (s + 1) * BAND)
            acc_ref[gi, s, rows(start, BAND), :] = pv[blk]
            m_tile = jnp.where(lane == s, m[blk], m_tile)
            l_tile = jnp.where(lane == s, l[blk], l_tile)
        m_ref[gi, rows(start, BAND), :] = m_tile
        l_ref[gi, rows(start, BAND), :] = l_tile

    def group_pass(gi, r):
        nb = t // (BAND * r)

        def head_block(cls, carry):
            unit(cls, gi, r, False)
            return carry

        lax.fori_loop(0, r, head_block, 0, unroll=min(r, ATTN_UNROLL))

        def later_block(u, carry):
            cls = u % r
            blk = 1 + u // r
            unit(blk * (BAND * r) + cls, gi, r, True)
            return carry

        if nb > 1:
            lax.fori_loop(0, r * (nb - 1), later_block, 0, unroll=ATTN_UNROLL)

    for gi, r in enumerate(DILATIONS):
        pl.when(g == gi)(functools.partial(group_pass, gi, r))

    @pl.when(g == N_GROUPS - 1)
    def _():
        def combine(c, carry):
            rows = pl.ds(pl.multiple_of(c * BAND, BAND), BAND)
            ms = [m_ref[gi, rows, :] for gi in range(N_GROUPS)]
            mx = functools.reduce(jnp.maximum, ms)
            es = [jnp.exp(mg - mx) for mg in ms]
            den = sum(es[gi] * l_ref[gi, rows, :] for gi in range(N_GROUPS))
            for s in range(GQA):
                num = sum(es[gi][:, s:s + 1] * acc_ref[gi, s, rows, :] for gi in range(N_GROUPS))
                cols = slice(s * HEAD_DIM, (s + 1) * HEAD_DIM)
                o = num / den[:, s:s + 1]
                o_ref[rows, cols] = (o * jax.nn.silu(gate_ref[rows, cols])).astype(o_ref.dtype)
            return carry

        lax.fori_loop(0, t // BAND, combine, 0)


def _attn_prompt_call(q_heads, kv_heads, bias, gate, n, t):
    assert all(t % (BAND * r) == 0 for r in DILATIONS)
    q5 = q_heads.reshape(N_GROUPS, KV_HEADS, GQA, n, t, HEAD_DIM)
    kv6 = kv_heads.reshape(N_GROUPS, 2, KV_HEADS, n, t, HEAD_DIM)
    gate3 = gate.reshape(n, t, ATTN_W)
    kv_spec = lambda which: pl.BlockSpec((None, None, None, None, t, HEAD_DIM),
                                         lambda b, h, g: (g, which, h, b, 0, 0))
    out = pl.pallas_call(
        functools.partial(_attn_prompt_kernel, t=t),
        grid=(n, KV_HEADS, N_GROUPS),
        in_specs=[
            pl.BlockSpec((None, None, GQA, None, t, HEAD_DIM), lambda b, h, g: (g, h, 0, b, 0, 0)),
            kv_spec(0), kv_spec(1),
            pl.BlockSpec((None, GQA, BAND, 2 * BAND), lambda b, h, g: (g, h, 0, 0)),
            pl.BlockSpec((None, t, GQA * HEAD_DIM), lambda b, h, g: (b, 0, h)),
        ],
        out_specs=pl.BlockSpec((None, t, GQA * HEAD_DIM), lambda b, h, g: (b, 0, h)),
        out_shape=jax.ShapeDtypeStruct((n, t, ATTN_W), BF16),
        scratch_shapes=[pltpu.VMEM((N_GROUPS, GQA, t, HEAD_DIM), F32),
                        pltpu.VMEM((N_GROUPS, t, LANES), F32),
                        pltpu.VMEM((N_GROUPS, t, LANES), F32)],
        compiler_params=_params("parallel", "parallel", "arbitrary"),
        name="attn_prompt",
    )(q5, kv6, kv6, bias, gate3)
    return out.reshape(n * t, ATTN_W)


def _attn_sample_kernel(q_ref, kvn_ref, s0_ref, s1_ref, s2_ref, bias_ref, gate_ref, o_ref):
    st_refs = (s0_ref, s1_ref, s2_ref)
    bf = lambda x: x.astype(BF16)
    rnd = lambda x: x.astype(BF16).astype(F32)
    for h in range(KV_HEADS):
        sl = slice(h * GQA, (h + 1) * GQA)
        s_old, s_new, v_old, v_new = [], [], [], []
        for g in range(N_GROUPS):
            q = q_ref[g * N_SLOTS + h * GQA:g * N_SLOTS + (h + 1) * GQA, :]
            k_old = st_refs[g][:, 0, h, :]
            v_old.append(st_refs[g][:, 1, h, :])
            row = g * 2 * KV_HEADS + h
            k_new = kvn_ref[row:row + 1, :]
            v_new.append(kvn_ref[row + KV_HEADS:row + KV_HEADS + 1, :])
            so = lax.dot_general(bf(q), bf(k_old), (((1,), (1,)), ((), ())), preferred_element_type=F32)
            s_old.append(so * SM_SCALE + bias_ref[g, sl, :BAND])
            sn = jnp.sum(rnd(q) * rnd(k_new), axis=-1, keepdims=True)
            s_new.append(sn * SM_SCALE + bias_ref[g, sl, BAND:BAND + 1])
        m = functools.reduce(jnp.maximum, [jnp.max(x, axis=-1, keepdims=True) for x in s_old] + s_new)
        num = jnp.zeros((GQA, HEAD_DIM), F32)
        den = jnp.zeros((GQA, 1), F32)
        for g in range(N_GROUPS):
            p_old = jnp.exp(s_old[g] - m)
            p_new = jnp.exp(s_new[g] - m)
            den = den + jnp.sum(p_old, axis=-1, keepdims=True) + p_new
            num = num + jnp.dot(bf(p_old), bf(v_old[g]), preferred_element_type=F32) + rnd(p_new) * rnd(v_new[g])
        o_ref[sl, :] = ((num / den) * jax.nn.silu(gate_ref[sl, :])).astype(o_ref.dtype)


def _attn_sample_call(q, kv_new, states, bias, gate):
    n = q.shape[0]
    st_views, st_specs = [], []
    for g, r in enumerate(DILATIONS):
        assert states[g].shape[1] == BAND * r
        st_views.append(states[g].reshape(n, BAND, r, 2, KV_HEADS, HEAD_DIM))
        st_specs.append(pl.BlockSpec((None, BAND, None, 2, KV_HEADS, HEAD_DIM), lambda b: (b, 0, 0, 0, 0, 0)))
    per_b = lambda rows: pl.BlockSpec((None, rows, HEAD_DIM), lambda b: (b, 0, 0))
    out = pl.pallas_call(
        _attn_sample_kernel,
        grid=(n,),
        in_specs=[per_b(N_GROUPS * N_SLOTS), per_b(N_GROUPS * 2 * KV_HEADS)] + st_specs + [
            pl.BlockSpec((N_GROUPS, N_SLOTS, 2 * BAND), lambda b: (0, 0, 0)), per_b(N_SLOTS)],
        out_specs=per_b(N_SLOTS),
        out_shape=jax.ShapeDtypeStruct((n, N_SLOTS, HEAD_DIM), BF16),
        compiler_params=_params("parallel"),
        name="attn_sample",
    )(q.reshape(n, N_GROUPS * N_SLOTS, HEAD_DIM), kv_new.reshape(n, N_GROUPS * 2 * KV_HEADS, HEAD_DIM),
      *st_views, bias, gate.reshape(n, N_SLOTS, HEAD_DIM))
    return out.reshape(n, ATTN_W)


def _roll_kernel(*refs):
    n_buf = (len(refs) - 1) // 3
    states, news, outs, sem = refs[:n_buf], refs[n_buf:2 * n_buf], refs[2 * n_buf:3 * n_buf], refs[-1]
    copies = []
    for g in range(n_buf):
        keep = states[g].shape[1] - 1
        copies.append(pltpu.make_async_copy(states[g].at[:, pl.ds(1, keep)], outs[g].at[:, pl.ds(0, keep)],
                                            sem.at[2 * g]))
        copies.append(pltpu.make_async_copy(news[g], outs[g].at[:, pl.ds(keep, 1)], sem.at[2 * g + 1]))
    for c in copies:
        c.start()
    for c in copies:
        c.wait()


def _roll_windows_call(states, news):
    any_spec = pl.BlockSpec(memory_space=pl.ANY)
    n_buf = len(states)
    return pl.pallas_call(
        _roll_kernel,
        in_specs=[any_spec] * (2 * n_buf),
        out_specs=[any_spec] * n_buf,
        out_shape=[jax.ShapeDtypeStruct(s.shape, s.dtype) for s in states],
        scratch_shapes=[pltpu.SemaphoreType.DMA((2 * n_buf,))],
        name="roll_windows",
    )(*states, *news)


def _trunk(x, conv_state, h_state, kv_bufs, w):
    n, t, d = x.shape
    m = n * t
    n_a = w["a_w_in"].shape[0]
    n_b = w["b_w_in"].shape[0]
    prompt = kv_bufs is None
    x2 = x.reshape(m, d)
    (xn,) = _norm_call(x2, [w["a_pre_g"][0]])
    new_conv, new_h = [], []
    for l in range(n_a):
        mixer = (w["a_conv_w"][l], w["a_conv_b"][l], w["a_w_gate_a"][l], w["a_b_gate_a"][l],
                 w["a_w_gate_x"][l], w["a_b_gate_x"][l], w["a_lambda"][l])
        if prompt:
            hg, nc, nh = _a_front_call(xn, w["a_w_in"], l, conv_state[l], h_state[l], *mixer, n, t)
        else:
            assert t == 1
            proj = _mm_call(xn, w["a_w_in"], layer=l, name="a_in_proj")
            hg, nc, nh = _rglru_step_call(proj, conv_state[l], h_state[l], *mixer)
        new_conv.append(nc)
        new_h.append(nh)
        y = _mm_call(hg, w["a_w_out"], layer=l, name="a_out_proj")
        gains = [w["a_pre_g"][l + 1]] if l + 1 < n_a else [w["kv_norm_g"], w["b_pre_g"][0]]
        x2, xns = _resnorm_call(y, x2, w["a_post_g"][l], gains)
        xn = xns[0]
    if prompt:
        kv, kv_heads = _mm_call(xn, w["w_kv"], heads=True, split=N_GROUPS, name="kv_proj")
        new_kv = [kv[g].reshape(n, t, 2, KV_HEADS, HEAD_DIM)[:, t - min(WINDOWS[g], t):] for g in range(N_GROUPS)]
        bias = _prompt_bias(w["rel_bias"])
    else:
        kv = _mm_call(xn, w["w_kv"], name="kv_proj")
        kv4 = kv.reshape(n, t, N_GROUPS, 2, KV_HEADS, HEAD_DIM)
        new_kv = _roll_windows_call(kv_bufs, [kv4[:, :, g] for g in range(N_GROUPS)])
        bias = _sample_bias(w["rel_bias"])
    xn = xns[1]
    for l in range(n_b):
        gate = _mm_call(xn, w["b_w_in"], layer=l, cols=(Q_W, ATTN_W), name="gate_proj")
        if prompt:
            q_heads = _mm_call(xn, w["b_w_in"], layer=l, cols=(0, Q_W), natural=False, heads=True, name="q_proj")
            og = _attn_prompt_call(q_heads, kv_heads, bias, gate, n, t)
        else:
            assert t == 1
            q = _mm_call(xn, w["b_w_in"], layer=l, cols=(0, Q_W), name="q_proj")
            og = _attn_sample_call(q, kv, kv_bufs, bias, gate)
        y = _mm_call(og, w["b_w_out"], layer=l, name="b_out_proj")
        gains = [w["b_pre_g"][l + 1]] if l + 1 < n_b else []
        x2, xns = _resnorm_call(y, x2, w["b_post_g"][l], gains)
        xn = xns[0] if xns else None
    return (x2.reshape(n, t, d), jnp.stack(new_conv), jnp.stack(new_h), *new_kv)


def kernel(x_prompt, x_sample, state_conv, state_h, state_kv_w128, state_kv_w512, state_kv_w2048,
           a_pre_g, a_w_in, a_conv_w, a_conv_b, a_w_gate_a, a_b_gate_a, a_w_gate_x, a_b_gate_x,
           a_lambda, a_w_out, a_post_g, kv_norm_g, w_kv, rel_bias, b_pre_g, b_w_in, b_w_out, b_post_g):
    w = dict(
        a_pre_g=a_pre_g, a_w_in=a_w_in, a_conv_w=a_conv_w, a_conv_b=a_conv_b,
        a_w_gate_a=a_w_gate_a, a_b_gate_a=a_b_gate_a, a_w_gate_x=a_w_gate_x, a_b_gate_x=a_b_gate_x,
        a_lambda=a_lambda, a_w_out=a_w_out, a_post_g=a_post_g, kv_norm_g=kv_norm_g,
        w_kv=w_kv, rel_bias=rel_bias, b_pre_g=b_pre_g, b_w_in=b_w_in, b_w_out=b_w_out, b_post_g=b_post_g)
    nb_p = x_prompt.shape[0]
    n_a = a_w_in.shape[0]
    d_rnn = a_conv_w.shape[-1]
    conv0 = jnp.zeros((n_a, nb_p, CONV_W - 1, d_rnn), F32)
    h0 = jnp.zeros((n_a, nb_p, d_rnn), F32)
    prompt_out = _trunk(x_prompt, conv0, h0, None, w)
    sample_out = _trunk(x_sample, state_conv, state_h, (state_kv_w128, state_kv_w512, state_kv_w2048), w)
    return (prompt_out[0], sample_out[0], *prompt_out[1:], *sample_out[1:])
```

```python
import functools
import math

import numpy as np
import jax
import jax.numpy as jnp
from jax import lax
from jax.experimental import pallas as pl
from jax.experimental.pallas import tpu as pltpu

F32 = jnp.float32
BF16 = jnp.bfloat16

EPS = 1e-6
LRU_C = 8.0
CONV_W = 4
HEAD_DIM = 128
N_SLOTS = 16
KV_HEADS = 4
GQA = N_SLOTS // KV_HEADS
WINDOWS = (128, 512, 2048)
DILATIONS = (1, 4, 16)
N_GROUPS = 3
BAND = 128
ATTN_W = N_SLOTS * HEAD_DIM
Q_W = N_GROUPS * ATTN_W
KV_GROUP_W = 2 * KV_HEADS * HEAD_DIM
N_BUCKETS = 32
MAX_EXACT = 16
MAX_DIST = 2048
SM_SCALE = HEAD_DIM ** -0.5

ATTN_UNROLL = 4
LANES = 128
VMEM_LIMIT = 56 * 1024 * 1024


def _params(*sem):
    return pltpu.CompilerParams(dimension_semantics=sem, vmem_limit_bytes=VMEM_LIMIT)


def _tile(dim, pref):
    if dim <= pref:
        return dim
    t = pref
    while dim % t:
        t //= 2
    return t


def _rms(x, g):
    return x * lax.rsqrt(jnp.mean(x * x, axis=-1, keepdims=True) + EPS) * g


def _norm_kernel(x_ref, g_ref, *o_refs):
    x = x_ref[...]
    xhat = x * lax.rsqrt(jnp.mean(x * x, axis=-1, keepdims=True) + EPS)
    for k, o_ref in enumerate(o_refs):
        o_ref[...] = (xhat * g_ref[k:k + 1, :]).astype(o_ref.dtype)


def _norm_call(x, gains):
    m, d = x.shape
    tm = _tile(m, 256)
    g = jnp.stack(gains)
    ng = len(gains)
    row = pl.BlockSpec((tm, d), lambda i: (i, 0))
    return pl.pallas_call(
        _norm_kernel,
        grid=(m // tm,),
        in_specs=[row, pl.BlockSpec((ng, d), lambda i: (0, 0))],
        out_specs=[row] * ng,
        out_shape=[jax.ShapeDtypeStruct((m, d), BF16)] * ng,
        compiler_params=_params("parallel"),
        name="rmsnorm",
    )(x, g)


def _resnorm_kernel(y_ref, x_ref, gp_ref, g_ref, xo_ref, *o_refs):
    xn = x_ref[...] + _rms(y_ref[...], gp_ref[...])
    xo_ref[...] = xn
    if o_refs:
        xhat = xn * lax.rsqrt(jnp.mean(xn * xn, axis=-1, keepdims=True) + EPS)
        for k, o_ref in enumerate(o_refs):
            o_ref[...] = (xhat * g_ref[k:k + 1, :]).astype(o_ref.dtype)


def _resnorm_call(y, x, g_post, gains):
    m, d = x.shape
    tm = _tile(m, 256)
    ng = len(gains)
    g = jnp.stack(gains) if gains else jnp.ones((1, d), F32)
    row = pl.BlockSpec((tm, d), lambda i: (i, 0))
    outs = pl.pallas_call(
        _resnorm_kernel,
        grid=(m // tm,),
        in_specs=[row, row, pl.BlockSpec((1, d), lambda i: (0, 0)),
                  pl.BlockSpec((g.shape[0], d), lambda i: (0, 0))],
        out_specs=[row] * (1 + ng),
        out_shape=[jax.ShapeDtypeStruct((m, d), F32)] + [jax.ShapeDtypeStruct((m, d), BF16)] * ng,
        compiler_params=_params("parallel"),
        name="residual_rmsnorm",
    )(y, x, g_post.reshape(1, d), g)
    return outs[0], list(outs[1:])


def _mm_kernel(a_ref, w_ref, a2_ref, *refs, natural, heads):
    *o_refs, o2_ref, wb_ref = refs

    @pl.when(pl.program_id(1) == 0)
    def _():
        wb_ref[...] = w_ref[...].astype(BF16)
        o2_ref[...] = jnp.dot(a2_ref[...], wb_ref[...], preferred_element_type=F32)

    y = jnp.dot(a_ref[...], wb_ref[...], preferred_element_type=F32)
    k = 0
    if natural:
        o_refs[k][...] = y
        k += 1
    if heads:
        for h in range(y.shape[1] // LANES):
            o_refs[k][h] = y[:, h * LANES:(h + 1) * LANES]


def _mm_call(a, a2, w, *, layer=0, cols=None, natural=True, heads=False, split=1, tm=1024, tn=512,
             name="matmul"):
    if w.ndim == 2:
        w = w[None]
    m, kdim = a.shape
    m2 = a2.shape[0]
    col0, n = cols if cols is not None else (0, w.shape[2])
    tm = _tile(m, tm)
    tn = _tile(n // split, tn)
    assert col0 % tn == 0
    out_specs, out_shape = [], []
    if natural and split == 1:
        out_specs.append(pl.BlockSpec((tm, tn), lambda j, i: (i, j)))
        out_shape.append(jax.ShapeDtypeStruct((m, n), F32))
    elif natural:
        per = n // split // tn
        out_specs.append(pl.BlockSpec((None, tm, tn), lambda j, i: (j // per, i, j % per)))
        out_shape.append(jax.ShapeDtypeStruct((split, m, n // split), F32))
    if heads:
        out_specs.append(pl.BlockSpec((tn // LANES, tm, LANES), lambda j, i: (j, i, 0)))
        out_shape.append(jax.ShapeDtypeStruct((n // LANES, m, LANES), F32))
    out_specs.append(pl.BlockSpec((m2, tn), lambda j, i: (0, j)))
    out_shape.append(jax.ShapeDtypeStruct((m2, n), F32))
    return pl.pallas_call(
        functools.partial(_mm_kernel, natural=natural, heads=heads),
        grid=(n // tn, m // tm),
        in_specs=[pl.BlockSpec((tm, kdim), lambda j, i: (i, 0)),
                  pl.BlockSpec((None, kdim, tn), lambda j, i: (layer, 0, col0 // tn + j)),
                  pl.BlockSpec((m2, kdim), lambda j, i: (0, 0))],
        out_specs=out_specs,
        out_shape=out_shape,
        scratch_shapes=[pltpu.VMEM((kdim, tn), BF16)],
        compiler_params=_params("parallel", "arbitrary"),
        name=name,
    )(a, w, a2)


def _rglru_kernel(xb_ref, gate_ref, cs_ref, h0_ref, cw_ref, cb_ref, wga_ref, bga_ref, wgx_ref, bgx_ref,
                  lam_ref, hg_ref, nc_ref, nh_ref, *, blk):
    tail = CONV_W - 1
    xb = xb_ref[...]
    xconv = cb_ref[...]
    for k in range(tail):
        xconv = xconv + cs_ref[:, k, :] * cw_ref[k:k + 1, :]
        if k:
            nc_ref[:, k - 1, :] = cs_ref[:, k, :]
    xconv = xconv + xb * cw_ref[tail:tail + 1, :]
    nc_ref[:, tail - 1, :] = xb

    r_parts, i_parts = [], []
    for hh in range(xconv.shape[1] // blk):
        xh = xconv[:, hh * blk:(hh + 1) * blk].astype(BF16)
        r_parts.append(jnp.dot(xh, wga_ref[hh], preferred_element_type=F32))
        i_parts.append(jnp.dot(xh, wgx_ref[hh], preferred_element_type=F32))
    r = jax.nn.sigmoid(jnp.concatenate(r_parts, axis=1) + bga_ref[...])
    i = jax.nn.sigmoid(jnp.concatenate(i_parts, axis=1) + bgx_ref[...])

    neg_lam = -lam_ref[...]
    softplus = jnp.maximum(neg_lam, 0.0) + jnp.log1p(jnp.exp(-jnp.abs(neg_lam)))
    a = jnp.exp((-LRU_C * softplus) * r)
    h = a * h0_ref[...] + jnp.sqrt(1.0 - a * a) * (i * xconv)
    nh_ref[...] = h
    hg_ref[...] = (h * jax.nn.silu(gate_ref[...])).astype(hg_ref.dtype)


def _rglru_step_call(proj, conv_state, h0, conv_w, conv_b, w_ga, b_ga, w_gx, b_gx, lam):
    _, n, c = proj.shape
    heads, blk, _ = w_ga.shape
    cb = _tile(c, max(1024, blk))
    assert cb % blk == 0 and blk % LANES == 0
    ncb = c // cb
    row = lambda a: a.reshape(1, c)
    vec = pl.BlockSpec((1, cb), lambda j: (0, j))
    rows = pl.BlockSpec((n, cb), lambda j: (0, j))
    conv_rows = pl.BlockSpec((n, CONV_W - 1, cb), lambda j: (0, 0, j))
    wspec = pl.BlockSpec((cb // blk, blk, blk), lambda j: (j, 0, 0))
    return pl.pallas_call(
        functools.partial(_rglru_kernel, blk=blk),
        grid=(ncb,),
        in_specs=[pl.BlockSpec((None, n, cb), lambda j: (0, 0, j)), pl.BlockSpec((None, n, cb), lambda j: (1, 0, j)),
                  conv_rows, rows,
                  pl.BlockSpec((CONV_W, cb), lambda j: (0, j)), vec, wspec, vec, wspec, vec, vec],
        out_specs=[rows, conv_rows, rows],
        out_shape=[
            jax.ShapeDtypeStruct((n, c), BF16),
            jax.ShapeDtypeStruct((n, CONV_W - 1, c), F32),
            jax.ShapeDtypeStruct((n, c), F32),
        ],
        compiler_params=_params("parallel"),
        name="rglru_step",
    )(proj, proj, conv_state, h0, conv_w, row(conv_b), w_ga.astype(BF16), row(b_ga),
      w_gx.astype(BF16), row(b_gx), row(lam))


def _scan_rows(a, b, h):
    t, c = a.shape
    a3 = a.reshape(t // 8, 8, c)
    b3 = b.reshape(t // 8, 8, c)
    row = lax.broadcasted_iota(jnp.int32, a3.shape, 1)
    for d in (1, 2, 4):
        keep = row >= d
        a_prev = jnp.where(keep, pltpu.roll(a3, d, axis=1), 1.0)
        b_prev = jnp.where(keep, pltpu.roll(b3, d, axis=1), 0.0)
        b3 = a3 * b_prev + b3
        a3 = a3 * a_prev
    out = []
    for g in range(t // 8):
        hg = a3[g] * h + b3[g]
        out.append(hg)
        h = hg[7:8, :]
    return jnp.concatenate(out, axis=0), h


def _a_front_kernel(xn_ref, wx_ref, wg_ref, xs_ref, cs_ref, h0_ref, cw_ref, cb_ref, wga_ref, bga_ref, wgx_ref,
                    bgx_ref, lam_ref, hg_ref, nc_ref, nh_ref, ps_ref, wb_ref, tail_ref, hc_ref, pr_ref,
                    *, tt, sub, blk):
    b = pl.program_id(1)
    ti = pl.program_id(2)
    tail = CONV_W - 1
    n_sub = tt // sub
    cb = cs_ref.shape[1]

    @pl.when((b == 0) & (ti == 0))
    def _():
        wb_ref[:, :cb] = wx_ref[...].astype(BF16)
        wb_ref[:, cb:] = wg_ref[...].astype(BF16)
        ps = jnp.dot(xs_ref[...], wb_ref[...], preferred_element_type=F32)
        ps_ref[0] = ps[:, :cb]
        ps_ref[1] = ps[:, cb:]

    @pl.when(ti == 0)
    def _():
        tail_ref[...] = jnp.concatenate([jnp.zeros((8 - tail, cb), F32), cs_ref[...]], axis=0)
        hc_ref[...] = h0_ref[...]

    neg_lam = -lam_ref[...]
    softplus = jnp.maximum(neg_lam, 0.0) + jnp.log1p(jnp.exp(-jnp.abs(neg_lam)))
    log_a_scale = -LRU_C * softplus
    prev8 = tail_ref[...]
    h = hc_ref[...]

    def project(k):
        pr_ref[k] = jnp.dot(xn_ref[k * sub:(k + 1) * sub, :], wb_ref[...], preferred_element_type=F32)

    project(0)
    for k in range(n_sub):
        if k + 1 < n_sub:
            project(k + 1)
        xb = pr_ref[k, :, :cb]
        ext = jnp.concatenate([prev8, xb], axis=0)
        prev8 = xb[sub - 8:, :]
        ext1 = pltpu.roll(ext, 1, axis=0)
        pair = ext * cw_ref[1:2, :] + ext1 * cw_ref[0:1, :]
        xconv = (cb_ref[...] + pltpu.roll(pair, 2, axis=0)[8:, :]) + (ext1[8:, :] * cw_ref[2:3, :] + xb * cw_ref[3:4, :])
        r_parts, i_parts = [], []
        for hh in range(cb // blk):
            xh = xconv[:, hh * blk:(hh + 1) * blk].astype(BF16)
            r_parts.append(jnp.dot(xh, wga_ref[hh], preferred_element_type=F32))
            i_parts.append(jnp.dot(xh, wgx_ref[hh], preferred_element_type=F32))
        r = jax.nn.sigmoid(jnp.concatenate(r_parts, axis=1) + bga_ref[...])
        i = jax.nn.sigmoid(jnp.concatenate(i_parts, axis=1) + bgx_ref[...])
        a = jnp.exp(log_a_scale * r)
        bb = jnp.sqrt(1.0 - a * a) * (i * xconv)
        h_all, h = _scan_rows(a, bb, h)
        hg_ref[k * sub:(k + 1) * sub, :] = (h_all * jax.nn.silu(pr_ref[k, :, cb:])).astype(hg_ref.dtype)
    tail_ref[...] = prev8
    hc_ref[...] = h

    @pl.when(ti == pl.num_programs(2) - 1)
    def _():
        nc_ref[...] = prev8[8 - tail:, :]
        nh_ref[...] = h


def _a_front_call(xn, xs, w_in, layer, conv_state, h0, conv_w, conv_b, w_ga, b_ga, w_gx, b_gx, lam, n, t):
    d = xn.shape[1]
    c = w_in.shape[2] // 2
    heads, blk, _ = w_ga.shape
    cb = max(512, blk)
    tt, sub = 512, 128
    assert c % cb == 0 and cb % blk == 0 and blk % LANES == 0 and t % tt == 0
    ncb = c // cb
    ns = xs.shape[0]
    row = lambda a: a.reshape(1, c)
    vec = pl.BlockSpec((1, cb), lambda j, b, i: (0, j))
    wspec = pl.BlockSpec((cb // blk, blk, blk), lambda j, b, i: (j, 0, 0))
    hg, nc, nh, ps = pl.pallas_call(
        functools.partial(_a_front_kernel, tt=tt, sub=sub, blk=blk),
        grid=(ncb, n, t // tt),
        in_specs=[
            pl.BlockSpec((None, tt, d), lambda j, b, i: (b, i, 0)),
            pl.BlockSpec((None, d, cb), lambda j, b, i: (layer, 0, j)),
            pl.BlockSpec((None, d, cb), lambda j, b, i: (layer, 0, ncb + j)),
            pl.BlockSpec((ns, d), lambda j, b, i: (0, 0)),
            pl.BlockSpec((None, CONV_W - 1, cb), lambda j, b, i: (b, 0, j)),
            pl.BlockSpec((None, 1, cb), lambda j, b, i: (b, 0, j)),
            pl.BlockSpec((CONV_W, cb), lambda j, b, i: (0, j)),
            vec, wspec, vec, wspec, vec, vec,
        ],
        out_specs=[
            pl.BlockSpec((None, tt, cb), lambda j, b, i: (b, i, j)),
            pl.BlockSpec((None, CONV_W - 1, cb), lambda j, b, i: (b, 0, j)),
            pl.BlockSpec((None, 1, cb), lambda j, b, i: (b, 0, j)),
            pl.BlockSpec((2, ns, cb), lambda j, b, i: (0, 0, j)),
        ],
        out_shape=[
            jax.ShapeDtypeStruct((n, t, c), BF16),
            jax.ShapeDtypeStruct((n, CONV_W - 1, c), F32),
            jax.ShapeDtypeStruct((n, 1, c), F32),
            jax.ShapeDtypeStruct((2, ns, c), F32),
        ],
        scratch_shapes=[
            pltpu.VMEM((d, 2 * cb), BF16),
            pltpu.VMEM((8, cb), F32),
            pltpu.VMEM((1, cb), F32),
            pltpu.VMEM((tt // sub, sub, 2 * cb), F32),
        ],
        compiler_params=_params("parallel", "arbitrary", "arbitrary"),
        name="a_in_rglru",
    )(xn.reshape(n, t, d), w_in, w_in, xs, conv_state, h0.reshape(n, 1, c), conv_w, row(conv_b),
      w_ga.astype(BF16), row(b_ga), w_gx.astype(BF16), row(b_gx), row(lam))
    return hg.reshape(n * t, c), nc, nh.reshape(n, c), ps


def _rel_bucket(dist):
    dist = np.asarray(dist)
    d = np.maximum(dist, 1).astype(np.float32)
    large = MAX_EXACT + (np.log(d / MAX_EXACT) / np.log(MAX_DIST / MAX_EXACT)
                         * (N_BUCKETS - MAX_EXACT)).astype(np.int32)
    large = np.minimum(large, N_BUCKETS - 1)
    return np.where(dist < MAX_EXACT, dist, large).astype(np.int32)


def _prompt_bias(rel_bias):
    span = 3 * BAND - 1
    neg = jnp.full((N_SLOTS, BAND - 1), -jnp.inf, F32)
    tables = []
    for g, r in enumerate(DILATIONS):
        tg = rel_bias[:, g * N_SLOTS:(g + 1) * N_SLOTS].astype(F32)
        tb = tg[_rel_bucket(np.arange(BAND + 1) * r)].T
        wr = jnp.concatenate([neg, tb[:, ::-1], neg, neg[:, :1]], axis=1)
        skew = jnp.tile(wr, (1, BAND))[:, :BAND * span].reshape(N_SLOTS, BAND, span)
        tables.append(skew[:, :, BAND - 1:])
    return jnp.stack(tables)


def _sample_bias(rel_bias):
    tables = []
    for g, r in enumerate(DILATIONS):
        steps = np.concatenate([BAND - np.arange(BAND), np.zeros(BAND, np.int64)])
        tg = rel_bias[:, g * N_SLOTS:(g + 1) * N_SLOTS].astype(F32)
        tables.append(tg[_rel_bucket(steps * r)].T)
    return jnp.stack(tables)


def _attn_prompt_kernel(q_ref, k_ref, v_ref, bias_ref, gate_ref, o_ref, acc_ref, m_ref, l_ref, *, t):
    g = pl.program_id(2)

    def unit(start, gi, r, has_prev):
        nk = 2 * BAND if has_prev else BAND
        kstart = start - BAND * r if has_prev else start
        rows = lambda s0, cnt: pl.ds(s0, cnt, stride=r) if r > 1 else pl.ds(s0, cnt)
        k = k_ref[rows(kstart, nk), :].astype(BF16)
        v = v_ref[rows(kstart, nk), :].astype(BF16)
        q = jnp.concatenate([q_ref[s, rows(start, BAND), :] for s in range(GQA)], axis=0).astype(BF16)
        sc = lax.dot_general(q, k, (((1,), (1,)), ((), ())), preferred_element_type=F32) * SM_SCALE
        sc = sc + bias_ref[:, :, 2 * BAND - nk:].reshape(GQA * BAND, nk)
        m = jnp.max(sc, axis=-1, keepdims=True)
        p = jnp.exp(sc - m)
        l = jnp.sum(p, axis=-1, keepdims=True)
        pv = jnp.dot(p.astype(BF16), v, preferred_element_type=F32)
        lane = lax.broadcasted_iota(jnp.int32, (BAND, LANES), 1)
        m_tile = jnp.zeros((BAND, LANES), F32)
        l_tile = jnp.zeros((BAND, LANES), F32)
        for s in range(GQA):
            blk = slice(s * BAND, (s + 1) * BAND)
            acc_ref[gi, s, rows(start, BAND), :] = pv[blk]
            m_tile = jnp.where(lane == s, m[blk], m_tile)
            l_tile = jnp.where(lane == s, l[blk], l_tile)
        m_ref[gi, rows(start, BAND), :] = m_tile
        l_ref[gi, rows(start, BAND), :] = l_tile

    def group_pass(gi, r):
        nb = t // (BAND * r)

        def head_block(cls, carry):
            unit(cls, gi, r, False)
            return carry

        lax.fori_loop(0, r, head_block, 0, unroll=min(r, ATTN_UNROLL))

        def later_block(u, carry):
            cls = u % r
            blk = 1 + u // r
            unit(blk * (BAND * r) + cls, gi, r, True)
            return carry

        if nb > 1:
            lax.fori_loop(0, r * (nb - 1), later_block, 0, unroll=ATTN_UNROLL)

    for gi, r in enumerate(DILATIONS):
        pl.when(g == gi)(functools.partial(group_pass, gi, r))

    @pl.when(g == N_GROUPS - 1)
    def _():
        def combine(c, carry):
            rows = pl.ds(pl.multiple_of(c * BAND, BAND), BAND)
            ms = [m_ref[gi, rows, :] for gi in range(N_GROUPS)]
            mx = functools.reduce(jnp.maximum, ms)
            es = [jnp.exp(mg - mx) for mg in ms]
            den = sum(es[gi] * l_ref[gi, rows, :] for gi in range(N_GROUPS))
            for s in range(GQA):
                num = sum(es[gi][:, s:s + 1] * acc_ref[gi, s, rows, :] for gi in range(N_GROUPS))
                cols = slice(s * HEAD_DIM, (s + 1) * HEAD_DIM)
                o = num / den[:, s:s + 1]
                o_ref[rows, cols] = (o * jax.nn.silu(gate_ref[rows, cols])).astype(o_ref.dtype)
            return carry

        lax.fori_loop(0, t // BAND, combine, 0)


def _attn_prompt_call(q_heads, kv_heads, bias, gate, n, t):
    assert all(t % (BAND * r) == 0 for r in DILATIONS)
    q5 = q_heads.reshape(N_GROUPS, KV_HEADS, GQA, n, t, HEAD_DIM)
    kv6 = kv_heads.reshape(N_GROUPS, 2, KV_HEADS, n, t, HEAD_DIM)
    gate3 = gate.reshape(n, t, ATTN_W)
    kv_spec = lambda which: pl.BlockSpec((None, None, None, None, t, HEAD_DIM),
                                         lambda b, h, g: (g, which, h, b, 0, 0))
    out = pl.pallas_call(
        functools.partial(_attn_prompt_kernel, t=t),
        grid=(n, KV_HEADS, N_GROUPS),
        in_specs=[
            pl.BlockSpec((None, None, GQA, None, t, HEAD_DIM), lambda b, h, g: (g, h, 0, b, 0, 0)),
            kv_spec(0), kv_spec(1),
            pl.BlockSpec((None, GQA, BAND, 2 * BAND), lambda b, h, g: (g, h, 0, 0)),
            pl.BlockSpec((None, t, GQA * HEAD_DIM), lambda b, h, g: (b, 0, h)),
        ],
        out_specs=pl.BlockSpec((None, t, GQA * HEAD_DIM), lambda b, h, g: (b, 0, h)),
        out_shape=jax.ShapeDtypeStruct((n, t, ATTN_W), BF16),
        scratch_shapes=[pltpu.VMEM((N_GROUPS, GQA, t, HEAD_DIM), F32),
                        pltpu.VMEM((N_GROUPS, t, LANES), F32),
                        pltpu.VMEM((N_GROUPS, t, LANES), F32)],
        compiler_params=_params("parallel", "parallel", "arbitrary"),
        name="attn_prompt",
    )(q5, kv6, kv6, bias, gate3)
    return out.reshape(n * t, ATTN_W)


def _attn_sample_kernel(q_ref, kvn_ref, s0_ref, s1_ref, s2_ref, bias_ref, gate_ref, o_ref):
    st_refs = (s0_ref, s1_ref, s2_ref)
    bf = lambda x: x.astype(BF16)
    rnd = lambda x: x.astype(BF16).astype(F32)
    for h in range(KV_HEADS):
        sl = slice(h * GQA, (h + 1) * GQA)
        s_old, s_new, v_old, v_new = [], [], [], []
        for g in range(N_GROUPS):
            q = q_ref[g * N_SLOTS + h * GQA:g * N_SLOTS + (h + 1) * GQA, :]
            k_old = st_refs[g][:, 0, h, :]
            v_old.append(st_refs[g][:, 1, h, :])
            row = g * 2 * KV_HEADS + h
            k_new = kvn_ref[row:row + 1, :]
            v_new.append(kvn_ref[row + KV_HEADS:row + KV_HEADS + 1, :])
            so = lax.dot_general(bf(q), bf(k_old), (((1,), (1,)), ((), ())), preferred_element_type=F32)
            s_old.append(so * SM_SCALE + bias_ref[g, sl, :BAND])
            sn = jnp.sum(rnd(q) * rnd(k_new), axis=-1, keepdims=True)
            s_new.append(sn * SM_SCALE + bias_ref[g, sl, BAND:BAND + 1])
        m = functools.reduce(jnp.maximum, [jnp.max(x, axis=-1, keepdims=True) for x in s_old] + s_new)
        num = jnp.zeros((GQA, HEAD_DIM), F32)
        den = jnp.zeros((GQA, 1), F32)
        for g in range(N_GROUPS):
            p_old = jnp.exp(s_old[g] - m)
            p_new = jnp.exp(s_new[g] - m)
            den = den + jnp.sum(p_old, axis=-1, keepdims=True) + p_new
            num = num + jnp.dot(bf(p_old), bf(v_old[g]), preferred_element_type=F32) + rnd(p_new) * rnd(v_new[g])
        o_ref[sl, :] = ((num / den) * jax.nn.silu(gate_ref[sl, :])).astype(o_ref.dtype)


def _attn_sample_call(q, kv_new, states, bias, gate):
    n = q.shape[0]
    st_views, st_specs = [], []
    for g, r in enumerate(DILATIONS):
        assert states[g].shape[1] == BAND * r
        st_views.append(states[g].reshape(n, BAND, r, 2, KV_HEADS, HEAD_DIM))
        st_specs.append(pl.BlockSpec((None, BAND, None, 2, KV_HEADS, HEAD_DIM), lambda b: (b, 0, 0, 0, 0, 0)))
    per_b = lambda rows: pl.BlockSpec((None, rows, HEAD_DIM), lambda b: (b, 0, 0))
    out = pl.pallas_call(
        _attn_sample_kernel,
        grid=(n,),
        in_specs=[per_b(N_GROUPS * N_SLOTS), per_b(N_GROUPS * 2 * KV_HEADS)] + st_specs + [
            pl.BlockSpec((N_GROUPS, N_SLOTS, 2 * BAND), lambda b: (0, 0, 0)), per_b(N_SLOTS)],
        out_specs=per_b(N_SLOTS),
        out_shape=jax.ShapeDtypeStruct((n, N_SLOTS, HEAD_DIM), BF16),
        compiler_params=_params("parallel"),
        name="attn_sample",
    )(q.reshape(n, N_GROUPS * N_SLOTS, HEAD_DIM), kv_new.reshape(n, N_GROUPS * 2 * KV_HEADS, HEAD_DIM),
      *st_views, bias, gate.reshape(n, N_SLOTS, HEAD_DIM))
    return out.reshape(n, ATTN_W)


def _trunks(xp, xs, conv_p, h_p, conv_s, h_s, kv_bufs, w):
    n, t, d = xp.shape
    ns = xs.shape[0]
    assert xs.shape[1] == 1
    n_a = w["a_w_in"].shape[0]
    n_b = w["b_w_in"].shape[0]
    xp2 = xp.reshape(n * t, d)
    xs2 = xs.reshape(ns, d)
    (xnp,) = _norm_call(xp2, [w["a_pre_g"][0]])
    (xns,) = _norm_call(xs2, [w["a_pre_g"][0]])
    conv_out_p, h_out_p, conv_out_s, h_out_s = [], [], [], []
    for l in range(n_a):
        mixer = (w["a_conv_w"][l], w["a_conv_b"][l], w["a_w_gate_a"][l], w["a_b_gate_a"][l],
                 w["a_w_gate_x"][l], w["a_b_gate_x"][l], w["a_lambda"][l])
        hgp, nc, nh, proj_s = _a_front_call(xnp, xns, w["a_w_in"], l, conv_p[l], h_p[l], *mixer, n, t)
        conv_out_p.append(nc)
        h_out_p.append(nh)
        hgs, nc, nh = _rglru_step_call(proj_s, conv_s[l], h_s[l], *mixer)
        conv_out_s.append(nc)
        h_out_s.append(nh)
        yp, ys = _mm_call(hgp, hgs, w["a_w_out"], layer=l, name="a_out_proj")
        gains = [w["a_pre_g"][l + 1]] if l + 1 < n_a else [w["kv_norm_g"], w["b_pre_g"][0]]
        xp2, xnps = _resnorm_call(yp, xp2, w["a_post_g"][l], gains)
        xs2, xnss = _resnorm_call(ys, xs2, w["a_post_g"][l], gains)
        xnp, xns = xnps[0], xnss[0]

    kvp, kvp_heads, kvs = _mm_call(xnp, xns, w["w_kv"], heads=True, split=N_GROUPS, name="kv_proj")
    kv_out_p = []
    for g in range(N_GROUPS):
        keep = min(WINDOWS[g], t)
        kept = kvp[g].reshape(n, t, KV_GROUP_W)[:, t - keep:]
        kv_out_p.append(kept.reshape(n, keep, 2, KV_HEADS, HEAD_DIM))
    kvs4 = kvs.reshape(ns, 1, N_GROUPS, 2, KV_HEADS, HEAD_DIM)
    kv_out_s = [jnp.concatenate([kv_bufs[g][:, 1:], kvs4[:, :, g]], axis=1) for g in range(N_GROUPS)]
    bias_p = _prompt_bias(w["rel_bias"])
    bias_s = _sample_bias(w["rel_bias"])

    xnp, xns = xnps[1], xnss[1]
    for l in range(n_b):
        gate_p, gate_s = _mm_call(xnp, xns, w["b_w_in"], layer=l, cols=(Q_W, ATTN_W), name="gate_proj")
        q_heads, q_s = _mm_call(xnp, xns, w["b_w_in"], layer=l, cols=(0, Q_W), natural=False, heads=True,
                                name="q_proj")
        ogp = _attn_prompt_call(q_heads, kvp_heads, bias_p, gate_p, n, t)
        ogs = _attn_sample_call(q_s, kvs, kv_bufs, bias_s, gate_s)
        yp, ys = _mm_call(ogp, ogs, w["b_w_out"], layer=l, name="b_out_proj")
        gains = [w["b_pre_g"][l + 1]] if l + 1 < n_b else []
        xp2, xnps = _resnorm_call(yp, xp2, w["b_post_g"][l], gains)
        xs2, xnss = _resnorm_call(ys, xs2, w["b_post_g"][l], gains)
        if gains:
            xnp, xns = xnps[0], xnss[0]
    prompt_out = (xp2.reshape(n, t, d), jnp.stack(conv_out_p), jnp.stack(h_out_p), *kv_out_p)
    sample_out = (xs2.reshape(ns, 1, d), jnp.stack(conv_out_s), jnp.stack(h_out_s), *kv_out_s)
    return prompt_out, sample_out


def kernel(x_prompt, x_sample, state_conv, state_h, state_kv_w128, state_kv_w512, state_kv_w2048,
           a_pre_g, a_w_in, a_conv_w, a_conv_b, a_w_gate_a, a_b_gate_a, a_w_gate_x, a_b_gate_x,
           a_lambda, a_w_out, a_post_g, kv_norm_g, w_kv, rel_bias, b_pre_g, b_w_in, b_w_out, b_post_g):
    w = dict(
        a_pre_g=a_pre_g, a_w_in=a_w_in, a_conv_w=a_conv_w, a_conv_b=a_conv_b,
        a_w_gate_a=a_w_gate_a, a_b_gate_a=a_b_gate_a, a_w_gate_x=a_w_gate_x, a_b_gate_x=a_b_gate_x,
        a_lambda=a_lambda, a_w_out=a_w_out, a_post_g=a_post_g, kv_norm_g=kv_norm_g,
        w_kv=w_kv, rel_bias=rel_bias, b_pre_g=b_pre_g, b_w_in=b_w_in, b_w_out=b_w_out, b_post_g=b_post_g)
    nb_p = x_prompt.shape[0]
    n_a = a_w_in.shape[0]
    d_rnn = a_conv_w.shape[-1]
    conv0 = jnp.zeros((n_a, nb_p, CONV_W - 1, d_rnn), F32)
    h0 = jnp.zeros((n_a, nb_p, d_rnn), F32)
    prompt_out, sample_out = _trunks(x_prompt, x_sample, conv0, h0, state_conv, state_h,
                                     (state_kv_w128, state_kv_w512, state_kv_w2048), w)
    return (prompt_out[0], sample_out[0], *prompt_out[1:], *sample_out[1:])
```

```python
import functools
import math

import numpy as np
import jax
import jax.numpy as jnp
from jax import lax
from jax.experimental import pallas as pl
from jax.experimental.pallas import tpu as pltpu

F32 = jnp.float32
BF16 = jnp.bfloat16

EPS = 1e-6
LRU_C = 8.0
CONV_W = 4
HEAD_DIM = 128
N_SLOTS = 16
KV_HEADS = 4
GQA = N_SLOTS // KV_HEADS
WINDOWS = (128, 512, 2048)
DILATIONS = (1, 4, 16)
N_GROUPS = 3
BAND = 128
ATTN_W = N_SLOTS * HEAD_DIM
Q_W = N_GROUPS * ATTN_W
KV_GROUP_W = 2 * KV_HEADS * HEAD_DIM
N_BUCKETS = 32
MAX_EXACT = 16
MAX_DIST = 2048
SM_SCALE = HEAD_DIM ** -0.5

ATTN_UNROLL = 8
LANES = 128
VMEM_LIMIT = 56 * 1024 * 1024


def _params(*sem):
    return pltpu.CompilerParams(dimension_semantics=sem, vmem_limit_bytes=VMEM_LIMIT)


def _tile(dim, pref):
    if dim <= pref:
        return dim
    t = pref
    while dim % t:
        t //= 2
    return t


def _rms(x, g):
    return x * lax.rsqrt(jnp.mean(x * x, axis=-1, keepdims=True) + EPS) * g


def _norm_kernel(x_ref, g_ref, *o_refs):
    x = x_ref[...]
    xhat = x * lax.rsqrt(jnp.mean(x * x, axis=-1, keepdims=True) + EPS)
    for k, o_ref in enumerate(o_refs):
        o_ref[...] = (xhat * g_ref[k:k + 1, :]).astype(o_ref.dtype)


def _norm_call(x, gains):
    m, d = x.shape
    tm = _tile(m, 256)
    g = jnp.stack(gains)
    ng = len(gains)
    row = pl.BlockSpec((tm, d), lambda i: (i, 0))
    return pl.pallas_call(
        _norm_kernel,
        grid=(m // tm,),
        in_specs=[row, pl.BlockSpec((ng, d), lambda i: (0, 0))],
        out_specs=[row] * ng,
        out_shape=[jax.ShapeDtypeStruct((m, d), BF16)] * ng,
        compiler_params=_params("parallel"),
        name="rmsnorm",
    )(x, g)


def _resnorm_kernel(y_ref, x_ref, gp_ref, g_ref, xo_ref, *o_refs):
    xn = x_ref[...] + _rms(y_ref[...], gp_ref[...])
    xo_ref[...] = xn
    if o_refs:
        xhat = xn * lax.rsqrt(jnp.mean(xn * xn, axis=-1, keepdims=True) + EPS)
        for k, o_ref in enumerate(o_refs):
            o_ref[...] = (xhat * g_ref[k:k + 1, :]).astype(o_ref.dtype)


def _resnorm_call(y, x, g_post, gains):
    m, d = x.shape
    tm = _tile(m, 256)
    ng = len(gains)
    g = jnp.stack(gains) if gains else jnp.ones((1, d), F32)
    row = pl.BlockSpec((tm, d), lambda i: (i, 0))
    outs = pl.pallas_call(
        _resnorm_kernel,
        grid=(m // tm,),
        in_specs=[row, row, pl.BlockSpec((1, d), lambda i: (0, 0)),
                  pl.BlockSpec((g.shape[0], d), lambda i: (0, 0))],
        out_specs=[row] * (1 + ng),
        out_shape=[jax.ShapeDtypeStruct((m, d), F32)] + [jax.ShapeDtypeStruct((m, d), BF16)] * ng,
        compiler_params=_params("parallel"),
        name="residual_rmsnorm",
    )(y, x, g_post.reshape(1, d), g)
    return outs[0], list(outs[1:])


def _mm_kernel(a_ref, w_ref, a2_ref, *refs, natural, heads):
    *o_refs, o2_ref, wb_ref = refs

    @pl.when(pl.program_id(1) == 0)
    def _():
        wb_ref[...] = w_ref[...].astype(BF16)
        o2_ref[...] = jnp.dot(a2_ref[...], wb_ref[...], preferred_element_type=F32)

    y = jnp.dot(a_ref[...], wb_ref[...], preferred_element_type=F32)
    k = 0
    if natural:
        o_refs[k][...] = y
        k += 1
    if heads:
        for h in range(y.shape[1] // LANES):
            o_refs[k][h] = y[:, h * LANES:(h + 1) * LANES]


def _mm_call(a, a2, w, *, layer=0, cols=None, natural=True, heads=False, tm=1024, tn=512, name="matmul"):
    if w.ndim == 2:
        w = w[None]
    m, kdim = a.shape
    m2 = a2.shape[0]
    col0, n = cols if cols is not None else (0, w.shape[2])
    tm = _tile(m, tm)
    tn = _tile(n, tn)
    assert col0 % tn == 0
    out_specs, out_shape = [], []
    if natural:
        out_specs.append(pl.BlockSpec((tm, tn), lambda j, i: (i, j)))
        out_shape.append(jax.ShapeDtypeStruct((m, n), F32))
    if heads:
        out_specs.append(pl.BlockSpec((tn // LANES, tm, LANES), lambda j, i: (j, i, 0)))
        out_shape.append(jax.ShapeDtypeStruct((n // LANES, m, LANES), F32))
    out_specs.append(pl.BlockSpec((m2, tn), lambda j, i: (0, j)))
    out_shape.append(jax.ShapeDtypeStruct((m2, n), F32))
    return pl.pallas_call(
        functools.partial(_mm_kernel, natural=natural, heads=heads),
        grid=(n // tn, m // tm),
        in_specs=[pl.BlockSpec((tm, kdim), lambda j, i: (i, 0)),
                  pl.BlockSpec((None, kdim, tn), lambda j, i: (layer, 0, col0 // tn + j)),
                  pl.BlockSpec((m2, kdim), lambda j, i: (0, 0))],
        out_specs=out_specs,
        out_shape=out_shape,
        scratch_shapes=[pltpu.VMEM((kdim, tn), BF16)],
        compiler_params=_params("parallel", "arbitrary"),
        name=name,
    )(a, w, a2)


def _rglru_kernel(xb_ref, gate_ref, cs_ref, h0_ref, cw_ref, cb_ref, wga_ref, bga_ref, wgx_ref, bgx_ref,
                  lam_ref, hg_ref, nc_ref, nh_ref, *, blk):
    tail = CONV_W - 1
    xb = xb_ref[...]
    xconv = cb_ref[...]
    for k in range(tail):
        xconv = xconv + cs_ref[:, k, :] * cw_ref[k:k + 1, :]
        if k:
            nc_ref[:, k - 1, :] = cs_ref[:, k, :]
    xconv = xconv + xb * cw_ref[tail:tail + 1, :]
    nc_ref[:, tail - 1, :] = xb

    r_parts, i_parts = [], []
    for hh in range(xconv.shape[1] // blk):
        xh = xconv[:, hh * blk:(hh + 1) * blk].astype(BF16)
        r_parts.append(jnp.dot(xh, wga_ref[hh], preferred_element_type=F32))
        i_parts.append(jnp.dot(xh, wgx_ref[hh], preferred_element_type=F32))
    r = jax.nn.sigmoid(jnp.concatenate(r_parts, axis=1) + bga_ref[...])
    i = jax.nn.sigmoid(jnp.concatenate(i_parts, axis=1) + bgx_ref[...])

    neg_lam = -lam_ref[...]
    softplus = jnp.maximum(neg_lam, 0.0) + jnp.log1p(jnp.exp(-jnp.abs(neg_lam)))
    a = jnp.exp((-LRU_C * softplus) * r)
    h = a * h0_ref[...] + jnp.sqrt(1.0 - a * a) * (i * xconv)
    nh_ref[...] = h
    hg_ref[...] = (h * jax.nn.silu(gate_ref[...])).astype(hg_ref.dtype)


def _rglru_step_call(proj, conv_state, h0, conv_w, conv_b, w_ga, b_ga, w_gx, b_gx, lam):
    _, n, c = proj.shape
    heads, blk, _ = w_ga.shape
    cb = _tile(c, max(1024, blk))
    assert cb % blk == 0 and blk % LANES == 0
    ncb = c // cb
    row = lambda a: a.reshape(1, c)
    vec = pl.BlockSpec((1, cb), lambda j: (0, j))
    rows = pl.BlockSpec((n, cb), lambda j: (0, j))
    conv_rows = pl.BlockSpec((n, CONV_W - 1, cb), lambda j: (0, 0, j))
    wspec = pl.BlockSpec((cb // blk, blk, blk), lambda j: (j, 0, 0))
    return pl.pallas_call(
        functools.partial(_rglru_kernel, blk=blk),
        grid=(ncb,),
        in_specs=[pl.BlockSpec((None, n, cb), lambda j: (0, 0, j)), pl.BlockSpec((None, n, cb), lambda j: (1, 0, j)),
                  conv_rows, rows,
                  pl.BlockSpec((CONV_W, cb), lambda j: (0, j)), vec, wspec, vec, wspec, vec, vec],
        out_specs=[rows, conv_rows, rows],
        out_shape=[
            jax.ShapeDtypeStruct((n, c), BF16),
            jax.ShapeDtypeStruct((n, CONV_W - 1, c), F32),
            jax.ShapeDtypeStruct((n, c), F32),
        ],
        compiler_params=_params("parallel"),
        name="rglru_step",
    )(proj, proj, conv_state, h0, conv_w, row(conv_b), w_ga.astype(BF16), row(b_ga),
      w_gx.astype(BF16), row(b_gx), row(lam))


def _scan_rows(a, b, h):
    t, c = a.shape
    a3 = a.reshape(t // 8, 8, c)
    b3 = b.reshape(t // 8, 8, c)
    row = lax.broadcasted_iota(jnp.int32, a3.shape, 1)
    for d in (1, 2, 4):
        keep = row >= d
        a_prev = jnp.where(keep, pltpu.roll(a3, d, axis=1), 1.0)
        b_prev = jnp.where(keep, pltpu.roll(b3, d, axis=1), 0.0)
        b3 = a3 * b_prev + b3
        a3 = a3 * a_prev
    out = []
    for g in range(t // 8):
        hg = a3[g] * h + b3[g]
        out.append(hg)
        h = hg[7:8, :]
    return jnp.concatenate(out, axis=0), h


def _a_front_kernel(xn_ref, wx_ref, wg_ref, xs_ref, cs_ref, h0_ref, cw_ref, cb_ref, wga_ref, bga_ref, wgx_ref,
                    bgx_ref, lam_ref, hg_ref, nc_ref, nh_ref, ps_ref, wb_ref, tail_ref, hc_ref, pr_ref,
                    *, tt, sub, blk):
    b = pl.program_id(1)
    ti = pl.program_id(2)
    tail = CONV_W - 1
    n_sub = tt // sub
    cb = cs_ref.shape[1]

    @pl.when((b == 0) & (ti == 0))
    def _():
        wb_ref[:, :cb] = wx_ref[...].astype(BF16)
        wb_ref[:, cb:] = wg_ref[...].astype(BF16)
        ps = jnp.dot(xs_ref[...], wb_ref[...], preferred_element_type=F32)
        ps_ref[0] = ps[:, :cb]
        ps_ref[1] = ps[:, cb:]

    @pl.when(ti == 0)
    def _():
        tail_ref[...] = jnp.concatenate([jnp.zeros((8 - tail, cb), F32), cs_ref[...]], axis=0)
        hc_ref[...] = h0_ref[...]

    neg_lam = -lam_ref[...]
    softplus = jnp.maximum(neg_lam, 0.0) + jnp.log1p(jnp.exp(-jnp.abs(neg_lam)))
    log_a_scale = -LRU_C * softplus
    prev8 = tail_ref[...]
    h = hc_ref[...]

    def project(k):
        pr_ref[k] = jnp.dot(xn_ref[k * sub:(k + 1) * sub, :], wb_ref[...], preferred_element_type=F32)

    project(0)
    for k in range(n_sub):
        if k + 1 < n_sub:
            project(k + 1)
        xb = pr_ref[k, :, :cb]
        ext = jnp.concatenate([prev8, xb], axis=0)
        prev8 = xb[sub - 8:, :]
        ext1 = pltpu.roll(ext, 1, axis=0)
        pair = ext * cw_ref[1:2, :] + ext1 * cw_ref[0:1, :]
        xconv = (cb_ref[...] + pltpu.roll(pair, 2, axis=0)[8:, :]) + (ext1[8:, :] * cw_ref[2:3, :] + xb * cw_ref[3:4, :])
        r_parts, i_parts = [], []
        for hh in range(cb // blk):
            xh = xconv[:, hh * blk:(hh + 1) * blk].astype(BF16)
            r_parts.append(jnp.dot(xh, wga_ref[hh], preferred_element_type=F32))
            i_parts.append(jnp.dot(xh, wgx_ref[hh], preferred_element_type=F32))
        r = jax.nn.sigmoid(jnp.concatenate(r_parts, axis=1) + bga_ref[...])
        i = jax.nn.sigmoid(jnp.concatenate(i_parts, axis=1) + bgx_ref[...])
        a = jnp.exp(log_a_scale * r)
        bb = jnp.sqrt(1.0 - a * a) * (i * xconv)
        h_all, h = _scan_rows(a, bb, h)
        hg_ref[k * sub:(k + 1) * sub, :] = (h_all * jax.nn.silu(pr_ref[k, :, cb:])).astype(hg_ref.dtype)
    tail_ref[...] = prev8
    hc_ref[...] = h

    @pl.when(ti == pl.num_programs(2) - 1)
    def _():
        nc_ref[...] = prev8[8 - tail:, :]
        nh_ref[...] = h


def _a_front_call(xn, xs, w_in, layer, conv_state, h0, conv_w, conv_b, w_ga, b_ga, w_gx, b_gx, lam, n, t):
    d = xn.shape[1]
    c = w_in.shape[2] // 2
    heads, blk, _ = w_ga.shape
    cb = max(512, blk)
    tt, sub = 512, 128
    assert c % cb == 0 and cb % blk == 0 and blk % LANES == 0 and t % tt == 0
    ncb = c // cb
    ns = xs.shape[0]
    row = lambda a: a.reshape(1, c)
    vec = pl.BlockSpec((1, cb), lambda j, b, i: (0, j))
    wspec = pl.BlockSpec((cb // blk, blk, blk), lambda j, b, i: (j, 0, 0))
    hg, nc, nh, ps = pl.pallas_call(
        functools.partial(_a_front_kernel, tt=tt, sub=sub, blk=blk),
        grid=(ncb, n, t // tt),
        in_specs=[
            pl.BlockSpec((None, tt, d), lambda j, b, i: (b, i, 0)),
            pl.BlockSpec((None, d, cb), lambda j, b, i: (layer, 0, j)),
            pl.BlockSpec((None, d, cb), lambda j, b, i: (layer, 0, ncb + j)),
            pl.BlockSpec((ns, d), lambda j, b, i: (0, 0)),
            pl.BlockSpec((None, CONV_W - 1, cb), lambda j, b, i: (b, 0, j)),
            pl.BlockSpec((None, 1, cb), lambda j, b, i: (b, 0, j)),
            pl.BlockSpec((CONV_W, cb), lambda j, b, i: (0, j)),
            vec, wspec, vec, wspec, vec, vec,
        ],
        out_specs=[
            pl.BlockSpec((None, tt, cb), lambda j, b, i: (b, i, j)),
            pl.BlockSpec((None, CONV_W - 1, cb), lambda j, b, i: (b, 0, j)),
            pl.BlockSpec((None, 1, cb), lambda j, b, i: (b, 0, j)),
            pl.BlockSpec((2, ns, cb), lambda j, b, i: (0, 0, j)),
        ],
        out_shape=[
            jax.ShapeDtypeStruct((n, t, c), BF16),
            jax.ShapeDtypeStruct((n, CONV_W - 1, c), F32),
            jax.ShapeDtypeStruct((n, 1, c), F32),
            jax.ShapeDtypeStruct((2, ns, c), F32),
        ],
        scratch_shapes=[
            pltpu.VMEM((d, 2 * cb), BF16),
            pltpu.VMEM((8, cb), F32),
            pltpu.VMEM((1, cb), F32),
            pltpu.VMEM((tt // sub, sub, 2 * cb), F32),
        ],
        compiler_params=_params("parallel", "arbitrary", "arbitrary"),
        name="a_in_rglru",
    )(xn.reshape(n, t, d), w_in, w_in, xs, conv_state, h0.reshape(n, 1, c), conv_w, row(conv_b),
      w_ga.astype(BF16), row(b_ga), w_gx.astype(BF16), row(b_gx), row(lam))
    return hg.reshape(n * t, c), nc, nh.reshape(n, c), ps


def _rel_bucket(dist):
    dist = np.asarray(dist)
    d = np.maximum(dist, 1).astype(np.float32)
    large = MAX_EXACT + (np.log(d / MAX_EXACT) / np.log(MAX_DIST / MAX_EXACT)
                         * (N_BUCKETS - MAX_EXACT)).astype(np.int32)
    large = np.minimum(large, N_BUCKETS - 1)
    return np.where(dist < MAX_EXACT, dist, large).astype(np.int32)


def _prompt_bias(rel_bias):
    span = 3 * BAND - 1
    neg = jnp.full((N_SLOTS, BAND - 1), -jnp.inf, F32)
    tables = []
    for g, r in enumerate(DILATIONS):
        tg = rel_bias[:, g * N_SLOTS:(g + 1) * N_SLOTS].astype(F32)
        tb = tg[_rel_bucket(np.arange(BAND + 1) * r)].T
        wr = jnp.concatenate([neg, tb[:, ::-1], neg, neg[:, :1]], axis=1)
        skew = jnp.tile(wr, (1, BAND))[:, :BAND * span].reshape(N_SLOTS, BAND, span)
        tables.append(skew[:, :, BAND - 1:])
    return jnp.stack(tables)


def _sample_bias(rel_bias):
    tables = []
    for g, r in enumerate(DILATIONS):
        steps = np.concatenate([BAND - np.arange(BAND), np.zeros(BAND, np.int64)])
        tg = rel_bias[:, g * N_SLOTS:(g + 1) * N_SLOTS].astype(F32)
        tables.append(tg[_rel_bucket(steps * r)].T)
    return jnp.stack(tables)


def _attn_prompt_kernel(q_ref, k_ref, v_ref, bias_ref, gate_ref, o_ref, acc_ref, m_ref, l_ref, *, t):
    g = pl.program_id(2)

    def unit(start, gi, r, has_prev):
        nk = 2 * BAND if has_prev else BAND
        kstart = start - BAND * r if has_prev else start
        rows = lambda s0, cnt: pl.ds(s0, cnt, stride=r) if r > 1 else pl.ds(s0, cnt)
        k = k_ref[rows(kstart, nk), :].astype(BF16)
        v = v_ref[rows(kstart, nk), :].astype(BF16)
        q = jnp.concatenate([q_ref[s, rows(start, BAND), :] for s in range(GQA)], axis=0).astype(BF16)
        sc = lax.dot_general(q, k, (((1,), (1,)), ((), ())), preferred_element_type=F32) * SM_SCALE
        sc = sc + bias_ref[:, :, 2 * BAND - nk:].reshape(GQA * BAND, nk)
        m = jnp.max(sc, axis=-1, keepdims=True)
        p = jnp.exp(sc - m)
        l = jnp.sum(p, axis=-1, keepdims=True)
        pv = jnp.dot(p.astype(BF16), v, preferred_element_type=F32)
        lane = lax.broadcasted_iota(jnp.int32, (BAND, LANES), 1)
        m_tile = jnp.zeros((BAND, LANES), F32)
        l_tile = jnp.ones((BAND, LANES), F32)
        for s in range(GQA):
            blk = slice(s * BAND, (s + 1) * BAND)
            acc_ref[gi, s, rows(start, BAND), :] = pv[blk]
            m_tile = jnp.where(lane == s, m[blk], m_tile)
            l_tile = jnp.where(lane == s, l[blk], l_tile)
        m_ref[gi, rows(start, BAND), :] = m_tile
        l_ref[gi, rows(start, BAND), :] = l_tile

    def group_pass(gi, r):
        nb = t // (BAND * r)

        def head_block(cls, carry):
            unit(cls, gi, r, False)
            return carry

        lax.fori_loop(0, r, head_block, 0, unroll=min(r, ATTN_UNROLL))

        def later_block(u, carry):
            cls = u % r
            blk = 1 + u // r
            unit(blk * (BAND * r) + cls, gi, r, True)
            return carry

        if nb > 1:
            lax.fori_loop(0, r * (nb - 1), later_block, 0, unroll=ATTN_UNROLL)

    for gi, r in enumerate(DILATIONS):
        pl.when(g == gi)(functools.partial(group_pass, gi, r))

    @pl.when(g == N_GROUPS - 1)
    def _():
        def combine(c, carry):
            rows = pl.ds(pl.multiple_of(c * BAND, BAND), BAND)
            ms = [m_ref[gi, rows, :] for gi in range(N_GROUPS)]
            mx = functools.reduce(jnp.maximum, ms)
            es = [jnp.exp(mg - mx) for mg in ms]
            den = sum(es[gi] * l_ref[gi, rows, :] for gi in range(N_GROUPS))
            ws = [e / den for e in es]
            for s in range(GQA):
                o = sum(ws[gi][:, s:s + 1] * acc_ref[gi, s, rows, :] for gi in range(N_GROUPS))
                cols = slice(s * HEAD_DIM, (s + 1) * HEAD_DIM)
                o_ref[rows, cols] = (o * jax.nn.silu(gate_ref[rows, cols])).astype(o_ref.dtype)
            return carry

        lax.fori_loop(0, t // BAND, combine, 0)


def _attn_prompt_call(q_heads, kv_heads, bias, gate, n, t):
    assert all(t % (BAND * r) == 0 for r in DILATIONS)
    q5 = q_heads.reshape(N_GROUPS, KV_HEADS, GQA, n, t, HEAD_DIM)
    kv6 = kv_heads.reshape(N_GROUPS, 2, KV_HEADS, n, t, HEAD_DIM)
    gate3 = gate.reshape(n, t, ATTN_W)
    kv_spec = lambda which: pl.BlockSpec((None, None, None, None, t, HEAD_DIM),
                                         lambda b, h, g: (g, which, h, b, 0, 0))
    out = pl.pallas_call(
        functools.partial(_attn_prompt_kernel, t=t),
        grid=(n, KV_HEADS, N_GROUPS),
        in_specs=[
            pl.BlockSpec((None, None, GQA, None, t, HEAD_DIM), lambda b, h, g: (g, h, 0, b, 0, 0)),
            kv_spec(0), kv_spec(1),
            pl.BlockSpec((None, GQA, BAND, 2 * BAND), lambda b, h, g: (g, h, 0, 0)),
            pl.BlockSpec((None, t, GQA * HEAD_DIM), lambda b, h, g: (b, 0, h)),
        ],
        out_specs=pl.BlockSpec((None, t, GQA * HEAD_DIM), lambda b, h, g: (b, 0, h)),
        out_shape=jax.ShapeDtypeStruct((n, t, ATTN_W), BF16),
        scratch_shapes=[pltpu.VMEM((N_GROUPS, GQA, t, HEAD_DIM), F32),
                        pltpu.VMEM((N_GROUPS, t, LANES), F32),
                        pltpu.VMEM((N_GROUPS, t, LANES), F32)],
        compiler_params=_params("parallel", "parallel", "arbitrary"),
        name="attn_prompt",
    )(q5, kv6, kv6, bias, gate3)
    return out.reshape(n * t, ATTN_W)


def _attn_sample_kernel(q_ref, kvn_ref, s0_ref, s1_ref, s2_ref, bias_ref, gate_ref, o_ref):
    st_refs = (s0_ref, s1_ref, s2_ref)
    bf = lambda x: x.astype(BF16)
    rnd = lambda x: x.astype(BF16).astype(F32)
    for h in range(KV_HEADS):
        sl = slice(h * GQA, (h + 1) * GQA)
        s_old, s_new, v_old, v_new = [], [], [], []
        for g in range(N_GROUPS):
            q = q_ref[g * N_SLOTS + h * GQA:g * N_SLOTS + (h + 1) * GQA, :]
            k_old = st_refs[g][:, 0, h, :]
            v_old.append(st_refs[g][:, 1, h, :])
            row = g * 2 * KV_HEADS + h
            k_new = kvn_ref[row:row + 1, :]
            v_new.append(kvn_ref[row + KV_HEADS:row + KV_HEADS + 1, :])
            so = lax.dot_general(bf(q), bf(k_old), (((1,), (1,)), ((), ())), preferred_element_type=F32)
            s_old.append(so * SM_SCALE + bias_ref[g, sl, :BAND])
            sn = jnp.sum(rnd(q) * rnd(k_new), axis=-1, keepdims=True)
            s_new.append(sn * SM_SCALE + bias_ref[g, sl, BAND:BAND + 1])
        m = functools.reduce(jnp.maximum, [jnp.max(x, axis=-1, keepdims=True) for x in s_old] + s_new)
        num = jnp.zeros((GQA, HEAD_DIM), F32)
        den = jnp.zeros((GQA, 1), F32)
        for g in range(N_GROUPS):
            p_old = jnp.exp(s_old[g] - m)
            p_new = jnp.exp(s_new[g] - m)
            den = den + jnp.sum(p_old, axis=-1, keepdims=True) + p_new
            num = num + jnp.dot(bf(p_old), bf(v_old[g]), preferred_element_type=F32) + rnd(p_new) * rnd(v_new[g])
        o_ref[sl, :] = ((num / den) * jax.nn.silu(gate_ref[sl, :])).astype(o_ref.dtype)


def _attn_sample_call(q, kv_new, states, bias, gate):
    n = q.shape[0]
    st_views, st_specs = [], []
    for g, r in enumerate(DILATIONS):
        assert states[g].shape[1] == BAND * r
        st_views.append(states[g].reshape(n, BAND, r, 2, KV_HEADS, HEAD_DIM))
        st_specs.append(pl.BlockSpec((None, BAND, None, 2, KV_HEADS, HEAD_DIM), lambda b: (b, 0, 0, 0, 0, 0)))
    per_b = lambda rows: pl.BlockSpec((None, rows, HEAD_DIM), lambda b: (b, 0, 0))
    out = pl.pallas_call(
        _attn_sample_kernel,
        grid=(n,),
        in_specs=[per_b(N_GROUPS * N_SLOTS), per_b(N_GROUPS * 2 * KV_HEADS)] + st_specs + [
            pl.BlockSpec((N_GROUPS, N_SLOTS, 2 * BAND), lambda b: (0, 0, 0)), per_b(N_SLOTS)],
        out_specs=per_b(N_SLOTS),
        out_shape=jax.ShapeDtypeStruct((n, N_SLOTS, HEAD_DIM), BF16),
        compiler_params=_params("parallel"),
        name="attn_sample",
    )(q.reshape(n, N_GROUPS * N_SLOTS, HEAD_DIM), kv_new.reshape(n, N_GROUPS * 2 * KV_HEADS, HEAD_DIM),
      *st_views, bias, gate.reshape(n, N_SLOTS, HEAD_DIM))
    return out.reshape(n, ATTN_W)


def _trunks(xp, xs, conv_p, h_p, conv_s, h_s, kv_bufs, w):
    n, t, d = xp.shape
    ns = xs.shape[0]
    assert xs.shape[1] == 1
    n_a = w["a_w_in"].shape[0]
    n_b = w["b_w_in"].shape[0]
    xp2 = xp.reshape(n * t, d)
    xs2 = xs.reshape(ns, d)
    (xnp,) = _norm_call(xp2, [w["a_pre_g"][0]])
    (xns,) = _norm_call(xs2, [w["a_pre_g"][0]])
    conv_out_p, h_out_p, conv_out_s, h_out_s = [], [], [], []
    for l in range(n_a):
        mixer = (w["a_conv_w"][l], w["a_conv_b"][l], w["a_w_gate_a"][l], w["a_b_gate_a"][l],
                 w["a_w_gate_x"][l], w["a_b_gate_x"][l], w["a_lambda"][l])
        hgp, nc, nh, proj_s = _a_front_call(xnp, xns, w["a_w_in"], l, conv_p[l], h_p[l], *mixer, n, t)
        conv_out_p.append(nc)
        h_out_p.append(nh)
        hgs, nc, nh = _rglru_step_call(proj_s, conv_s[l], h_s[l], *mixer)
        conv_out_s.append(nc)
        h_out_s.append(nh)
        yp, ys = _mm_call(hgp, hgs, w["a_w_out"], layer=l, name="a_out_proj")
        gains = [w["a_pre_g"][l + 1]] if l + 1 < n_a else [w["kv_norm_g"], w["b_pre_g"][0]]
        xp2, xnps = _resnorm_call(yp, xp2, w["a_post_g"][l], gains)
        xs2, xnss = _resnorm_call(ys, xs2, w["a_post_g"][l], gains)
        xnp, xns = xnps[0], xnss[0]

    kvp, kvp_heads, kvs = _mm_call(xnp, xns, w["w_kv"], heads=True, name="kv_proj")
    kv_out_p = []
    for g in range(N_GROUPS):
        keep = min(WINDOWS[g], t)
        kept = kvp.reshape(n, t, N_GROUPS * KV_GROUP_W)[:, t - keep:, g * KV_GROUP_W:(g + 1) * KV_GROUP_W]
        kv_out_p.append(kept.reshape(n, keep, 2, KV_HEADS, HEAD_DIM))
    kvs4 = kvs.reshape(ns, 1, N_GROUPS, 2, KV_HEADS, HEAD_DIM)
    kv_out_s = [jnp.concatenate([kv_bufs[g][:, 1:], kvs4[:, :, g]], axis=1) for g in range(N_GROUPS)]
    bias_p = _prompt_bias(w["rel_bias"])
    bias_s = _sample_bias(w["rel_bias"])

    xnp, xns = xnps[1], xnss[1]
    for l in range(n_b):
        gate_p, gate_s = _mm_call(xnp, xns, w["b_w_in"], layer=l, cols=(Q_W, ATTN_W), name="gate_proj")
        q_heads, q_s = _mm_call(xnp, xns, w["b_w_in"], layer=l, cols=(0, Q_W), natural=False, heads=True,
                                name="q_proj")
        ogp = _attn_prompt_call(q_heads, kvp_heads, bias_p, gate_p, n, t)
        ogs = _attn_sample_call(q_s, kvs, kv_bufs, bias_s, gate_s)
        yp, ys = _mm_call(ogp, ogs, w["b_w_out"], layer=l, name="b_out_proj")
        gains = [w["b_pre_g"][l + 1]] if l + 1 < n_b else []
        xp2, xnps = _resnorm_call(yp, xp2, w["b_post_g"][l], gains)
        xs2, xnss = _resnorm_call(ys, xs2, w["b_post_g"][l], gains)
        if gains:
            xnp, xns = xnps[0], xnss[0]
    prompt_out = (xp2.reshape(n, t, d), jnp.stack(conv_out_p), jnp.stack(h_out_p), *kv_out_p)
    sample_out = (xs2.reshape(ns, 1, d), jnp.stack(conv_out_s), jnp.stack(h_out_s), *kv_out_s)
    return prompt_out, sample_out


def kernel(x_prompt, x_sample, state_conv, state_h, state_kv_w128, state_kv_w512, state_kv_w2048,
           a_pre_g, a_w_in, a_conv_w, a_conv_b, a_w_gate_a, a_b_gate_a, a_w_gate_x, a_b_gate_x,
           a_lambda, a_w_out, a_post_g, kv_norm_g, w_kv, rel_bias, b_pre_g, b_w_in, b_w_out, b_post_g):
    w = dict(
        a_pre_g=a_pre_g, a_w_in=a_w_in, a_conv_w=a_conv_w, a_conv_b=a_conv_b,
        a_w_gate_a=a_w_gate_a, a_b_gate_a=a_b_gate_a, a_w_gate_x=a_w_gate_x, a_b_gate_x=a_b_gate_x,
        a_lambda=a_lambda, a_w_out=a_w_out, a_post_g=a_post_g, kv_norm_g=kv_norm_g,
        w_kv=w_kv, rel_bias=rel_bias, b_pre_g=b_pre_g, b_w_in=b_w_in, b_w_out=b_w_out, b_post_g=b_post_g)
    nb_p = x_prompt.shape[0]
    n_a = a_w_in.shape[0]
    d_rnn = a_conv_w.shape[-1]
    conv0 = jnp.zeros((n_a, nb_p, CONV_W - 1, d_rnn), F32)
    h0 = jnp.zeros((n_a, nb_p, d_rnn), F32)
    prompt_out, sample_out = _trunks(x_prompt, x_sample, conv0, h0, state_conv, state_h,
                                     (state_kv_w128, state_kv_w512, state_kv_w2048), w)
    return (prompt_out[0], sample_out[0], *prompt_out[1:], *sample_out[1:])
```

```python
import functools
import math

import numpy as np
import jax
import jax.numpy as jnp
from jax import lax
from jax.experimental import pallas as pl
from jax.experimental.pallas import tpu as pltpu

F32 = jnp.float32
BF16 = jnp.bfloat16

EPS = 1e-6
LRU_C = 8.0
CONV_W = 4
HEAD_DIM = 128
N_SLOTS = 16
KV_HEADS = 4
GQA = N_SLOTS // KV_HEADS
WINDOWS = (128, 512, 2048)
DILATIONS = (1, 4, 16)
N_GROUPS = 3
BAND = 128
ATTN_W = N_SLOTS * HEAD_DIM
Q_W = N_GROUPS * ATTN_W
KV_GROUP_W = 2 * KV_HEADS * HEAD_DIM
N_BUCKETS = 32
MAX_EXACT = 16
MAX_DIST = 2048
SM_SCALE = HEAD_DIM ** -0.5

ATTN_UNROLL = 8
LANES = 128
VMEM_LIMIT = 56 * 1024 * 1024


def _params(*sem):
    return pltpu.CompilerParams(dimension_semantics=sem, vmem_limit_bytes=VMEM_LIMIT)


def _tile(dim, pref):
    if dim <= pref:
        return dim
    t = pref
    while dim % t:
        t //= 2
    return t


def _rms(x, g):
    return x * lax.rsqrt(jnp.mean(x * x, axis=-1, keepdims=True) + EPS) * g


def _norm_kernel(x_ref, g_ref, *o_refs):
    x = x_ref[...]
    xhat = x * lax.rsqrt(jnp.mean(x * x, axis=-1, keepdims=True) + EPS)
    for k, o_ref in enumerate(o_refs):
        o_ref[...] = (xhat * g_ref[k:k + 1, :]).astype(o_ref.dtype)


def _norm_call(x, gains):
    m, d = x.shape
    tm = _tile(m, 256)
    g = jnp.stack(gains)
    ng = len(gains)
    row = pl.BlockSpec((tm, d), lambda i: (i, 0))
    return pl.pallas_call(
        _norm_kernel,
        grid=(m // tm,),
        in_specs=[row, pl.BlockSpec((ng, d), lambda i: (0, 0))],
        out_specs=[row] * ng,
        out_shape=[jax.ShapeDtypeStruct((m, d), BF16)] * ng,
        compiler_params=_params("parallel"),
        name="rmsnorm",
    )(x, g)


def _resnorm_kernel(y_ref, x_ref, gp_ref, g_ref, xo_ref, *o_refs):
    xn = x_ref[...] + _rms(y_ref[...], gp_ref[...])
    xo_ref[...] = xn
    if o_refs:
        xhat = xn * lax.rsqrt(jnp.mean(xn * xn, axis=-1, keepdims=True) + EPS)
        for k, o_ref in enumerate(o_refs):
            o_ref[...] = (xhat * g_ref[k:k + 1, :]).astype(o_ref.dtype)


def _resnorm_call(y, x, g_post, gains):
    m, d = x.shape
    tm = _tile(m, 256)
    ng = len(gains)
    g = jnp.stack(gains) if gains else jnp.ones((1, d), F32)
    row = pl.BlockSpec((tm, d), lambda i: (i, 0))
    outs = pl.pallas_call(
        _resnorm_kernel,
        grid=(m // tm,),
        in_specs=[row, row, pl.BlockSpec((1, d), lambda i: (0, 0)),
                  pl.BlockSpec((g.shape[0], d), lambda i: (0, 0))],
        out_specs=[row] * (1 + ng),
        out_shape=[jax.ShapeDtypeStruct((m, d), F32)] + [jax.ShapeDtypeStruct((m, d), BF16)] * ng,
        compiler_params=_params("parallel"),
        name="residual_rmsnorm",
    )(y, x, g_post.reshape(1, d), g)
    return outs[0], list(outs[1:])


def _mm_kernel(a_ref, w_ref, a2_ref, *refs, natural, heads):
    *o_refs, o2_ref, wb_ref = refs

    @pl.when(pl.program_id(1) == 0)
    def _():
        wb_ref[...] = w_ref[...].astype(BF16)
        o2_ref[...] = jnp.dot(a2_ref[...], wb_ref[...], preferred_element_type=F32)

    y = jnp.dot(a_ref[...], wb_ref[...], preferred_element_type=F32)
    k = 0
    if natural:
        o_refs[k][...] = y
        k += 1
    if heads:
        for h in range(y.shape[1] // LANES):
            o_refs[k][h] = y[:, h * LANES:(h + 1) * LANES]


def _mm_call(a, a2, w, *, layer=0, cols=None, natural=True, heads=False, tm=1024, tn=512, name="matmul"):
    if w.ndim == 2:
        w = w[None]
    m, kdim = a.shape
    m2 = a2.shape[0]
    col0, n = cols if cols is not None else (0, w.shape[2])
    tm = _tile(m, tm)
    tn = _tile(n, tn)
    assert col0 % tn == 0
    out_specs, out_shape = [], []
    if natural:
        out_specs.append(pl.BlockSpec((tm, tn), lambda j, i: (i, j)))
        out_shape.append(jax.ShapeDtypeStruct((m, n), F32))
    if heads:
        out_specs.append(pl.BlockSpec((tn // LANES, tm, LANES), lambda j, i: (j, i, 0)))
        out_shape.append(jax.ShapeDtypeStruct((n // LANES, m, LANES), F32))
    out_specs.append(pl.BlockSpec((m2, tn), lambda j, i: (0, j)))
    out_shape.append(jax.ShapeDtypeStruct((m2, n), F32))
    return pl.pallas_call(
        functools.partial(_mm_kernel, natural=natural, heads=heads),
        grid=(n // tn, m // tm),
        in_specs=[pl.BlockSpec((tm, kdim), lambda j, i: (i, 0)),
                  pl.BlockSpec((None, kdim, tn), lambda j, i: (layer, 0, col0 // tn + j)),
                  pl.BlockSpec((m2, kdim), lambda j, i: (0, 0))],
        out_specs=out_specs,
        out_shape=out_shape,
        scratch_shapes=[pltpu.VMEM((kdim, tn), BF16)],
        compiler_params=_params("parallel", "arbitrary"),
        name=name,
    )(a, w, a2)


def _lane_fold_sq(v):
    return sum(v[:, s:s + LANES] * v[:, s:s + LANES] for s in range(0, v.shape[1], LANES))


def _times_rows(v, scale):
    return jnp.concatenate([v[:, s:s + LANES] * scale for s in range(0, v.shape[1], LANES)], axis=1)


def _out_res_kernel(a_ref, w_ref, a2_ref, x_ref, gp_ref, g_ref, xo_ref, *refs, n_row, n_gain, width):
    xn_refs = refs[:n_gain]
    o2_ref, y_scr, xnew_scr, ssq_y, ssq_x, sc_y, sc_x = refs[n_gain:]
    i = pl.program_id(0)
    j = pl.program_id(1)
    tn = w_ref.shape[1]
    cols = pl.ds(pl.multiple_of(j * tn, tn), tn)

    @pl.when((i == 0) & (j == 0))
    def _():
        ssq_x[...] = jnp.zeros(ssq_x.shape, F32)
        ssq_y[...] = jnp.zeros(ssq_y.shape, F32)

    @pl.when(j == 0)
    def _():
        for ssq, sc in ((ssq_x, sc_x), (ssq_y, sc_y)):
            mean = jnp.sum(ssq[...], axis=-1, keepdims=True) * (1.0 / width)
            sc[...] = jnp.broadcast_to(lax.rsqrt(mean + EPS), sc.shape)
            ssq[...] = jnp.zeros(ssq.shape, F32)

    if n_gain:
        @pl.when(i >= 2)
        def _():
            xhat = _times_rows(xnew_scr[:, cols], sc_x[...])
            for k in range(n_gain):
                xn_refs[k][...] = (xhat * g_ref[k:k + 1, :]).astype(xn_refs[k].dtype)

    @pl.when((i >= 1) & (i <= n_row))
    def _():
        xnew = x_ref[...] + _times_rows(y_scr[:, cols], sc_y[...]) * gp_ref[...]
        xo_ref[...] = xnew
        xnew_scr[:, cols] = xnew
        ssq_x[...] += _lane_fold_sq(xnew)

    @pl.when(i < n_row)
    def _():
        y = jnp.dot(a_ref[...], w_ref[...], preferred_element_type=F32)
        y_scr[:, cols] = y
        ssq_y[...] += _lane_fold_sq(y)

    @pl.when(i == 0)
    def _():
        o2_ref[...] = jnp.dot(a2_ref[...], w_ref[...], preferred_element_type=F32)


def _out_res_call(a, a2, w, x, g_post, gains, *, tm=512, tn=512, name="out_proj_residual"):
    m, kdim = a.shape
    n = w.shape[1]
    m2 = a2.shape[0]
    assert m % tm == 0 and n % tn == 0
    n_row, nj = m // tm, n // tn
    ng = len(gains)
    g = jnp.stack(gains) if gains else jnp.ones((1, n), F32)
    last = nj - 1
    lag1 = lambda i, j: (jnp.clip(i - 1, 0, n_row - 1), jnp.where(i < 1, 0, jnp.where(i <= n_row, j, last)))
    lag2 = lambda i, j: (jnp.clip(i - 2, 0, n_row - 1), jnp.where(i < 2, 0, j))
    outs = pl.pallas_call(
        functools.partial(_out_res_kernel, n_row=n_row, n_gain=ng, width=n),
        grid=(n_row + 2, nj),
        in_specs=[
            pl.BlockSpec((tm, kdim), lambda i, j: (jnp.minimum(i, n_row - 1), 0)),
            pl.BlockSpec((kdim, tn), lambda i, j: (0, jnp.where(i < n_row, j, last))),
            pl.BlockSpec((m2, kdim), lambda i, j: (0, 0)),
            pl.BlockSpec((tm, tn), lag1),
            pl.BlockSpec((1, tn), lambda i, j: (0, j)),
            pl.BlockSpec((g.shape[0], tn), lambda i, j: (0, j)),
        ],
        out_specs=[pl.BlockSpec((tm, tn), lag1)] + [pl.BlockSpec((tm, tn), lag2)] * ng + [
            pl.BlockSpec((m2, tn), lambda i, j: (0, jnp.where(i == 0, j, last)))],
        out_shape=[jax.ShapeDtypeStruct((m, n), F32)] + [jax.ShapeDtypeStruct((m, n), BF16)] * ng + [
            jax.ShapeDtypeStruct((m2, n), F32)],
        scratch_shapes=[pltpu.VMEM((tm, n), F32), pltpu.VMEM((tm, n), F32)] + [pltpu.VMEM((tm, LANES), F32)] * 4,
        compiler_params=_params("arbitrary", "arbitrary"),
        name=name,
    )(a, w, a2, x, g_post.reshape(1, n), g)
    return outs[0], list(outs[1:1 + ng]), outs[-1]


def _rglru_kernel(xb_ref, gate_ref, cs_ref, h0_ref, cw_ref, cb_ref, wga_ref, bga_ref, wgx_ref, bgx_ref,
                  lam_ref, hg_ref, nc_ref, nh_ref, *, blk):
    tail = CONV_W - 1
    xb = xb_ref[...]
    xconv = cb_ref[...]
    for k in range(tail):
        xconv = xconv + cs_ref[:, k, :] * cw_ref[k:k + 1, :]
        if k:
            nc_ref[:, k - 1, :] = cs_ref[:, k, :]
    xconv = xconv + xb * cw_ref[tail:tail + 1, :]
    nc_ref[:, tail - 1, :] = xb

    r_parts, i_parts = [], []
    for hh in range(xconv.shape[1] // blk):
        xh = xconv[:, hh * blk:(hh + 1) * blk].astype(BF16)
        r_parts.append(jnp.dot(xh, wga_ref[hh], preferred_element_type=F32))
        i_parts.append(jnp.dot(xh, wgx_ref[hh], preferred_element_type=F32))
    r = jax.nn.sigmoid(jnp.concatenate(r_parts, axis=1) + bga_ref[...])
    i = jax.nn.sigmoid(jnp.concatenate(i_parts, axis=1) + bgx_ref[...])

    neg_lam = -lam_ref[...]
    softplus = jnp.maximum(neg_lam, 0.0) + jnp.log1p(jnp.exp(-jnp.abs(neg_lam)))
    a = jnp.exp((-LRU_C * softplus) * r)
    h = a * h0_ref[...] + jnp.sqrt(1.0 - a * a) * (i * xconv)
    nh_ref[...] = h
    hg_ref[...] = (h * jax.nn.silu(gate_ref[...])).astype(hg_ref.dtype)


def _rglru_step_call(proj, conv_state, h0, conv_w, conv_b, w_ga, b_ga, w_gx, b_gx, lam):
    _, n, c = proj.shape
    heads, blk, _ = w_ga.shape
    cb = _tile(c, max(1024, blk))
    assert cb % blk == 0 and blk % LANES == 0
    ncb = c // cb
    row = lambda a: a.reshape(1, c)
    vec = pl.BlockSpec((1, cb), lambda j: (0, j))
    rows = pl.BlockSpec((n, cb), lambda j: (0, j))
    conv_rows = pl.BlockSpec((n, CONV_W - 1, cb), lambda j: (0, 0, j))
    wspec = pl.BlockSpec((cb // blk, blk, blk), lambda j: (j, 0, 0))
    return pl.pallas_call(
        functools.partial(_rglru_kernel, blk=blk),
        grid=(ncb,),
        in_specs=[pl.BlockSpec((None, n, cb), lambda j: (0, 0, j)), pl.BlockSpec((None, n, cb), lambda j: (1, 0, j)),
                  conv_rows, rows,
                  pl.BlockSpec((CONV_W, cb), lambda j: (0, j)), vec, wspec, vec, wspec, vec, vec],
        out_specs=[rows, conv_rows, rows],
        out_shape=[
            jax.ShapeDtypeStruct((n, c), BF16),
            jax.ShapeDtypeStruct((n, CONV_W - 1, c), F32),
            jax.ShapeDtypeStruct((n, c), F32),
        ],
        compiler_params=_params("parallel"),
        name="rglru_step",
    )(proj, proj, conv_state, h0, conv_w, row(conv_b), w_ga.astype(BF16), row(b_ga),
      w_gx.astype(BF16), row(b_gx), row(lam))


def _scan_rows(a, b, h):
    t, c = a.shape
    a3 = a.reshape(t // 8, 8, c)
    b3 = b.reshape(t // 8, 8, c)
    row = lax.broadcasted_iota(jnp.int32, a3.shape, 1)
    for d in (1, 2, 4):
        keep = row >= d
        a_prev = jnp.where(keep, pltpu.roll(a3, d, axis=1), 1.0)
        b_prev = jnp.where(keep, pltpu.roll(b3, d, axis=1), 0.0)
        b3 = a3 * b_prev + b3
        a3 = a3 * a_prev
    out = []
    for g in range(t // 8):
        hg = a3[g] * h + b3[g]
        out.append(hg)
        h = hg[7:8, :]
    return jnp.concatenate(out, axis=0), h


def _a_front_kernel(xn_ref, wx_ref, wg_ref, xs_ref, cs_ref, h0_ref, cw_ref, cb_ref, wga_ref, bga_ref, wgx_ref,
                    bgx_ref, lam_ref, hg_ref, nc_ref, nh_ref, ps_ref, wb_ref, tail_ref, hc_ref, pr_ref,
                    *, tt, sub, blk):
    b = pl.program_id(1)
    ti = pl.program_id(2)
    tail = CONV_W - 1
    n_sub = tt // sub
    cb = cs_ref.shape[1]

    @pl.when((b == 0) & (ti == 0))
    def _():
        wb_ref[:, :cb] = wx_ref[...].astype(BF16)
        wb_ref[:, cb:] = wg_ref[...].astype(BF16)
        ps = jnp.dot(xs_ref[...], wb_ref[...], preferred_element_type=F32)
        ps_ref[0] = ps[:, :cb]
        ps_ref[1] = ps[:, cb:]

    @pl.when(ti == 0)
    def _():
        tail_ref[...] = jnp.concatenate([jnp.zeros((8 - tail, cb), F32), cs_ref[...]], axis=0)
        hc_ref[...] = h0_ref[...]

    neg_lam = -lam_ref[...]
    softplus = jnp.maximum(neg_lam, 0.0) + jnp.log1p(jnp.exp(-jnp.abs(neg_lam)))
    log_a_scale = -LRU_C * softplus
    prev8 = tail_ref[...]
    h = hc_ref[...]

    def project(k):
        pr_ref[k] = jnp.dot(xn_ref[k * sub:(k + 1) * sub, :], wb_ref[...], preferred_element_type=F32)

    project(0)
    for k in range(n_sub):
        if k + 1 < n_sub:
            project(k + 1)
        xb = pr_ref[k, :, :cb]
        ext = jnp.concatenate([prev8, xb], axis=0)
        prev8 = xb[sub - 8:, :]
        ext1 = pltpu.roll(ext, 1, axis=0)
        pair = ext * cw_ref[1:2, :] + ext1 * cw_ref[0:1, :]
        xconv = (cb_ref[...] + pltpu.roll(pair, 2, axis=0)[8:, :]) + (ext1[8:, :] * cw_ref[2:3, :] + xb * cw_ref[3:4, :])
        r_parts, i_parts = [], []
        for hh in range(cb // blk):
            xh = xconv[:, hh * blk:(hh + 1) * blk].astype(BF16)
            r_parts.append(jnp.dot(xh, wga_ref[hh], preferred_element_type=F32))
            i_parts.append(jnp.dot(xh, wgx_ref[hh], preferred_element_type=F32))
        r = jax.nn.sigmoid(jnp.concatenate(r_parts, axis=1) + bga_ref[...])
        i = jax.nn.sigmoid(jnp.concatenate(i_parts, axis=1) + bgx_ref[...])
        a = jnp.exp(log_a_scale * r)
        bb = jnp.sqrt(1.0 - a * a) * (i * xconv)
        h_all, h = _scan_rows(a, bb, h)
        hg_ref[k * sub:(k + 1) * sub, :] = (h_all * jax.nn.silu(pr_ref[k, :, cb:])).astype(hg_ref.dtype)
    tail_ref[...] = prev8
    hc_ref[...] = h

    @pl.when(ti == pl.num_programs(2) - 1)
    def _():
        nc_ref[...] = prev8[8 - tail:, :]
        nh_ref[...] = h


def _a_front_call(xn, xs, w_in, layer, conv_state, h0, conv_w, conv_b, w_ga, b_ga, w_gx, b_gx, lam, n, t):
    d = xn.shape[1]
    c = w_in.shape[2] // 2
    heads, blk, _ = w_ga.shape
    cb = max(512, blk)
    tt, sub = 512, 128
    assert c % cb == 0 and cb % blk == 0 and blk % LANES == 0 and t % tt == 0
    ncb = c // cb
    ns = xs.shape[0]
    row = lambda a: a.reshape(1, c)
    vec = pl.BlockSpec((1, cb), lambda j, b, i: (0, j))
    wspec = pl.BlockSpec((cb // blk, blk, blk), lambda j, b, i: (j, 0, 0))
    hg, nc, nh, ps = pl.pallas_call(
        functools.partial(_a_front_kernel, tt=tt, sub=sub, blk=blk),
        grid=(ncb, n, t // tt),
        in_specs=[
            pl.BlockSpec((None, tt, d), lambda j, b, i: (b, i, 0)),
            pl.BlockSpec((None, d, cb), lambda j, b, i: (layer, 0, j)),
            pl.BlockSpec((None, d, cb), lambda j, b, i: (layer, 0, ncb + j)),
            pl.BlockSpec((ns, d), lambda j, b, i: (0, 0)),
            pl.BlockSpec((None, CONV_W - 1, cb), lambda j, b, i: (b, 0, j)),
            pl.BlockSpec((None, 1, cb), lambda j, b, i: (b, 0, j)),
            pl.BlockSpec((CONV_W, cb), lambda j, b, i: (0, j)),
            vec, wspec, vec, wspec, vec, vec,
        ],
        out_specs=[
            pl.BlockSpec((None, tt, cb), lambda j, b, i: (b, i, j)),
            pl.BlockSpec((None, CONV_W - 1, cb), lambda j, b, i: (b, 0, j)),
            pl.BlockSpec((None, 1, cb), lambda j, b, i: (b, 0, j)),
            pl.BlockSpec((2, ns, cb), lambda j, b, i: (0, 0, j)),
        ],
        out_shape=[
            jax.ShapeDtypeStruct((n, t, c), BF16),
            jax.ShapeDtypeStruct((n, CONV_W - 1, c), F32),
            jax.ShapeDtypeStruct((n, 1, c), F32),
            jax.ShapeDtypeStruct((2, ns, c), F32),
        ],
        scratch_shapes=[
            pltpu.VMEM((d, 2 * cb), BF16),
            pltpu.VMEM((8, cb), F32),
            pltpu.VMEM((1, cb), F32),
            pltpu.VMEM((tt // sub, sub, 2 * cb), F32),
        ],
        compiler_params=_params("parallel", "arbitrary", "arbitrary"),
        name="a_in_rglru",
    )(xn.reshape(n, t, d), w_in, w_in, xs, conv_state, h0.reshape(n, 1, c), conv_w, row(conv_b),
      w_ga.astype(BF16), row(b_ga), w_gx.astype(BF16), row(b_gx), row(lam))
    return hg.reshape(n * t, c), nc, nh.reshape(n, c), ps


def _rel_bucket(dist):
    dist = np.asarray(dist)
    d = np.maximum(dist, 1).astype(np.float32)
    large = MAX_EXACT + (np.log(d / MAX_EXACT) / np.log(MAX_DIST / MAX_EXACT)
                         * (N_BUCKETS - MAX_EXACT)).astype(np.int32)
    large = np.minimum(large, N_BUCKETS - 1)
    return np.where(dist < MAX_EXACT, dist, large).astype(np.int32)


def _prompt_bias(rel_bias):
    span = 3 * BAND - 1
    neg = jnp.full((N_SLOTS, BAND - 1), -jnp.inf, F32)
    tables = []
    for g, r in enumerate(DILATIONS):
        tg = rel_bias[:, g * N_SLOTS:(g + 1) * N_SLOTS].astype(F32)
        tb = tg[_rel_bucket(np.arange(BAND + 1) * r)].T
        wr = jnp.concatenate([neg, tb[:, ::-1], neg, neg[:, :1]], axis=1)
        skew = jnp.tile(wr, (1, BAND))[:, :BAND * span].reshape(N_SLOTS, BAND, span)
        tables.append(skew[:, :, BAND - 1:])
    return jnp.stack(tables)


def _sample_bias(rel_bias):
    tables = []
    for g, r in enumerate(DILATIONS):
        steps = np.concatenate([BAND - np.arange(BAND), np.zeros(BAND, np.int64)])
        tg = rel_bias[:, g * N_SLOTS:(g + 1) * N_SLOTS].astype(F32)
        tables.append(tg[_rel_bucket(steps * r)].T)
    return jnp.stack(tables)


def _attn_prompt_kernel(q_ref, k_ref, v_ref, bias_ref, gate_ref, o_ref, acc_ref, m_ref, l_ref, *, t):
    g = pl.program_id(2)

    def unit(start, gi, r, has_prev):
        nk = 2 * BAND if has_prev else BAND
        kstart = start - BAND * r if has_prev else start
        rows = lambda s0, cnt: pl.ds(s0, cnt, stride=r) if r > 1 else pl.ds(s0, cnt)
        k = k_ref[rows(kstart, nk), :].astype(BF16)
        v = v_ref[rows(kstart, nk), :].astype(BF16)
        q = jnp.concatenate([q_ref[s, rows(start, BAND), :] for s in range(GQA)], axis=0).astype(BF16)
        sc = lax.dot_general(q, k, (((1,), (1,)), ((), ())), preferred_element_type=F32) * SM_SCALE
        sc = sc + bias_ref[:, :, 2 * BAND - nk:].reshape(GQA * BAND, nk)
        m = jnp.max(sc, axis=-1, keepdims=True)
        p = jnp.exp(sc - m)
        l = jnp.sum(p, axis=-1, keepdims=True)
        pv = jnp.dot(p.astype(BF16), v, preferred_element_type=F32)
        lane = lax.broadcasted_iota(jnp.int32, (BAND, LANES), 1)
        m_tile = jnp.zeros((BAND, LANES), F32)
        l_tile = jnp.ones((BAND, LANES), F32)
        for s in range(GQA):
            blk = slice(s * BAND, (s + 1) * BAND)
            acc_ref[gi, s, rows(start, BAND), :] = pv[blk]
            m_tile = jnp.where(lane == s, m[blk], m_tile)
            l_tile = jnp.where(lane == s, l[blk], l_tile)
        m_ref[gi, rows(start, BAND), :] = m_tile
        l_ref[gi, rows(start, BAND), :] = l_tile

    def group_pass(gi, r):
        nb = t // (BAND * r)

        def head_block(cls, carry):
            unit(cls, gi, r, False)
            return carry

        lax.fori_loop(0, r, head_block, 0, unroll=min(r, ATTN_UNROLL))

        def later_block(u, carry):
            cls = u % r
            blk = 1 + u // r
            unit(blk * (BAND * r) + cls, gi, r, True)
            return carry

        if nb > 1:
            lax.fori_loop(0, r * (nb - 1), later_block, 0, unroll=ATTN_UNROLL)

    for gi, r in enumerate(DILATIONS):
        pl.when(g == gi)(functools.partial(group_pass, gi, r))

    @pl.when(g == N_GROUPS - 1)
    def _():
        def combine(c, carry):
            rows = pl.ds(pl.multiple_of(c * BAND, BAND), BAND)
            ms = [m_ref[gi, rows, :] for gi in range(N_GROUPS)]
            mx = functools.reduce(jnp.maximum, ms)
            es = [jnp.exp(mg - mx) for mg in ms]
            den = sum(es[gi] * l_ref[gi, rows, :] for gi in range(N_GROUPS))
            ws = [e / den for e in es]
            for s in range(GQA):
                o = sum(ws[gi][:, s:s + 1] * acc_ref[gi, s, rows, :] for gi in range(N_GROUPS))
                cols = slice(s * HEAD_DIM, (s + 1) * HEAD_DIM)
                o_ref[rows, cols] = (o * jax.nn.silu(gate_ref[rows, cols])).astype(o_ref.dtype)
            return carry

        lax.fori_loop(0, t // BAND, combine, 0)


def _attn_prompt_call(q_heads, kv_heads, bias, gate, n, t):
    assert all(t % (BAND * r) == 0 for r in DILATIONS)
    q5 = q_heads.reshape(N_GROUPS, KV_HEADS, GQA, n, t, HEAD_DIM)
    kv6 = kv_heads.reshape(N_GROUPS, 2, KV_HEADS, n, t, HEAD_DIM)
    gate3 = gate.reshape(n, t, ATTN_W)
    kv_spec = lambda which: pl.BlockSpec((None, None, None, None, t, HEAD_DIM),
                                         lambda b, h, g: (g, which, h, b, 0, 0))
    out = pl.pallas_call(
        functools.partial(_attn_prompt_kernel, t=t),
        grid=(n, KV_HEADS, N_GROUPS),
        in_specs=[
            pl.BlockSpec((None, None, GQA, None, t, HEAD_DIM), lambda b, h, g: (g, h, 0, b, 0, 0)),
            kv_spec(0), kv_spec(1),
            pl.BlockSpec((None, GQA, BAND, 2 * BAND), lambda b, h, g: (g, h, 0, 0)),
            pl.BlockSpec((None, t, GQA * HEAD_DIM), lambda b, h, g: (b, 0, h)),
        ],
        out_specs=pl.BlockSpec((None, t, GQA * HEAD_DIM), lambda b, h, g: (b, 0, h)),
        out_shape=jax.ShapeDtypeStruct((n, t, ATTN_W), BF16),
        scratch_shapes=[pltpu.VMEM((N_GROUPS, GQA, t, HEAD_DIM), F32),
                        pltpu.VMEM((N_GROUPS, t, LANES), F32),
                        pltpu.VMEM((N_GROUPS, t, LANES), F32)],
        compiler_params=_params("parallel", "parallel", "arbitrary"),
        name="attn_prompt",
    )(q5, kv6, kv6, bias, gate3)
    return out.reshape(n * t, ATTN_W)


def _attn_sample_kernel(q_ref, kvn_ref, s0_ref, s1_ref, s2_ref, bias_ref, gate_ref, o_ref):
    st_refs = (s0_ref, s1_ref, s2_ref)
    bf = lambda x: x.astype(BF16)
    rnd = lambda x: x.astype(BF16).astype(F32)
    for h in range(KV_HEADS):
        sl = slice(h * GQA, (h + 1) * GQA)
        s_old, s_new, v_old, v_new = [], [], [], []
        for g in range(N_GROUPS):
            q = q_ref[g * N_SLOTS + h * GQA:g * N_SLOTS + (h + 1) * GQA, :]
            k_old = st_refs[g][:, 0, h, :]
            v_old.append(st_refs[g][:, 1, h, :])
            row = g * 2 * KV_HEADS + h
            k_new = kvn_ref[row:row + 1, :]
            v_new.append(kvn_ref[row + KV_HEADS:row + KV_HEADS + 1, :])
            so = lax.dot_general(bf(q), bf(k_old), (((1,), (1,)), ((), ())), preferred_element_type=F32)
            s_old.append(so * SM_SCALE + bias_ref[g, sl, :BAND])
            sn = jnp.sum(rnd(q) * rnd(k_new), axis=-1, keepdims=True)
            s_new.append(sn * SM_SCALE + bias_ref[g, sl, BAND:BAND + 1])
        m = functools.reduce(jnp.maximum, [jnp.max(x, axis=-1, keepdims=True) for x in s_old] + s_new)
        num = jnp.zeros((GQA, HEAD_DIM), F32)
        den = jnp.zeros((GQA, 1), F32)
        for g in range(N_GROUPS):
            p_old = jnp.exp(s_old[g] - m)
            p_new = jnp.exp(s_new[g] - m)
            den = den + jnp.sum(p_old, axis=-1, keepdims=True) + p_new
            num = num + jnp.dot(bf(p_old), bf(v_old[g]), preferred_element_type=F32) + rnd(p_new) * rnd(v_new[g])
        o_ref[sl, :] = ((num / den) * jax.nn.silu(gate_ref[sl, :])).astype(o_ref.dtype)


def _attn_sample_call(q, kv_new, states, bias, gate):
    n = q.shape[0]
    st_views, st_specs = [], []
    for g, r in enumerate(DILATIONS):
        assert states[g].shape[1] == BAND * r
        st_views.append(states[g].reshape(n, BAND, r, 2, KV_HEADS, HEAD_DIM))
        st_specs.append(pl.BlockSpec((None, BAND, None, 2, KV_HEADS, HEAD_DIM), lambda b: (b, 0, 0, 0, 0, 0)))
    per_b = lambda rows: pl.BlockSpec((None, rows, HEAD_DIM), lambda b: (b, 0, 0))
    out = pl.pallas_call(
        _attn_sample_kernel,
        grid=(n,),
        in_specs=[per_b(N_GROUPS * N_SLOTS), per_b(N_GROUPS * 2 * KV_HEADS)] + st_specs + [
            pl.BlockSpec((N_GROUPS, N_SLOTS, 2 * BAND), lambda b: (0, 0, 0)), per_b(N_SLOTS)],
        out_specs=per_b(N_SLOTS),
        out_shape=jax.ShapeDtypeStruct((n, N_SLOTS, HEAD_DIM), BF16),
        compiler_params=_params("parallel"),
        name="attn_sample",
    )(q.reshape(n, N_GROUPS * N_SLOTS, HEAD_DIM), kv_new.reshape(n, N_GROUPS * 2 * KV_HEADS, HEAD_DIM),
      *st_views, bias, gate.reshape(n, N_SLOTS, HEAD_DIM))
    return out.reshape(n, ATTN_W)


def _trunks(xp, xs, conv_p, h_p, conv_s, h_s, kv_bufs, w):
    n, t, d = xp.shape
    ns = xs.shape[0]
    assert xs.shape[1] == 1
    n_a = w["a_w_in"].shape[0]
    n_b = w["b_w_in"].shape[0]
    xp2 = xp.reshape(n * t, d)
    xs2 = xs.reshape(ns, d)
    (xnp,) = _norm_call(xp2, [w["a_pre_g"][0]])
    (xns,) = _norm_call(xs2, [w["a_pre_g"][0]])
    conv_out_p, h_out_p, conv_out_s, h_out_s = [], [], [], []
    for l in range(n_a):
        mixer = (w["a_conv_w"][l], w["a_conv_b"][l], w["a_w_gate_a"][l], w["a_b_gate_a"][l],
                 w["a_w_gate_x"][l], w["a_b_gate_x"][l], w["a_lambda"][l])
        hgp, nc, nh, proj_s = _a_front_call(xnp, xns, w["a_w_in"], l, conv_p[l], h_p[l], *mixer, n, t)
        conv_out_p.append(nc)
        h_out_p.append(nh)
        hgs, nc, nh = _rglru_step_call(proj_s, conv_s[l], h_s[l], *mixer)
        conv_out_s.append(nc)
        h_out_s.append(nh)
        gains = [w["a_pre_g"][l + 1]] if l + 1 < n_a else [w["kv_norm_g"], w["b_pre_g"][0]]
        xp2, xnps, ys = _out_res_call(hgp, hgs, w["a_w_out"][l].astype(BF16), xp2, w["a_post_g"][l], gains,
                                      name="a_out_proj")
        xs2, xnss = _resnorm_call(ys, xs2, w["a_post_g"][l], gains)
        xnp, xns = xnps[0], xnss[0]

    kvp, kvp_heads, kvs = _mm_call(xnp, xns, w["w_kv"], heads=True, name="kv_proj")
    kv_out_p = []
    for g in range(N_GROUPS):
        keep = min(WINDOWS[g], t)
        kept = kvp.reshape(n, t, N_GROUPS * KV_GROUP_W)[:, t - keep:, g * KV_GROUP_W:(g + 1) * KV_GROUP_W]
        kv_out_p.append(kept.reshape(n, keep, 2, KV_HEADS, HEAD_DIM))
    kvs4 = kvs.reshape(ns, 1, N_GROUPS, 2, KV_HEADS, HEAD_DIM)
    kv_out_s = [jnp.concatenate([kv_bufs[g][:, 1:], kvs4[:, :, g]], axis=1) for g in range(N_GROUPS)]
    bias_p = _prompt_bias(w["rel_bias"])
    bias_s = _sample_bias(w["rel_bias"])

    xnp, xns = xnps[1], xnss[1]
    for l in range(n_b):
        gate_p, gate_s = _mm_call(xnp, xns, w["b_w_in"], layer=l, cols=(Q_W, ATTN_W), name="gate_proj")
        q_heads, q_s = _mm_call(xnp, xns, w["b_w_in"], layer=l, cols=(0, Q_W), natural=False, heads=True,
                                name="q_proj")
        ogp = _attn_prompt_call(q_heads, kvp_heads, bias_p, gate_p, n, t)
        ogs = _attn_sample_call(q_s, kvs, kv_bufs, bias_s, gate_s)
        gains = [w["b_pre_g"][l + 1]] if l + 1 < n_b else []
        xp2, xnps, ys = _out_res_call(ogp, ogs, w["b_w_out"][l].astype(BF16), xp2, w["b_post_g"][l], gains,
                                      name="b_out_proj")
        xs2, xnss = _resnorm_call(ys, xs2, w["b_post_g"][l], gains)
        if gains:
            xnp, xns = xnps[0], xnss[0]
    prompt_out = (xp2.reshape(n, t, d), jnp.stack(conv_out_p), jnp.stack(h_out_p), *kv_out_p)
    sample_out = (xs2.reshape(ns, 1, d), jnp.stack(conv_out_s), jnp.stack(h_out_s), *kv_out_s)
    return prompt_out, sample_out


def kernel(x_prompt, x_sample, state_conv, state_h, state_kv_w128, state_kv_w512, state_kv_w2048,
           a_pre_g, a_w_in, a_conv_w, a_conv_b, a_w_gate_a, a_b_gate_a, a_w_gate_x, a_b_gate_x,
           a_lambda, a_w_out, a_post_g, kv_norm_g, w_kv, rel_bias, b_pre_g, b_w_in, b_w_out, b_post_g):
    w = dict(
        a_pre_g=a_pre_g, a_w_in=a_w_in, a_conv_w=a_conv_w, a_conv_b=a_conv_b,
        a_w_gate_a=a_w_gate_a, a_b_gate_a=a_b_gate_a, a_w_gate_x=a_w_gate_x, a_b_gate_x=a_b_gate_x,
        a_lambda=a_lambda, a_w_out=a_w_out, a_post_g=a_post_g, kv_norm_g=kv_norm_g,
        w_kv=w_kv, rel_bias=rel_bias, b_pre_g=b_pre_g, b_w_in=b_w_in, b_w_out=b_w_out, b_post_g=b_post_g)
    nb_p = x_prompt.shape[0]
    n_a = a_w_in.shape[0]
    d_rnn = a_conv_w.shape[-1]
    conv0 = jnp.zeros((n_a, nb_p, CONV_W - 1, d_rnn), F32)
    h0 = jnp.zeros((n_a, nb_p, d_rnn), F32)
    prompt_out, sample_out = _trunks(x_prompt, x_sample, conv0, h0, state_conv, state_h,
                                     (state_kv_w128, state_kv_w512, state_kv_w2048), w)
    return (prompt_out[0], sample_out[0], *prompt_out[1:], *sample_out[1:])
```

```python
import functools
import math

import numpy as np
import jax
import jax.numpy as jnp
from jax import lax
from jax.experimental import pallas as pl
from jax.experimental.pallas import tpu as pltpu

F32 = jnp.float32
BF16 = jnp.bfloat16

EPS = 1e-6
LRU_C = 8.0
CONV_W = 4
HEAD_DIM = 128
N_SLOTS = 16
KV_HEADS = 4
GQA = N_SLOTS // KV_HEADS
WINDOWS = (128, 512, 2048)
DILATIONS = (1, 4, 16)
N_GROUPS = 3
BAND = 128
ATTN_W = N_SLOTS * HEAD_DIM
Q_W = N_GROUPS * ATTN_W
KV_GROUP_W = 2 * KV_HEADS * HEAD_DIM
N_BUCKETS = 32
MAX_EXACT = 16
MAX_DIST = 2048
SM_SCALE = HEAD_DIM ** -0.5

ATTN_UNROLL = 8
LANES = 128
VMEM_LIMIT = 56 * 1024 * 1024


def _params(*sem):
    return pltpu.CompilerParams(dimension_semantics=sem, vmem_limit_bytes=VMEM_LIMIT)


def _tile(dim, pref):
    if dim <= pref:
        return dim
    t = pref
    while dim % t:
        t //= 2
    return t


def _rms(x, g):
    return x * lax.rsqrt(jnp.mean(x * x, axis=-1, keepdims=True) + EPS) * g


def _norm_kernel(x_ref, g_ref, *o_refs):
    x = x_ref[...]
    xhat = x * lax.rsqrt(jnp.mean(x * x, axis=-1, keepdims=True) + EPS)
    for k, o_ref in enumerate(o_refs):
        o_ref[...] = (xhat * g_ref[k:k + 1, :]).astype(o_ref.dtype)


def _norm_call(x, gains):
    m, d = x.shape
    tm = _tile(m, 256)
    g = jnp.stack(gains)
    ng = len(gains)
    row = pl.BlockSpec((tm, d), lambda i: (i, 0))
    return pl.pallas_call(
        _norm_kernel,
        grid=(m // tm,),
        in_specs=[row, pl.BlockSpec((ng, d), lambda i: (0, 0))],
        out_specs=[row] * ng,
        out_shape=[jax.ShapeDtypeStruct((m, d), BF16)] * ng,
        compiler_params=_params("parallel"),
        name="rmsnorm",
    )(x, g)


def _resnorm_kernel(y_ref, x_ref, gp_ref, g_ref, xo_ref, *o_refs):
    xn = x_ref[...] + _rms(y_ref[...], gp_ref[...])
    xo_ref[...] = xn
    if o_refs:
        xhat = xn * lax.rsqrt(jnp.mean(xn * xn, axis=-1, keepdims=True) + EPS)
        for k, o_ref in enumerate(o_refs):
            o_ref[...] = (xhat * g_ref[k:k + 1, :]).astype(o_ref.dtype)


def _resnorm_call(y, x, g_post, gains):
    m, d = x.shape
    tm = _tile(m, 256)
    ng = len(gains)
    g = jnp.stack(gains) if gains else jnp.ones((1, d), F32)
    row = pl.BlockSpec((tm, d), lambda i: (i, 0))
    outs = pl.pallas_call(
        _resnorm_kernel,
        grid=(m // tm,),
        in_specs=[row, row, pl.BlockSpec((1, d), lambda i: (0, 0)),
                  pl.BlockSpec((g.shape[0], d), lambda i: (0, 0))],
        out_specs=[row] * (1 + ng),
        out_shape=[jax.ShapeDtypeStruct((m, d), F32)] + [jax.ShapeDtypeStruct((m, d), BF16)] * ng,
        compiler_params=_params("parallel"),
        name="residual_rmsnorm",
    )(y, x, g_post.reshape(1, d), g)
    return outs[0], list(outs[1:])


def _mm_kernel(a_ref, w_ref, a2_ref, *refs, natural, heads):
    *o_refs, o2_ref, wb_ref = refs

    @pl.when(pl.program_id(1) == 0)
    def _():
        wb_ref[...] = w_ref[...].astype(BF16)
        o2_ref[...] = jnp.dot(a2_ref[...], wb_ref[...], preferred_element_type=F32)

    y = jnp.dot(a_ref[...], wb_ref[...], preferred_element_type=F32)
    k = 0
    if natural:
        o_refs[k][...] = y
        k += 1
    if heads:
        for h in range(y.shape[1] // LANES):
            o_refs[k][h] = y[:, h * LANES:(h + 1) * LANES]


def _mm_call(a, a2, w, *, layer=0, cols=None, natural=True, heads=False, tm=1024, tn=512, name="matmul"):
    if w.ndim == 2:
        w = w[None]
    m, kdim = a.shape
    m2 = a2.shape[0]
    col0, n = cols if cols is not None else (0, w.shape[2])
    tm = _tile(m, tm)
    tn = _tile(n, tn)
    assert col0 % tn == 0
    out_specs, out_shape = [], []
    if natural:
        out_specs.append(pl.BlockSpec((tm, tn), lambda j, i: (i, j)))
        out_shape.append(jax.ShapeDtypeStruct((m, n), F32))
    if heads:
        out_specs.append(pl.BlockSpec((tn // LANES, tm, LANES), lambda j, i: (j, i, 0)))
        out_shape.append(jax.ShapeDtypeStruct((n // LANES, m, LANES), F32))
    out_specs.append(pl.BlockSpec((m2, tn), lambda j, i: (0, j)))
    out_shape.append(jax.ShapeDtypeStruct((m2, n), F32))
    return pl.pallas_call(
        functools.partial(_mm_kernel, natural=natural, heads=heads),
        grid=(n // tn, m // tm),
        in_specs=[pl.BlockSpec((tm, kdim), lambda j, i: (i, 0)),
                  pl.BlockSpec((None, kdim, tn), lambda j, i: (layer, 0, col0 // tn + j)),
                  pl.BlockSpec((m2, kdim), lambda j, i: (0, 0))],
        out_specs=out_specs,
        out_shape=out_shape,
        scratch_shapes=[pltpu.VMEM((kdim, tn), BF16)],
        compiler_params=_params("parallel", "arbitrary"),
        name=name,
    )(a, w, a2)


def _lane_fold_sq(v):
    return sum(v[:, s:s + LANES] * v[:, s:s + LANES] for s in range(0, v.shape[1], LANES))


def _times_rows(v, scale):
    return jnp.concatenate([v[:, s:s + LANES] * scale for s in range(0, v.shape[1], LANES)], axis=1)


def _out_res_kernel(a_ref, w_ref, a2_ref, x_ref, gp_ref, g_ref, xo_ref, *refs, n_row, n_gain, width):
    xn_refs = refs[:n_gain]
    o2_ref, y_scr, xnew_scr, ssq_y, ssq_x, sc_y, sc_x = refs[n_gain:]
    i = pl.program_id(0)
    j = pl.program_id(1)
    tn = w_ref.shape[1]
    cols = pl.ds(pl.multiple_of(j * tn, tn), tn)

    @pl.when((i == 0) & (j == 0))
    def _():
        ssq_x[...] = jnp.zeros(ssq_x.shape, F32)
        ssq_y[...] = jnp.zeros(ssq_y.shape, F32)

    @pl.when(j == 0)
    def _():
        for ssq, sc in ((ssq_x, sc_x), (ssq_y, sc_y)):
            mean = jnp.sum(ssq[...], axis=-1, keepdims=True) * (1.0 / width)
            sc[...] = jnp.broadcast_to(lax.rsqrt(mean + EPS), sc.shape)
            ssq[...] = jnp.zeros(ssq.shape, F32)

    if n_gain:
        @pl.when(i >= 2)
        def _():
            xhat = _times_rows(xnew_scr[:, cols], sc_x[...])
            for k in range(n_gain):
                xn_refs[k][...] = (xhat * g_ref[k:k + 1, :]).astype(xn_refs[k].dtype)

    @pl.when((i >= 1) & (i <= n_row))
    def _():
        xnew = x_ref[...] + _times_rows(y_scr[:, cols], sc_y[...]) * gp_ref[...]
        xo_ref[...] = xnew
        xnew_scr[:, cols] = xnew
        ssq_x[...] += _lane_fold_sq(xnew)

    @pl.when(i < n_row)
    def _():
        y = jnp.dot(a_ref[...], w_ref[...], preferred_element_type=F32)
        y_scr[:, cols] = y
        ssq_y[...] += _lane_fold_sq(y)

    @pl.when(i == 0)
    def _():
        o2_ref[...] = jnp.dot(a2_ref[...], w_ref[...], preferred_element_type=F32)


def _out_res_vmem_bytes(tm, tn, kdim, n, ng):
    return 2 * tm * n * 4 + 2 * (tm * kdim * 2 + kdim * tn * 2 + 2 * tm * tn * 4 + ng * tm * tn * 2)


def _out_res_call(a, a2, w, layer, x, g_post, gains, *, tm=512, name="out_proj_residual"):
    m, kdim = a.shape
    n = w.shape[2]
    m2 = a2.shape[0]
    ng = len(gains)
    tn = next(c for c in (1024, 512, 256, 128) if n % c == 0
              and _out_res_vmem_bytes(tm, c, kdim, n, ng) <= VMEM_LIMIT * 4 // 5)
    assert m % tm == 0
    n_row, nj = m // tm, n // tn
    g = jnp.stack(gains) if gains else jnp.ones((1, n), F32)
    last = nj - 1
    lag1 = lambda i, j: (jnp.clip(i - 1, 0, n_row - 1), jnp.where(i < 1, 0, jnp.where(i <= n_row, j, last)))
    lag2 = lambda i, j: (jnp.clip(i - 2, 0, n_row - 1), jnp.where(i < 2, 0, j))
    outs = pl.pallas_call(
        functools.partial(_out_res_kernel, n_row=n_row, n_gain=ng, width=n),
        grid=(n_row + 2, nj),
        in_specs=[
            pl.BlockSpec((tm, kdim), lambda i, j: (jnp.minimum(i, n_row - 1), 0)),
            pl.BlockSpec((None, kdim, tn), lambda i, j: (layer, 0, jnp.where(i < n_row, j, last))),
            pl.BlockSpec((m2, kdim), lambda i, j: (0, 0)),
            pl.BlockSpec((tm, tn), lag1),
            pl.BlockSpec((1, tn), lambda i, j: (0, j)),
            pl.BlockSpec((g.shape[0], tn), lambda i, j: (0, j)),
        ],
        out_specs=[pl.BlockSpec((tm, tn), lag1)] + [pl.BlockSpec((tm, tn), lag2)] * ng + [
            pl.BlockSpec((m2, tn), lambda i, j: (0, jnp.where(i == 0, j, last)))],
        out_shape=[jax.ShapeDtypeStruct((m, n), F32)] + [jax.ShapeDtypeStruct((m, n), BF16)] * ng + [
            jax.ShapeDtypeStruct((m2, n), F32)],
        scratch_shapes=[pltpu.VMEM((tm, n), F32), pltpu.VMEM((tm, n), F32)] + [pltpu.VMEM((tm, LANES), F32)] * 4,
        compiler_params=_params("arbitrary", "arbitrary"),
        name=name,
    )(a, w, a2, x, g_post.reshape(1, n), g)
    return outs[0], list(outs[1:1 + ng]), outs[-1]


def _rglru_kernel(xb_ref, gate_ref, cs_ref, h0_ref, cw_ref, cb_ref, wga_ref, bga_ref, wgx_ref, bgx_ref,
                  lam_ref, hg_ref, nc_ref, nh_ref, *, blk):
    tail = CONV_W - 1
    xb = xb_ref[...]
    xconv = cb_ref[...]
    for k in range(tail):
        xconv = xconv + cs_ref[:, k, :] * cw_ref[k:k + 1, :]
        if k:
            nc_ref[:, k - 1, :] = cs_ref[:, k, :]
    xconv = xconv + xb * cw_ref[tail:tail + 1, :]
    nc_ref[:, tail - 1, :] = xb

    r_parts, i_parts = [], []
    for hh in range(xconv.shape[1] // blk):
        xh = xconv[:, hh * blk:(hh + 1) * blk].astype(BF16)
        r_parts.append(jnp.dot(xh, wga_ref[hh], preferred_element_type=F32))
        i_parts.append(jnp.dot(xh, wgx_ref[hh], preferred_element_type=F32))
    r = jax.nn.sigmoid(jnp.concatenate(r_parts, axis=1) + bga_ref[...])
    i = jax.nn.sigmoid(jnp.concatenate(i_parts, axis=1) + bgx_ref[...])

    neg_lam = -lam_ref[...]
    softplus = jnp.maximum(neg_lam, 0.0) + jnp.log1p(jnp.exp(-jnp.abs(neg_lam)))
    a = jnp.exp((-LRU_C * softplus) * r)
    h = a * h0_ref[...] + jnp.sqrt(1.0 - a * a) * (i * xconv)
    nh_ref[...] = h
    hg_ref[...] = (h * jax.nn.silu(gate_ref[...])).astype(hg_ref.dtype)


def _rglru_step_call(proj, conv_state, h0, conv_w, conv_b, w_ga, b_ga, w_gx, b_gx, lam):
    _, n, c = proj.shape
    heads, blk, _ = w_ga.shape
    cb = _tile(c, max(1024, blk))
    assert cb % blk == 0 and blk % LANES == 0
    ncb = c // cb
    row = lambda a: a.reshape(1, c)
    vec = pl.BlockSpec((1, cb), lambda j: (0, j))
    rows = pl.BlockSpec((n, cb), lambda j: (0, j))
    conv_rows = pl.BlockSpec((n, CONV_W - 1, cb), lambda j: (0, 0, j))
    wspec = pl.BlockSpec((cb // blk, blk, blk), lambda j: (j, 0, 0))
    return pl.pallas_call(
        functools.partial(_rglru_kernel, blk=blk),
        grid=(ncb,),
        in_specs=[pl.BlockSpec((None, n, cb), lambda j: (0, 0, j)), pl.BlockSpec((None, n, cb), lambda j: (1, 0, j)),
                  conv_rows, rows,
                  pl.BlockSpec((CONV_W, cb), lambda j: (0, j)), vec, wspec, vec, wspec, vec, vec],
        out_specs=[rows, conv_rows, rows],
        out_shape=[
            jax.ShapeDtypeStruct((n, c), BF16),
            jax.ShapeDtypeStruct((n, CONV_W - 1, c), F32),
            jax.ShapeDtypeStruct((n, c), F32),
        ],
        compiler_params=_params("parallel"),
        name="rglru_step",
    )(proj, proj, conv_state, h0, conv_w, row(conv_b), w_ga.astype(BF16), row(b_ga),
      w_gx.astype(BF16), row(b_gx), row(lam))


def _scan_rows(a, b, h):
    t, c = a.shape
    a3 = a.reshape(t // 8, 8, c)
    b3 = b.reshape(t // 8, 8, c)
    row = lax.broadcasted_iota(jnp.int32, a3.shape, 1)
    for d in (1, 2, 4):
        keep = row >= d
        a_prev = jnp.where(keep, pltpu.roll(a3, d, axis=1), 1.0)
        b_prev = jnp.where(keep, pltpu.roll(b3, d, axis=1), 0.0)
        b3 = a3 * b_prev + b3
        a3 = a3 * a_prev
    out = []
    for g in range(t // 8):
        hg = a3[g] * h + b3[g]
        out.append(hg)
        h = hg[7:8, :]
    return jnp.concatenate(out, axis=0), h


def _a_front_kernel(xn_ref, wx_ref, wg_ref, xs_ref, cs_ref, h0_ref, cw_ref, cb_ref, wga_ref, bga_ref, wgx_ref,
                    bgx_ref, lam_ref, hg_ref, nc_ref, nh_ref, ps_ref, wb_ref, tail_ref, hc_ref, pr_ref,
                    *, tt, sub, blk):
    b = pl.program_id(1)
    ti = pl.program_id(2)
    tail = CONV_W - 1
    n_sub = tt // sub
    cb = cs_ref.shape[1]

    @pl.when((b == 0) & (ti == 0))
    def _():
        wb_ref[:, :cb] = wx_ref[...].astype(BF16)
        wb_ref[:, cb:] = wg_ref[...].astype(BF16)
        ps = jnp.dot(xs_ref[...], wb_ref[...], preferred_element_type=F32)
        ps_ref[0] = ps[:, :cb]
        ps_ref[1] = ps[:, cb:]

    @pl.when(ti == 0)
    def _():
        tail_ref[...] = jnp.concatenate([jnp.zeros((8 - tail, cb), F32), cs_ref[...]], axis=0)
        hc_ref[...] = h0_ref[...]

    neg_lam = -lam_ref[...]
    softplus = jnp.maximum(neg_lam, 0.0) + jnp.log1p(jnp.exp(-jnp.abs(neg_lam)))
    log_a_scale = -LRU_C * softplus
    prev8 = tail_ref[...]
    h = hc_ref[...]

    def project(k):
        pr_ref[k] = jnp.dot(xn_ref[k * sub:(k + 1) * sub, :], wb_ref[...], preferred_element_type=F32)

    project(0)
    for k in range(n_sub):
        if k + 1 < n_sub:
            project(k + 1)
        xb = pr_ref[k, :, :cb]
        ext = jnp.concatenate([prev8, xb], axis=0)
        prev8 = xb[sub - 8:, :]
        ext1 = pltpu.roll(ext, 1, axis=0)
        pair = ext * cw_ref[1:2, :] + ext1 * cw_ref[0:1, :]
        xconv = (cb_ref[...] + pltpu.roll(pair, 2, axis=0)[8:, :]) + (ext1[8:, :] * cw_ref[2:3, :] + xb * cw_ref[3:4, :])
        r_parts, i_parts = [], []
        for hh in range(cb // blk):
            xh = xconv[:, hh * blk:(hh + 1) * blk].astype(BF16)
            r_parts.append(jnp.dot(xh, wga_ref[hh], preferred_element_type=F32))
            i_parts.append(jnp.dot(xh, wgx_ref[hh], preferred_element_type=F32))
        r = jax.nn.sigmoid(jnp.concatenate(r_parts, axis=1) + bga_ref[...])
        i = jax.nn.sigmoid(jnp.concatenate(i_parts, axis=1) + bgx_ref[...])
        a = jnp.exp(log_a_scale * r)
        bb = jnp.sqrt(1.0 - a * a) * (i * xconv)
        h_all, h = _scan_rows(a, bb, h)
        hg_ref[k * sub:(k + 1) * sub, :] = (h_all * jax.nn.silu(pr_ref[k, :, cb:])).astype(hg_ref.dtype)
    tail_ref[...] = prev8
    hc_ref[...] = h

    @pl.when(ti == pl.num_programs(2) - 1)
    def _():
        nc_ref[...] = prev8[8 - tail:, :]
        nh_ref[...] = h


def _a_front_call(xn, xs, w_in, layer, conv_state, h0, conv_w, conv_b, w_ga, b_ga, w_gx, b_gx, lam, n, t):
    d = xn.shape[1]
    c = w_in.shape[2] // 2
    heads, blk, _ = w_ga.shape
    cb = max(512, blk)
    tt, sub = 512, 128
    assert c % cb == 0 and cb % blk == 0 and blk % LANES == 0 and t % tt == 0
    ncb = c // cb
    ns = xs.shape[0]
    row = lambda a: a.reshape(1, c)
    vec = pl.BlockSpec((1, cb), lambda j, b, i: (0, j))
    wspec = pl.BlockSpec((cb // blk, blk, blk), lambda j, b, i: (j, 0, 0))
    hg, nc, nh, ps = pl.pallas_call(
        functools.partial(_a_front_kernel, tt=tt, sub=sub, blk=blk),
        grid=(ncb, n, t // tt),
        in_specs=[
            pl.BlockSpec((None, tt, d), lambda j, b, i: (b, i, 0)),
            pl.BlockSpec((None, d, cb), lambda j, b, i: (layer, 0, j)),
            pl.BlockSpec((None, d, cb), lambda j, b, i: (layer, 0, ncb + j)),
            pl.BlockSpec((ns, d), lambda j, b, i: (0, 0)),
            pl.BlockSpec((None, CONV_W - 1, cb), lambda j, b, i: (b, 0, j)),
            pl.BlockSpec((None, 1, cb), lambda j, b, i: (b, 0, j)),
            pl.BlockSpec((CONV_W, cb), lambda j, b, i: (0, j)),
            vec, wspec, vec, wspec, vec, vec,
        ],
        out_specs=[
            pl.BlockSpec((None, tt, cb), lambda j, b, i: (b, i, j)),
            pl.BlockSpec((None, CONV_W - 1, cb), lambda j, b, i: (b, 0, j)),
            pl.BlockSpec((None, 1, cb), lambda j, b, i: (b, 0, j)),
            pl.BlockSpec((2, ns, cb), lambda j, b, i: (0, 0, j)),
        ],
        out_shape=[
            jax.ShapeDtypeStruct((n, t, c), BF16),
            jax.ShapeDtypeStruct((n, CONV_W - 1, c), F32),
            jax.ShapeDtypeStruct((n, 1, c), F32),
            jax.ShapeDtypeStruct((2, ns, c), F32),
        ],
        scratch_shapes=[
            pltpu.VMEM((d, 2 * cb), BF16),
            pltpu.VMEM((8, cb), F32),
            pltpu.VMEM((1, cb), F32),
            pltpu.VMEM((tt // sub, sub, 2 * cb), F32),
        ],
        compiler_params=_params("parallel", "arbitrary", "arbitrary"),
        name="a_in_rglru",
    )(xn.reshape(n, t, d), w_in, w_in, xs, conv_state, h0.reshape(n, 1, c), conv_w, row(conv_b),
      w_ga.astype(BF16), row(b_ga), w_gx.astype(BF16), row(b_gx), row(lam))
    return hg.reshape(n * t, c), nc, nh.reshape(n, c), ps


def _rel_bucket(dist):
    dist = np.asarray(dist)
    d = np.maximum(dist, 1).astype(np.float32)
    large = MAX_EXACT + (np.log(d / MAX_EXACT) / np.log(MAX_DIST / MAX_EXACT)
                         * (N_BUCKETS - MAX_EXACT)).astype(np.int32)
    large = np.minimum(large, N_BUCKETS - 1)
    return np.where(dist < MAX_EXACT, dist, large).astype(np.int32)


def _prompt_bias(rel_bias):
    span = 3 * BAND - 1
    neg = jnp.full((N_SLOTS, BAND - 1), -jnp.inf, F32)
    tables = []
    for g, r in enumerate(DILATIONS):
        tg = rel_bias[:, g * N_SLOTS:(g + 1) * N_SLOTS].astype(F32)
        tb = tg[_rel_bucket(np.arange(BAND + 1) * r)].T
        wr = jnp.concatenate([neg, tb[:, ::-1], neg, neg[:, :1]], axis=1)
        skew = jnp.tile(wr, (1, BAND))[:, :BAND * span].reshape(N_SLOTS, BAND, span)
        tables.append(skew[:, :, BAND - 1:])
    return jnp.stack(tables)


def _sample_bias(rel_bias):
    tables = []
    for g, r in enumerate(DILATIONS):
        steps = np.concatenate([BAND - np.arange(BAND), np.zeros(BAND, np.int64)])
        tg = rel_bias[:, g * N_SLOTS:(g + 1) * N_SLOTS].astype(F32)
        tables.append(tg[_rel_bucket(steps * r)].T)
    return jnp.stack(tables)


def _attn_prompt_kernel(q_ref, k_ref, v_ref, bias_ref, gate_ref, o_ref, acc_ref, m_ref, l_ref, *, t):
    g = pl.program_id(2)

    def unit(start, gi, r, has_prev):
        nk = 2 * BAND if has_prev else BAND
        kstart = start - BAND * r if has_prev else start
        rows = lambda s0, cnt: pl.ds(s0, cnt, stride=r) if r > 1 else pl.ds(s0, cnt)
        k = k_ref[rows(kstart, nk), :].astype(BF16)
        v = v_ref[rows(kstart, nk), :].astype(BF16)
        q = jnp.concatenate([q_ref[s, rows(start, BAND), :] for s in range(GQA)], axis=0).astype(BF16)
        sc = lax.dot_general(q, k, (((1,), (1,)), ((), ())), preferred_element_type=F32) * SM_SCALE
        sc = sc + bias_ref[:, :, 2 * BAND - nk:].reshape(GQA * BAND, nk)
        m = jnp.max(sc, axis=-1, keepdims=True)
        p = jnp.exp(sc - m)
        l = jnp.sum(p, axis=-1, keepdims=True)
        pv = jnp.dot(p.astype(BF16), v, preferred_element_type=F32)
        lane = lax.broadcasted_iota(jnp.int32, (BAND, LANES), 1)
        m_tile = jnp.zeros((BAND, LANES), F32)
        l_tile = jnp.ones((BAND, LANES), F32)
        for s in range(GQA):
            blk = slice(s * BAND, (s + 1) * BAND)
            acc_ref[gi, s, rows(start, BAND), :] = pv[blk]
            m_tile = jnp.where(lane == s, m[blk], m_tile)
            l_tile = jnp.where(lane == s, l[blk], l_tile)
        m_ref[gi, rows(start, BAND), :] = m_tile
        l_ref[gi, rows(start, BAND), :] = l_tile

    def group_pass(gi, r):
        nb = t // (BAND * r)

        def head_block(cls, carry):
            unit(cls, gi, r, False)
            return carry

        lax.fori_loop(0, r, head_block, 0, unroll=min(r, ATTN_UNROLL))

        def later_block(u, carry):
            cls = u % r
            blk = 1 + u // r
            unit(blk * (BAND * r) + cls, gi, r, True)
            return carry

        if nb > 1:
            lax.fori_loop(0, r * (nb - 1), later_block, 0, unroll=ATTN_UNROLL)

    for gi, r in enumerate(DILATIONS):
        pl.when(g == gi)(functools.partial(group_pass, gi, r))

    @pl.when(g == N_GROUPS - 1)
    def _():
        def combine(c, carry):
            rows = pl.ds(pl.multiple_of(c * BAND, BAND), BAND)
            ms = [m_ref[gi, rows, :] for gi in range(N_GROUPS)]
            mx = functools.reduce(jnp.maximum, ms)
            es = [jnp.exp(mg - mx) for mg in ms]
            den = sum(es[gi] * l_ref[gi, rows, :] for gi in range(N_GROUPS))
            ws = [e / den for e in es]
            for s in range(GQA):
                o = sum(ws[gi][:, s:s + 1] * acc_ref[gi, s, rows, :] for gi in range(N_GROUPS))
                cols = slice(s * HEAD_DIM, (s + 1) * HEAD_DIM)
                o_ref[rows, cols] = (o * jax.nn.silu(gate_ref[rows, cols])).astype(o_ref.dtype)
            return carry

        lax.fori_loop(0, t // BAND, combine, 0)


def _attn_prompt_call(q_heads, kv_heads, bias, gate, n, t):
    assert all(t % (BAND * r) == 0 for r in DILATIONS)
    q5 = q_heads.reshape(N_GROUPS, KV_HEADS, GQA, n, t, HEAD_DIM)
    kv6 = kv_heads.reshape(N_GROUPS, 2, KV_HEADS, n, t, HEAD_DIM)
    gate3 = gate.reshape(n, t, ATTN_W)
    kv_spec = lambda which: pl.BlockSpec((None, None, None, None, t, HEAD_DIM),
                                         lambda b, h, g: (g, which, h, b, 0, 0))
    out = pl.pallas_call(
        functools.partial(_attn_prompt_kernel, t=t),
        grid=(n, KV_HEADS, N_GROUPS),
        in_specs=[
            pl.BlockSpec((None, None, GQA, None, t, HEAD_DIM), lambda b, h, g: (g, h, 0, b, 0, 0)),
            kv_spec(0), kv_spec(1),
            pl.BlockSpec((None, GQA, BAND, 2 * BAND), lambda b, h, g: (g, h, 0, 0)),
            pl.BlockSpec((None, t, GQA * HEAD_DIM), lambda b, h, g: (b, 0, h)),
        ],
        out_specs=pl.BlockSpec((None, t, GQA * HEAD_DIM), lambda b, h, g: (b, 0, h)),
        out_shape=jax.ShapeDtypeStruct((n, t, ATTN_W), BF16),
        scratch_shapes=[pltpu.VMEM((N_GROUPS, GQA, t, HEAD_DIM), F32),
                        pltpu.VMEM((N_GROUPS, t, LANES), F32),
                        pltpu.VMEM((N_GROUPS, t, LANES), F32)],
        compiler_params=_params("parallel", "parallel", "arbitrary"),
        name="attn_prompt",
    )(q5, kv6, kv6, bias, gate3)
    return out.reshape(n * t, ATTN_W)


def _attn_sample_kernel(q_ref, kvn_ref, s0_ref, s1_ref, s2_ref, bias_ref, gate_ref, o_ref):
    st_refs = (s0_ref, s1_ref, s2_ref)
    bf = lambda x: x.astype(BF16)
    rnd = lambda x: x.astype(BF16).astype(F32)
    for h in range(KV_HEADS):
        sl = slice(h * GQA, (h + 1) * GQA)
        s_old, s_new, v_old, v_new = [], [], [], []
        for g in range(N_GROUPS):
            q = q_ref[g * N_SLOTS + h * GQA:g * N_SLOTS + (h + 1) * GQA, :]
            k_old = st_refs[g][:, 0, h, :]
            v_old.append(st_refs[g][:, 1, h, :])
            row = g * 2 * KV_HEADS + h
            k_new = kvn_ref[row:row + 1, :]
            v_new.append(kvn_ref[row + KV_HEADS:row + KV_HEADS + 1, :])
            so = lax.dot_general(bf(q), bf(k_old), (((1,), (1,)), ((), ())), preferred_element_type=F32)
            s_old.append(so * SM_SCALE + bias_ref[g, sl, :BAND])
            sn = jnp.sum(rnd(q) * rnd(k_new), axis=-1, keepdims=True)
            s_new.append(sn * SM_SCALE + bias_ref[g, sl, BAND:BAND + 1])
        m = functools.reduce(jnp.maximum, [jnp.max(x, axis=-1, keepdims=True) for x in s_old] + s_new)
        num = jnp.zeros((GQA, HEAD_DIM), F32)
        den = jnp.zeros((GQA, 1), F32)
        for g in range(N_GROUPS):
            p_old = jnp.exp(s_old[g] - m)
            p_new = jnp.exp(s_new[g] - m)
            den = den + jnp.sum(p_old, axis=-1, keepdims=True) + p_new
            num = num + jnp.dot(bf(p_old), bf(v_old[g]), preferred_element_type=F32) + rnd(p_new) * rnd(v_new[g])
        o_ref[sl, :] = ((num / den) * jax.nn.silu(gate_ref[sl, :])).astype(o_ref.dtype)


def _attn_sample_call(q, kv_new, states, bias, gate):
    n = q.shape[0]
    st_views, st_specs = [], []
    for g, r in enumerate(DILATIONS):
        assert states[g].shape[1] == BAND * r
        st_views.append(states[g].reshape(n, BAND, r, 2, KV_HEADS, HEAD_DIM))
        st_specs.append(pl.BlockSpec((None, BAND, None, 2, KV_HEADS, HEAD_DIM), lambda b: (b, 0, 0, 0, 0, 0)))
    per_b = lambda rows: pl.BlockSpec((None, rows, HEAD_DIM), lambda b: (b, 0, 0))
    out = pl.pallas_call(
        _attn_sample_kernel,
        grid=(n,),
        in_specs=[per_b(N_GROUPS * N_SLOTS), per_b(N_GROUPS * 2 * KV_HEADS)] + st_specs + [
            pl.BlockSpec((N_GROUPS, N_SLOTS, 2 * BAND), lambda b: (0, 0, 0)), per_b(N_SLOTS)],
        out_specs=per_b(N_SLOTS),
        out_shape=jax.ShapeDtypeStruct((n, N_SLOTS, HEAD_DIM), BF16),
        compiler_params=_params("parallel"),
        name="attn_sample",
    )(q.reshape(n, N_GROUPS * N_SLOTS, HEAD_DIM), kv_new.reshape(n, N_GROUPS * 2 * KV_HEADS, HEAD_DIM),
      *st_views, bias, gate.reshape(n, N_SLOTS, HEAD_DIM))
    return out.reshape(n, ATTN_W)


def _trunks(xp, xs, conv_p, h_p, conv_s, h_s, kv_bufs, w):
    n, t, d = xp.shape
    ns = xs.shape[0]
    assert xs.shape[1] == 1
    n_a = w["a_w_in"].shape[0]
    n_b = w["b_w_in"].shape[0]
    xp2 = xp.reshape(n * t, d)
    xs2 = xs.reshape(ns, d)
    (xnp,) = _norm_call(xp2, [w["a_pre_g"][0]])
    (xns,) = _norm_call(xs2, [w["a_pre_g"][0]])
    a_w_out = w["a_w_out"].astype(BF16)
    b_w_out = w["b_w_out"].astype(BF16)
    conv_out_p, h_out_p, conv_out_s, h_out_s = [], [], [], []
    for l in range(n_a):
        mixer = (w["a_conv_w"][l], w["a_conv_b"][l], w["a_w_gate_a"][l], w["a_b_gate_a"][l],
                 w["a_w_gate_x"][l], w["a_b_gate_x"][l], w["a_lambda"][l])
        hgp, nc, nh, proj_s = _a_front_call(xnp, xns, w["a_w_in"], l, conv_p[l], h_p[l], *mixer, n, t)
        conv_out_p.append(nc)
        h_out_p.append(nh)
        hgs, nc, nh = _rglru_step_call(proj_s, conv_s[l], h_s[l], *mixer)
        conv_out_s.append(nc)
        h_out_s.append(nh)
        gains = [w["a_pre_g"][l + 1]] if l + 1 < n_a else [w["kv_norm_g"], w["b_pre_g"][0]]
        xp2, xnps, ys = _out_res_call(hgp, hgs, a_w_out, l, xp2, w["a_post_g"][l], gains,
                                      name="a_out_proj")
        xs2, xnss = _resnorm_call(ys, xs2, w["a_post_g"][l], gains)
        xnp, xns = xnps[0], xnss[0]

    kvp, kvp_heads, kvs = _mm_call(xnp, xns, w["w_kv"], heads=True, name="kv_proj")
    kv_out_p = []
    for g in range(N_GROUPS):
        keep = min(WINDOWS[g], t)
        kept = kvp.reshape(n, t, N_GROUPS * KV_GROUP_W)[:, t - keep:, g * KV_GROUP_W:(g + 1) * KV_GROUP_W]
        kv_out_p.append(kept.reshape(n, keep, 2, KV_HEADS, HEAD_DIM))
    kvs4 = kvs.reshape(ns, 1, N_GROUPS, 2, KV_HEADS, HEAD_DIM)
    kv_out_s = [jnp.concatenate([kv_bufs[g][:, 1:], kvs4[:, :, g]], axis=1) for g in range(N_GROUPS)]
    bias_p = _prompt_bias(w["rel_bias"])
    bias_s = _sample_bias(w["rel_bias"])

    xnp, xns = xnps[1], xnss[1]
    for l in range(n_b):
        gate_p, gate_s = _mm_call(xnp, xns, w["b_w_in"], layer=l, cols=(Q_W, ATTN_W), name="gate_proj")
        q_heads, q_s = _mm_call(xnp, xns, w["b_w_in"], layer=l, cols=(0, Q_W), natural=False, heads=True,
                                name="q_proj")
        ogp = _attn_prompt_call(q_heads, kvp_heads, bias_p, gate_p, n, t)
        ogs = _attn_sample_call(q_s, kvs, kv_bufs, bias_s, gate_s)
        gains = [w["b_pre_g"][l + 1]] if l + 1 < n_b else []
        xp2, xnps, ys = _out_res_call(ogp, ogs, b_w_out, l, xp2, w["b_post_g"][l], gains,
                                      name="b_out_proj")
        xs2, xnss = _resnorm_call(ys, xs2, w["b_post_g"][l], gains)
        if gains:
            xnp, xns = xnps[0], xnss[0]
    prompt_out = (xp2.reshape(n, t, d), jnp.stack(conv_out_p), jnp.stack(h_out_p), *kv_out_p)
    sample_out = (xs2.reshape(ns, 1, d), jnp.stack(conv_out_s), jnp.stack(h_out_s), *kv_out_s)
    return prompt_out, sample_out


def kernel(x_prompt, x_sample, state_conv, state_h, state_kv_w128, state_kv_w512, state_kv_w2048,
           a_pre_g, a_w_in, a_conv_w, a_conv_b, a_w_gate_a, a_b_gate_a, a_w_gate_x, a_b_gate_x,
           a_lambda, a_w_out, a_post_g, kv_norm_g, w_kv, rel_bias, b_pre_g, b_w_in, b_w_out, b_post_g):
    w = dict(
        a_pre_g=a_pre_g, a_w_in=a_w_in, a_conv_w=a_conv_w, a_conv_b=a_conv_b,
        a_w_gate_a=a_w_gate_a, a_b_gate_a=a_b_gate_a, a_w_gate_x=a_w_gate_x, a_b_gate_x=a_b_gate_x,
        a_lambda=a_lambda, a_w_out=a_w_out, a_post_g=a_post_g, kv_norm_g=kv_norm_g,
        w_kv=w_kv, rel_bias=rel_bias, b_pre_g=b_pre_g, b_w_in=b_w_in, b_w_out=b_w_out, b_post_g=b_post_g)
    nb_p = x_prompt.shape[0]
    n_a = a_w_in.shape[0]
    d_rnn = a_conv_w.shape[-1]
    conv0 = jnp.zeros((n_a, nb_p, CONV_W - 1, d_rnn), F32)
    h0 = jnp.zeros((n_a, nb_p, d_rnn), F32)
    prompt_out, sample_out = _trunks(x_prompt, x_sample, conv0, h0, state_conv, state_h,
                                     (state_kv_w128, state_kv_w512, state_kv_w2048), w)
    return (prompt_out[0], sample_out[0], *prompt_out[1:], *sample_out[1:])
```

```python
import functools
import math

import numpy as np
import jax
import jax.numpy as jnp
from jax import lax
from jax.experimental import pallas as pl
from jax.experimental.pallas import tpu as pltpu

F32 = jnp.float32
BF16 = jnp.bfloat16

EPS = 1e-6
LRU_C = 8.0
CONV_W = 4
HEAD_DIM = 128
N_SLOTS = 16
KV_HEADS = 4
GQA = N_SLOTS // KV_HEADS
WINDOWS = (128, 512, 2048)
DILATIONS = (1, 4, 16)
N_GROUPS = 3
BAND = 128
ATTN_W = N_SLOTS * HEAD_DIM
Q_W = N_GROUPS * ATTN_W
KV_GROUP_W = 2 * KV_HEADS * HEAD_DIM
N_BUCKETS = 32
MAX_EXACT = 16
MAX_DIST = 2048
SM_SCALE = HEAD_DIM ** -0.5

ATTN_UNROLL = 8
LANES = 128
VMEM_LIMIT = 56 * 1024 * 1024


def _params(*sem):
    return pltpu.CompilerParams(dimension_semantics=sem, vmem_limit_bytes=VMEM_LIMIT)


def _tile(dim, pref):
    if dim <= pref:
        return dim
    t = pref
    while dim % t:
        t //= 2
    return t


def _input_scale(a):
    y = 1.0 - a * a
    return jnp.where(y > 0.0, y * lax.rsqrt(y), 0.0)


def _rms(x, g):
    return x * lax.rsqrt(jnp.mean(x * x, axis=-1, keepdims=True) + EPS) * g


def _norm_kernel(x_ref, g_ref, *o_refs):
    x = x_ref[...]
    xhat = x * lax.rsqrt(jnp.mean(x * x, axis=-1, keepdims=True) + EPS)
    for k, o_ref in enumerate(o_refs):
        o_ref[...] = (xhat * g_ref[k:k + 1, :]).astype(o_ref.dtype)


def _norm_call(x, gains):
    m, d = x.shape
    tm = _tile(m, 256)
    g = jnp.stack(gains)
    ng = len(gains)
    row = pl.BlockSpec((tm, d), lambda i: (i, 0))
    return pl.pallas_call(
        _norm_kernel,
        grid=(m // tm,),
        in_specs=[row, pl.BlockSpec((ng, d), lambda i: (0, 0))],
        out_specs=[row] * ng,
        out_shape=[jax.ShapeDtypeStruct((m, d), BF16)] * ng,
        compiler_params=_params("parallel"),
        name="rmsnorm",
    )(x, g)


def _resnorm_kernel(y_ref, x_ref, gp_ref, g_ref, xo_ref, *o_refs):
    xn = x_ref[...] + _rms(y_ref[...], gp_ref[...])
    xo_ref[...] = xn
    if o_refs:
        xhat = xn * lax.rsqrt(jnp.mean(xn * xn, axis=-1, keepdims=True) + EPS)
        for k, o_ref in enumerate(o_refs):
            o_ref[...] = (xhat * g_ref[k:k + 1, :]).astype(o_ref.dtype)


def _resnorm_call(y, x, g_post, gains):
    m, d = x.shape
    tm = _tile(m, 256)
    ng = len(gains)
    g = jnp.stack(gains) if gains else jnp.ones((1, d), F32)
    row = pl.BlockSpec((tm, d), lambda i: (i, 0))
    outs = pl.pallas_call(
        _resnorm_kernel,
        grid=(m // tm,),
        in_specs=[row, row, pl.BlockSpec((1, d), lambda i: (0, 0)),
                  pl.BlockSpec((g.shape[0], d), lambda i: (0, 0))],
        out_specs=[row] * (1 + ng),
        out_shape=[jax.ShapeDtypeStruct((m, d), F32)] + [jax.ShapeDtypeStruct((m, d), BF16)] * ng,
        compiler_params=_params("parallel"),
        name="residual_rmsnorm",
    )(y, x, g_post.reshape(1, d), g)
    return outs[0], list(outs[1:])


def _mm_kernel(a_ref, w_ref, a2_ref, *refs, natural, heads):
    *o_refs, o2_ref, wb_ref = refs

    @pl.when(pl.program_id(1) == 0)
    def _():
        wb_ref[...] = w_ref[...].astype(BF16)
        o2_ref[...] = jnp.dot(a2_ref[...], wb_ref[...], preferred_element_type=F32)

    y = jnp.dot(a_ref[...], wb_ref[...], preferred_element_type=F32)
    k = 0
    if natural:
        o_refs[k][...] = y
        k += 1
    if heads:
        for h in range(y.shape[1] // LANES):
            o_refs[k][h] = y[:, h * LANES:(h + 1) * LANES]


def _mm_call(a, a2, w, *, layer=0, cols=None, natural=True, heads=False, tm=1024, tn=512, name="matmul"):
    if w.ndim == 2:
        w = w[None]
    m, kdim = a.shape
    m2 = a2.shape[0]
    col0, n = cols if cols is not None else (0, w.shape[2])
    tm = _tile(m, tm)
    tn = _tile(n, tn)
    assert col0 % tn == 0
    out_specs, out_shape = [], []
    if natural:
        out_specs.append(pl.BlockSpec((tm, tn), lambda j, i: (i, j)))
        out_shape.append(jax.ShapeDtypeStruct((m, n), F32))
    if heads:
        out_specs.append(pl.BlockSpec((tn // LANES, tm, LANES), lambda j, i: (j, i, 0)))
        out_shape.append(jax.ShapeDtypeStruct((n // LANES, m, LANES), F32))
    out_specs.append(pl.BlockSpec((m2, tn), lambda j, i: (0, j)))
    out_shape.append(jax.ShapeDtypeStruct((m2, n), F32))
    return pl.pallas_call(
        functools.partial(_mm_kernel, natural=natural, heads=heads),
        grid=(n // tn, m // tm),
        in_specs=[pl.BlockSpec((tm, kdim), lambda j, i: (i, 0)),
                  pl.BlockSpec((None, kdim, tn), lambda j, i: (layer, 0, col0 // tn + j)),
                  pl.BlockSpec((m2, kdim), lambda j, i: (0, 0))],
        out_specs=out_specs,
        out_shape=out_shape,
        scratch_shapes=[pltpu.VMEM((kdim, tn), BF16)],
        compiler_params=_params("parallel", "arbitrary"),
        name=name,
    )(a, w, a2)


def _lane_fold_sq(v):
    return sum(v[:, s:s + LANES] * v[:, s:s + LANES] for s in range(0, v.shape[1], LANES))


def _times_rows(v, scale):
    return jnp.concatenate([v[:, s:s + LANES] * scale for s in range(0, v.shape[1], LANES)], axis=1)


def _out_res_kernel(a_ref, w_ref, a2_ref, x_ref, gp_ref, g_ref, xo_ref, *refs, n_row, n_gain, width):
    xn_refs = refs[:n_gain]
    o2_ref, y_scr, xnew_scr, ssq_y, ssq_x, sc_y, sc_x = refs[n_gain:]
    i = pl.program_id(0)
    j = pl.program_id(1)
    tn = w_ref.shape[1]
    cols = pl.ds(pl.multiple_of(j * tn, tn), tn)

    @pl.when((i == 0) & (j == 0))
    def _():
        ssq_x[...] = jnp.zeros(ssq_x.shape, F32)
        ssq_y[...] = jnp.zeros(ssq_y.shape, F32)

    @pl.when(j == 0)
    def _():
        for ssq, sc in ((ssq_x, sc_x), (ssq_y, sc_y)):
            mean = jnp.sum(ssq[...], axis=-1, keepdims=True) * (1.0 / width)
            sc[...] = jnp.broadcast_to(lax.rsqrt(mean + EPS), sc.shape)
            ssq[...] = jnp.zeros(ssq.shape, F32)

    if n_gain:
        @pl.when(i >= 2)
        def _():
            xhat = _times_rows(xnew_scr[:, cols], sc_x[...])
            for k in range(n_gain):
                xn_refs[k][...] = (xhat * g_ref[k:k + 1, :]).astype(xn_refs[k].dtype)

    @pl.when((i >= 1) & (i <= n_row))
    def _():
        xnew = x_ref[...] + _times_rows(y_scr[:, cols], sc_y[...]) * gp_ref[...]
        xo_ref[...] = xnew
        xnew_scr[:, cols] = xnew
        ssq_x[...] += _lane_fold_sq(xnew)

    @pl.when(i < n_row)
    def _():
        y = jnp.dot(a_ref[...], w_ref[...], preferred_element_type=F32)
        y_scr[:, cols] = y
        ssq_y[...] += _lane_fold_sq(y)

    @pl.when(i == 0)
    def _():
        o2_ref[...] = jnp.dot(a2_ref[...], w_ref[...], preferred_element_type=F32)


def _out_res_vmem_bytes(tm, tn, kdim, n, ng):
    return 2 * tm * n * 4 + 2 * (tm * kdim * 2 + kdim * tn * 2 + 2 * tm * tn * 4 + ng * tm * tn * 2)


def _out_res_call(a, a2, w, layer, x, g_post, gains, *, tm=512, name="out_proj_residual"):
    m, kdim = a.shape
    n = w.shape[2]
    m2 = a2.shape[0]
    ng = len(gains)
    tn = next(c for c in (1024, 512, 256, 128) if n % c == 0
              and _out_res_vmem_bytes(tm, c, kdim, n, ng) <= VMEM_LIMIT * 4 // 5)
    assert m % tm == 0
    n_row, nj = m // tm, n // tn
    g = jnp.stack(gains) if gains else jnp.ones((1, n), F32)
    last = nj - 1
    lag1 = lambda i, j: (jnp.clip(i - 1, 0, n_row - 1), jnp.where(i < 1, 0, jnp.where(i <= n_row, j, last)))
    lag2 = lambda i, j: (jnp.clip(i - 2, 0, n_row - 1), jnp.where(i < 2, 0, j))
    outs = pl.pallas_call(
        functools.partial(_out_res_kernel, n_row=n_row, n_gain=ng, width=n),
        grid=(n_row + 2, nj),
        in_specs=[
            pl.BlockSpec((tm, kdim), lambda i, j: (jnp.minimum(i, n_row - 1), 0)),
            pl.BlockSpec((None, kdim, tn), lambda i, j: (layer, 0, jnp.where(i < n_row, j, last))),
            pl.BlockSpec((m2, kdim), lambda i, j: (0, 0)),
            pl.BlockSpec((tm, tn), lag1),
            pl.BlockSpec((1, tn), lambda i, j: (0, j)),
            pl.BlockSpec((g.shape[0], tn), lambda i, j: (0, j)),
        ],
        out_specs=[pl.BlockSpec((tm, tn), lag1)] + [pl.BlockSpec((tm, tn), lag2)] * ng + [
            pl.BlockSpec((m2, tn), lambda i, j: (0, jnp.where(i == 0, j, last)))],
        out_shape=[jax.ShapeDtypeStruct((m, n), F32)] + [jax.ShapeDtypeStruct((m, n), BF16)] * ng + [
            jax.ShapeDtypeStruct((m2, n), F32)],
        scratch_shapes=[pltpu.VMEM((tm, n), F32), pltpu.VMEM((tm, n), F32)] + [pltpu.VMEM((tm, LANES), F32)] * 4,
        compiler_params=_params("arbitrary", "arbitrary"),
        name=name,
    )(a, w, a2, x, g_post.reshape(1, n), g)
    return outs[0], list(outs[1:1 + ng]), outs[-1]


def _rglru_kernel(xb_ref, gate_ref, cs_ref, h0_ref, cw_ref, cb_ref, wga_ref, bga_ref, wgx_ref, bgx_ref,
                  lam_ref, hg_ref, nc_ref, nh_ref, *, blk):
    tail = CONV_W - 1
    xb = xb_ref[...]
    xconv = cb_ref[...]
    for k in range(tail):
        xconv = xconv + cs_ref[:, k, :] * cw_ref[k:k + 1, :]
        if k:
            nc_ref[:, k - 1, :] = cs_ref[:, k, :]
    xconv = xconv + xb * cw_ref[tail:tail + 1, :]
    nc_ref[:, tail - 1, :] = xb

    r_parts, i_parts = [], []
    for hh in range(xconv.shape[1] // blk):
        xh = xconv[:, hh * blk:(hh + 1) * blk].astype(BF16)
        r_parts.append(jnp.dot(xh, wga_ref[hh], preferred_element_type=F32))
        i_parts.append(jnp.dot(xh, wgx_ref[hh], preferred_element_type=F32))
    r = jax.nn.sigmoid(jnp.concatenate(r_parts, axis=1) + bga_ref[...])
    i = jax.nn.sigmoid(jnp.concatenate(i_parts, axis=1) + bgx_ref[...])

    neg_lam = -lam_ref[...]
    softplus = jnp.maximum(neg_lam, 0.0) + jnp.log1p(jnp.exp(-jnp.abs(neg_lam)))
    a = jnp.exp((-LRU_C * softplus) * r)
    h = a * h0_ref[...] + _input_scale(a) * (i * xconv)
    nh_ref[...] = h
    hg_ref[...] = (h * jax.nn.silu(gate_ref[...])).astype(hg_ref.dtype)


def _rglru_step_call(proj, conv_state, h0, conv_w, conv_b, w_ga, b_ga, w_gx, b_gx, lam):
    _, n, c = proj.shape
    heads, blk, _ = w_ga.shape
    cb = _tile(c, max(1024, blk))
    assert cb % blk == 0 and blk % LANES == 0
    ncb = c // cb
    row = lambda a: a.reshape(1, c)
    vec = pl.BlockSpec((1, cb), lambda j: (0, j))
    rows = pl.BlockSpec((n, cb), lambda j: (0, j))
    conv_rows = pl.BlockSpec((n, CONV_W - 1, cb), lambda j: (0, 0, j))
    wspec = pl.BlockSpec((cb // blk, blk, blk), lambda j: (j, 0, 0))
    return pl.pallas_call(
        functools.partial(_rglru_kernel, blk=blk),
        grid=(ncb,),
        in_specs=[pl.BlockSpec((None, n, cb), lambda j: (0, 0, j)), pl.BlockSpec((None, n, cb), lambda j: (1, 0, j)),
                  conv_rows, rows,
                  pl.BlockSpec((CONV_W, cb), lambda j: (0, j)), vec, wspec, vec, wspec, vec, vec],
        out_specs=[rows, conv_rows, rows],
        out_shape=[
            jax.ShapeDtypeStruct((n, c), BF16),
            jax.ShapeDtypeStruct((n, CONV_W - 1, c), F32),
            jax.ShapeDtypeStruct((n, c), F32),
        ],
        compiler_params=_params("parallel"),
        name="rglru_step",
    )(proj, proj, conv_state, h0, conv_w, row(conv_b), w_ga.astype(BF16), row(b_ga),
      w_gx.astype(BF16), row(b_gx), row(lam))


def _scan_rows(a, b, h):
    t, c = a.shape
    a3 = a.reshape(t // 8, 8, c)
    b3 = b.reshape(t // 8, 8, c)
    row = lax.broadcasted_iota(jnp.int32, a3.shape, 1)
    for d in (1, 2, 4):
        keep = row >= d
        a_prev = jnp.where(keep, pltpu.roll(a3, d, axis=1), 1.0)
        b_prev = jnp.where(keep, pltpu.roll(b3, d, axis=1), 0.0)
        b3 = a3 * b_prev + b3
        a3 = a3 * a_prev
    out = []
    for g in range(t // 8):
        hg = a3[g] * h + b3[g]
        out.append(hg)
        h = hg[7:8, :]
    return jnp.concatenate(out, axis=0), h


def _a_front_kernel(xn_ref, wx_ref, wg_ref, xs_ref, cs_ref, h0_ref, cw_ref, cb_ref, wga_ref, bga_ref, wgx_ref,
                    bgx_ref, lam_ref, hg_ref, nc_ref, nh_ref, ps_ref, wb_ref, tail_ref, hc_ref, pr_ref,
                    *, tt, sub, blk):
    b = pl.program_id(1)
    ti = pl.program_id(2)
    tail = CONV_W - 1
    n_sub = tt // sub
    cb = cs_ref.shape[1]

    @pl.when((b == 0) & (ti == 0))
    def _():
        wb_ref[:, :cb] = wx_ref[...].astype(BF16)
        wb_ref[:, cb:] = wg_ref[...].astype(BF16)
        ps = jnp.dot(xs_ref[...], wb_ref[...], preferred_element_type=F32)
        ps_ref[0] = ps[:, :cb]
        ps_ref[1] = ps[:, cb:]

    @pl.when(ti == 0)
    def _():
        tail_ref[...] = jnp.concatenate([jnp.zeros((8 - tail, cb), F32), cs_ref[...]], axis=0)
        hc_ref[...] = h0_ref[...]

    neg_lam = -lam_ref[...]
    softplus = jnp.maximum(neg_lam, 0.0) + jnp.log1p(jnp.exp(-jnp.abs(neg_lam)))
    log_a_scale = -LRU_C * softplus
    prev8 = tail_ref[...]
    h = hc_ref[...]

    def project(k):
        pr_ref[k] = jnp.dot(xn_ref[k * sub:(k + 1) * sub, :], wb_ref[...], preferred_element_type=F32)

    project(0)
    for k in range(n_sub):
        if k + 1 < n_sub:
            project(k + 1)
        xb = pr_ref[k, :, :cb]
        ext = jnp.concatenate([prev8, xb], axis=0)
        prev8 = xb[sub - 8:, :]
        ext1 = pltpu.roll(ext, 1, axis=0)
        pair = ext * cw_ref[1:2, :] + ext1 * cw_ref[0:1, :]
        xconv = (cb_ref[...] + pltpu.roll(pair, 2, axis=0)[8:, :]) + (ext1[8:, :] * cw_ref[2:3, :] + xb * cw_ref[3:4, :])
        r_parts, i_parts = [], []
        for hh in range(cb // blk):
            xh = xconv[:, hh * blk:(hh + 1) * blk].astype(BF16)
            r_parts.append(jnp.dot(xh, wga_ref[hh], preferred_element_type=F32))
            i_parts.append(jnp.dot(xh, wgx_ref[hh], preferred_element_type=F32))
        r = jax.nn.sigmoid(jnp.concatenate(r_parts, axis=1) + bga_ref[...])
        i = jax.nn.sigmoid(jnp.concatenate(i_parts, axis=1) + bgx_ref[...])
        a = jnp.exp(log_a_scale * r)
        bb = _input_scale(a) * (i * xconv)
        h_all, h = _scan_rows(a, bb, h)
        hg_ref[k * sub:(k + 1) * sub, :] = (h_all * jax.nn.silu(pr_ref[k, :, cb:])).astype(hg_ref.dtype)
    tail_ref[...] = prev8
    hc_ref[...] = h

    @pl.when(ti == pl.num_programs(2) - 1)
    def _():
        nc_ref[...] = prev8[8 - tail:, :]
        nh_ref[...] = h


def _a_front_call(xn, xs, w_in, layer, conv_state, h0, conv_w, conv_b, w_ga, b_ga, w_gx, b_gx, lam, n, t):
    d = xn.shape[1]
    c = w_in.shape[2] // 2
    heads, blk, _ = w_ga.shape
    cb = max(512, blk)
    tt, sub = 512, 128
    assert c % cb == 0 and cb % blk == 0 and blk % LANES == 0 and t % tt == 0
    ncb = c // cb
    ns = xs.shape[0]
    row = lambda a: a.reshape(1, c)
    vec = pl.BlockSpec((1, cb), lambda j, b, i: (0, j))
    wspec = pl.BlockSpec((cb // blk, blk, blk), lambda j, b, i: (j, 0, 0))
    hg, nc, nh, ps = pl.pallas_call(
        functools.partial(_a_front_kernel, tt=tt, sub=sub, blk=blk),
        grid=(ncb, n, t // tt),
        in_specs=[
            pl.BlockSpec((None, tt, d), lambda j, b, i: (b, i, 0)),
            pl.BlockSpec((None, d, cb), lambda j, b, i: (layer, 0, j)),
            pl.BlockSpec((None, d, cb), lambda j, b, i: (layer, 0, ncb + j)),
            pl.BlockSpec((ns, d), lambda j, b, i: (0, 0)),
            pl.BlockSpec((None, CONV_W - 1, cb), lambda j, b, i: (b, 0, j)),
            pl.BlockSpec((None, 1, cb), lambda j, b, i: (b, 0, j)),
            pl.BlockSpec((CONV_W, cb), lambda j, b, i: (0, j)),
            vec, wspec, vec, wspec, vec, vec,
        ],
        out_specs=[
            pl.BlockSpec((None, tt, cb), lambda j, b, i: (b, i, j)),
            pl.BlockSpec((None, CONV_W - 1, cb), lambda j, b, i: (b, 0, j)),
            pl.BlockSpec((None, 1, cb), lambda j, b, i: (b, 0, j)),
            pl.BlockSpec((2, ns, cb), lambda j, b, i: (0, 0, j)),
        ],
        out_shape=[
            jax.ShapeDtypeStruct((n, t, c), BF16),
            jax.ShapeDtypeStruct((n, CONV_W - 1, c), F32),
            jax.ShapeDtypeStruct((n, 1, c), F32),
            jax.ShapeDtypeStruct((2, ns, c), F32),
        ],
        scratch_shapes=[
            pltpu.VMEM((d, 2 * cb), BF16),
            pltpu.VMEM((8, cb), F32),
            pltpu.VMEM((1, cb), F32),
            pltpu.VMEM((tt // sub, sub, 2 * cb), F32),
        ],
        compiler_params=_params("parallel", "arbitrary", "arbitrary"),
        name="a_in_rglru",
    )(xn.reshape(n, t, d), w_in, w_in, xs, conv_state, h0.reshape(n, 1, c), conv_w, row(conv_b),
      w_ga.astype(BF16), row(b_ga), w_gx.astype(BF16), row(b_gx), row(lam))
    return hg.reshape(n * t, c), nc, nh.reshape(n, c), ps


def _rel_bucket(dist):
    dist = np.asarray(dist)
    d = np.maximum(dist, 1).astype(np.float32)
    large = MAX_EXACT + (np.log(d / MAX_EXACT) / np.log(MAX_DIST / MAX_EXACT)
                         * (N_BUCKETS - MAX_EXACT)).astype(np.int32)
    large = np.minimum(large, N_BUCKETS - 1)
    return np.where(dist < MAX_EXACT, dist, large).astype(np.int32)


def _prompt_bias(rel_bias):
    span = 3 * BAND - 1
    neg = jnp.full((N_SLOTS, BAND - 1), -jnp.inf, F32)
    tables = []
    for g, r in enumerate(DILATIONS):
        tg = rel_bias[:, g * N_SLOTS:(g + 1) * N_SLOTS].astype(F32)
        tb = tg[_rel_bucket(np.arange(BAND + 1) * r)].T
        wr = jnp.concatenate([neg, tb[:, ::-1], neg, neg[:, :1]], axis=1)
        skew = jnp.tile(wr, (1, BAND))[:, :BAND * span].reshape(N_SLOTS, BAND, span)
        tables.append(skew[:, :, BAND - 1:])
    return jnp.stack(tables)


def _sample_bias(rel_bias):
    tables = []
    for g, r in enumerate(DILATIONS):
        steps = np.concatenate([BAND - np.arange(BAND), np.zeros(BAND, np.int64)])
        tg = rel_bias[:, g * N_SLOTS:(g + 1) * N_SLOTS].astype(F32)
        tables.append(tg[_rel_bucket(steps * r)].T)
    return jnp.stack(tables)


def _attn_prompt_kernel(q_ref, k_ref, v_ref, bias_ref, gate_ref, o_ref, acc_ref, m_ref, l_ref, *, t):
    g = pl.program_id(2)

    def unit(start, gi, r, has_prev):
        nk = 2 * BAND if has_prev else BAND
        kstart = start - BAND * r if has_prev else start
        rows = lambda s0, cnt: pl.ds(s0, cnt, stride=r) if r > 1 else pl.ds(s0, cnt)
        k = k_ref[rows(kstart, nk), :].astype(BF16)
        v = v_ref[rows(kstart, nk), :].astype(BF16)
        q = jnp.concatenate([q_ref[s, rows(start, BAND), :] for s in range(GQA)], axis=0).astype(BF16)
        sc = lax.dot_general(q, k, (((1,), (1,)), ((), ())), preferred_element_type=F32) * SM_SCALE
        sc = sc + bias_ref[:, :, 2 * BAND - nk:].reshape(GQA * BAND, nk)
        m = jnp.max(sc, axis=-1, keepdims=True)
        p = jnp.exp(sc - m)
        l = jnp.sum(p, axis=-1, keepdims=True)
        pv = jnp.dot(p.astype(BF16), v, preferred_element_type=F32)
        lane = lax.broadcasted_iota(jnp.int32, (BAND, LANES), 1)
        m_tile = jnp.zeros((BAND, LANES), F32)
        l_tile = jnp.ones((BAND, LANES), F32)
        for s in range(GQA):
            blk = slice(s * BAND, (s + 1) * BAND)
            acc_ref[gi, s, rows(start, BAND), :] = pv[blk]
            m_tile = jnp.where(lane == s, m[blk], m_tile)
            l_tile = jnp.where(lane == s, l[blk], l_tile)
        m_ref[gi, rows(start, BAND), :] = m_tile
        l_ref[gi, rows(start, BAND), :] = l_tile

    def group_pass(gi, r):
        nb = t // (BAND * r)

        def head_block(cls, carry):
            unit(cls, gi, r, False)
            return carry

        lax.fori_loop(0, r, head_block, 0, unroll=min(r, ATTN_UNROLL))

        def later_block(u, carry):
            cls = u % r
            blk = 1 + u // r
            unit(blk * (BAND * r) + cls, gi, r, True)
            return carry

        if nb > 1:
            lax.fori_loop(0, r * (nb - 1), later_block, 0, unroll=ATTN_UNROLL)

    for gi, r in enumerate(DILATIONS):
        pl.when(g == gi)(functools.partial(group_pass, gi, r))

    @pl.when(g == N_GROUPS - 1)
    def _():
        def combine(c, carry):
            rows = pl.ds(pl.multiple_of(c * BAND, BAND), BAND)
            ms = [m_ref[gi, rows, :] for gi in range(N_GROUPS)]
            mx = functools.reduce(jnp.maximum, ms)
            es = [jnp.exp(mg - mx) for mg in ms]
            den = sum(es[gi] * l_ref[gi, rows, :] for gi in range(N_GROUPS))
            ws = [e / den for e in es]
            for s in range(GQA):
                o = sum(ws[gi][:, s:s + 1] * acc_ref[gi, s, rows, :] for gi in range(N_GROUPS))
                cols = slice(s * HEAD_DIM, (s + 1) * HEAD_DIM)
                o_ref[rows, cols] = (o * jax.nn.silu(gate_ref[rows, cols])).astype(o_ref.dtype)
            return carry

        lax.fori_loop(0, t // BAND, combine, 0, unroll=2)


def _attn_prompt_call(q_heads, kv_heads, bias, gate, n, t):
    assert all(t % (BAND * r) == 0 for r in DILATIONS)
    q5 = q_heads.reshape(N_GROUPS, KV_HEADS, GQA, n, t, HEAD_DIM)
    kv6 = kv_heads.reshape(N_GROUPS, 2, KV_HEADS, n, t, HEAD_DIM)
    gate3 = gate.reshape(n, t, ATTN_W)
    kv_spec = lambda which: pl.BlockSpec((None, None, None, None, t, HEAD_DIM),
                                         lambda b, h, g: (g, which, h, b, 0, 0))
    out = pl.pallas_call(
        functools.partial(_attn_prompt_kernel, t=t),
        grid=(n, KV_HEADS, N_GROUPS),
        in_specs=[
            pl.BlockSpec((None, None, GQA, None, t, HEAD_DIM), lambda b, h, g: (g, h, 0, b, 0, 0)),
            kv_spec(0), kv_spec(1),
            pl.BlockSpec((None, GQA, BAND, 2 * BAND), lambda b, h, g: (g, h, 0, 0)),
            pl.BlockSpec((None, t, GQA * HEAD_DIM), lambda b, h, g: (b, 0, h)),
        ],
        out_specs=pl.BlockSpec((None, t, GQA * HEAD_DIM), lambda b, h, g: (b, 0, h)),
        out_shape=jax.ShapeDtypeStruct((n, t, ATTN_W), BF16),
        scratch_shapes=[pltpu.VMEM((N_GROUPS, GQA, t, HEAD_DIM), F32),
                        pltpu.VMEM((N_GROUPS, t, LANES), F32),
                        pltpu.VMEM((N_GROUPS, t, LANES), F32)],
        compiler_params=_params("parallel", "parallel", "arbitrary"),
        name="attn_prompt",
    )(q5, kv6, kv6, bias, gate3)
    return out.reshape(n * t, ATTN_W)


def _attn_sample_kernel(q_ref, kvn_ref, s0_ref, s1_ref, s2_ref, bias_ref, gate_ref, o_ref):
    st_refs = (s0_ref, s1_ref, s2_ref)
    bf = lambda x: x.astype(BF16)
    rnd = lambda x: x.astype(BF16).astype(F32)
    for h in range(KV_HEADS):
        sl = slice(h * GQA, (h + 1) * GQA)
        s_old, s_new, v_old, v_new = [], [], [], []
        for g in range(N_GROUPS):
            q = q_ref[g * N_SLOTS + h * GQA:g * N_SLOTS + (h + 1) * GQA, :]
            k_old = st_refs[g][:, 0, h, :]
            v_old.append(st_refs[g][:, 1, h, :])
            row = g * 2 * KV_HEADS + h
            k_new = kvn_ref[row:row + 1, :]
            v_new.append(kvn_ref[row + KV_HEADS:row + KV_HEADS + 1, :])
            so = lax.dot_general(bf(q), bf(k_old), (((1,), (1,)), ((), ())), preferred_element_type=F32)
            s_old.append(so * SM_SCALE + bias_ref[g, sl, :BAND])
            sn = jnp.sum(rnd(q) * rnd(k_new), axis=-1, keepdims=True)
            s_new.append(sn * SM_SCALE + bias_ref[g, sl, BAND:BAND + 1])
        m = functools.reduce(jnp.maximum, [jnp.max(x, axis=-1, keepdims=True) for x in s_old] + s_new)
        num = jnp.zeros((GQA, HEAD_DIM), F32)
        den = jnp.zeros((GQA, 1), F32)
        for g in range(N_GROUPS):
            p_old = jnp.exp(s_old[g] - m)
            p_new = jnp.exp(s_new[g] - m)
            den = den + jnp.sum(p_old, axis=-1, keepdims=True) + p_new
            num = num + jnp.dot(bf(p_old), bf(v_old[g]), preferred_element_type=F32) + rnd(p_new) * rnd(v_new[g])
        o_ref[sl, :] = ((num / den) * jax.nn.silu(gate_ref[sl, :])).astype(o_ref.dtype)


def _attn_sample_call(q, kv_new, states, bias, gate):
    n = q.shape[0]
    st_views, st_specs = [], []
    for g, r in enumerate(DILATIONS):
        assert states[g].shape[1] == BAND * r
        st_views.append(states[g].reshape(n, BAND, r, 2, KV_HEADS, HEAD_DIM))
        st_specs.append(pl.BlockSpec((None, BAND, None, 2, KV_HEADS, HEAD_DIM), lambda b: (b, 0, 0, 0, 0, 0)))
    per_b = lambda rows: pl.BlockSpec((None, rows, HEAD_DIM), lambda b: (b, 0, 0))
    out = pl.pallas_call(
        _attn_sample_kernel,
        grid=(n,),
        in_specs=[per_b(N_GROUPS * N_SLOTS), per_b(N_GROUPS * 2 * KV_HEADS)] + st_specs + [
            pl.BlockSpec((N_GROUPS, N_SLOTS, 2 * BAND), lambda b: (0, 0, 0)), per_b(N_SLOTS)],
        out_specs=per_b(N_SLOTS),
        out_shape=jax.ShapeDtypeStruct((n, N_SLOTS, HEAD_DIM), BF16),
        compiler_params=_params("parallel"),
        name="attn_sample",
    )(q.reshape(n, N_GROUPS * N_SLOTS, HEAD_DIM), kv_new.reshape(n, N_GROUPS * 2 * KV_HEADS, HEAD_DIM),
      *st_views, bias, gate.reshape(n, N_SLOTS, HEAD_DIM))
    return out.reshape(n, ATTN_W)


def _trunks(xp, xs, conv_p, h_p, conv_s, h_s, kv_bufs, w):
    n, t, d = xp.shape
    ns = xs.shape[0]
    assert xs.shape[1] == 1
    n_a = w["a_w_in"].shape[0]
    n_b = w["b_w_in"].shape[0]
    xp2 = xp.reshape(n * t, d)
    xs2 = xs.reshape(ns, d)
    (xnp,) = _norm_call(xp2, [w["a_pre_g"][0]])
    (xns,) = _norm_call(xs2, [w["a_pre_g"][0]])
    a_w_out = w["a_w_out"].astype(BF16)
    b_w_out = w["b_w_out"].astype(BF16)
    conv_out_p, h_out_p, conv_out_s, h_out_s = [], [], [], []
    for l in range(n_a):
        mixer = (w["a_conv_w"][l], w["a_conv_b"][l], w["a_w_gate_a"][l], w["a_b_gate_a"][l],
                 w["a_w_gate_x"][l], w["a_b_gate_x"][l], w["a_lambda"][l])
        hgp, nc, nh, proj_s = _a_front_call(xnp, xns, w["a_w_in"], l, conv_p[l], h_p[l], *mixer, n, t)
        conv_out_p.append(nc)
        h_out_p.append(nh)
        hgs, nc, nh = _rglru_step_call(proj_s, conv_s[l], h_s[l], *mixer)
        conv_out_s.append(nc)
        h_out_s.append(nh)
        gains = [w["a_pre_g"][l + 1]] if l + 1 < n_a else [w["kv_norm_g"], w["b_pre_g"][0]]
        xp2, xnps, ys = _out_res_call(hgp, hgs, a_w_out, l, xp2, w["a_post_g"][l], gains,
                                      name="a_out_proj")
        xs2, xnss = _resnorm_call(ys, xs2, w["a_post_g"][l], gains)
        xnp, xns = xnps[0], xnss[0]

    kvp, kvp_heads, kvs = _mm_call(xnp, xns, w["w_kv"], heads=True, name="kv_proj")
    kv_out_p = []
    for g in range(N_GROUPS):
        keep = min(WINDOWS[g], t)
        kept = kvp.reshape(n, t, N_GROUPS * KV_GROUP_W)[:, t - keep:, g * KV_GROUP_W:(g + 1) * KV_GROUP_W]
        kv_out_p.append(kept.reshape(n, keep, 2, KV_HEADS, HEAD_DIM))
    kvs4 = kvs.reshape(ns, 1, N_GROUPS, 2, KV_HEADS, HEAD_DIM)
    kv_out_s = [jnp.concatenate([kv_bufs[g][:, 1:], kvs4[:, :, g]], axis=1) for g in range(N_GROUPS)]
    bias_p = _prompt_bias(w["rel_bias"])
    bias_s = _sample_bias(w["rel_bias"])

    xnp, xns = xnps[1], xnss[1]
    for l in range(n_b):
        gate_p, gate_s = _mm_call(xnp, xns, w["b_w_in"], layer=l, cols=(Q_W, ATTN_W), name="gate_proj")
        q_heads, q_s = _mm_call(xnp, xns, w["b_w_in"], layer=l, cols=(0, Q_W), natural=False, heads=True,
                                name="q_proj")
        ogp = _attn_prompt_call(q_heads, kvp_heads, bias_p, gate_p, n, t)
        ogs = _attn_sample_call(q_s, kvs, kv_bufs, bias_s, gate_s)
        gains = [w["b_pre_g"][l + 1]] if l + 1 < n_b else []
        xp2, xnps, ys = _out_res_call(ogp, ogs, b_w_out, l, xp2, w["b_post_g"][l], gains,
                                      name="b_out_proj")
        xs2, xnss = _resnorm_call(ys, xs2, w["b_post_g"][l], gains)
        if gains:
            xnp, xns = xnps[0], xnss[0]
    prompt_out = (xp2.reshape(n, t, d), jnp.stack(conv_out_p), jnp.stack(h_out_p), *kv_out_p)
    sample_out = (xs2.reshape(ns, 1, d), jnp.stack(conv_out_s), jnp.stack(h_out_s), *kv_out_s)
    return prompt_out, sample_out


def kernel(x_prompt, x_sample, state_conv, state_h, state_kv_w128, state_kv_w512, state_kv_w2048,
           a_pre_g, a_w_in, a_conv_w, a_conv_b, a_w_gate_a, a_b_gate_a, a_w_gate_x, a_b_gate_x,
           a_lambda, a_w_out, a_post_g, kv_norm_g, w_kv, rel_bias, b_pre_g, b_w_in, b_w_out, b_post_g):
    w = dict(
        a_pre_g=a_pre_g, a_w_in=a_w_in, a_conv_w=a_conv_w, a_conv_b=a_conv_b,
        a_w_gate_a=a_w_gate_a, a_b_gate_a=a_b_gate_a, a_w_gate_x=a_w_gate_x, a_b_gate_x=a_b_gate_x,
        a_lambda=a_lambda, a_w_out=a_w_out, a_post_g=a_post_g, kv_norm_g=kv_norm_g,
        w_kv=w_kv, rel_bias=rel_bias, b_pre_g=b_pre_g, b_w_in=b_w_in, b_w_out=b_w_out, b_post_g=b_post_g)
    nb_p = x_prompt.shape[0]
    n_a = a_w_in.shape[0]
    d_rnn = a_conv_w.shape[-1]
    conv0 = jnp.zeros((n_a, nb_p, CONV_W - 1, d_rnn), F32)
    h0 = jnp.zeros((n_a, nb_p, d_rnn), F32)
    prompt_out, sample_out = _trunks(x_prompt, x_sample, conv0, h0, state_conv, state_h,
                                     (state_kv_w128, state_kv_w512, state_kv_w2048), w)
    return (prompt_out[0], sample_out[0], *prompt_out[1:], *sample_out[1:])
```

```python
import functools
import math

import numpy as np
import jax
import jax.numpy as jnp
from jax import lax
from jax.experimental import pallas as pl
from jax.experimental.pallas import tpu as pltpu

F32 = jnp.float32
BF16 = jnp.bfloat16

EPS = 1e-6
LRU_C = 8.0
CONV_W = 4
HEAD_DIM = 128
N_SLOTS = 16
KV_HEADS = 4
GQA = N_SLOTS // KV_HEADS
WINDOWS = (128, 512, 2048)
DILATIONS = (1, 4, 16)
N_GROUPS = 3
BAND = 128
ATTN_W = N_SLOTS * HEAD_DIM
Q_W = N_GROUPS * ATTN_W
KV_GROUP_W = 2 * KV_HEADS * HEAD_DIM
N_BUCKETS = 32
MAX_EXACT = 16
MAX_DIST = 2048
SM_SCALE = HEAD_DIM ** -0.5

ATTN_UNROLL = 8
LANES = 128
VMEM_LIMIT = 56 * 1024 * 1024


def _params(*sem):
    return pltpu.CompilerParams(dimension_semantics=sem, vmem_limit_bytes=VMEM_LIMIT)


def _tile(dim, pref):
    if dim <= pref:
        return dim
    t = pref
    while dim % t:
        t //= 2
    return t


def _round_slab(rows, steps):
    slab = rows // steps
    assert slab * steps == rows and slab % 16 == 0
    return slab


def _input_scale(a):
    y = 1.0 - a * a
    return jnp.where(y > 0.0, y * lax.rsqrt(y), 0.0)


def _rms(x, g):
    return x * lax.rsqrt(jnp.mean(x * x, axis=-1, keepdims=True) + EPS) * g


def _norm_kernel(x_ref, g_ref, *o_refs):
    x = x_ref[...]
    xhat = x * lax.rsqrt(jnp.mean(x * x, axis=-1, keepdims=True) + EPS)
    for k, o_ref in enumerate(o_refs):
        o_ref[...] = (xhat * g_ref[k:k + 1, :]).astype(o_ref.dtype)


def _norm_call(x, gains):
    m, d = x.shape
    tm = _tile(m, 256)
    g = jnp.stack(gains)
    ng = len(gains)
    row = pl.BlockSpec((tm, d), lambda i: (i, 0))
    return pl.pallas_call(
        _norm_kernel,
        grid=(m // tm,),
        in_specs=[row, pl.BlockSpec((ng, d), lambda i: (0, 0))],
        out_specs=[row] * ng,
        out_shape=[jax.ShapeDtypeStruct((m, d), BF16)] * ng,
        compiler_params=_params("parallel"),
        name="rmsnorm",
    )(x, g)


def _resnorm_kernel(y_ref, x_ref, gp_ref, g_ref, xo_ref, *o_refs):
    xn = x_ref[...] + _rms(y_ref[...], gp_ref[...])
    xo_ref[...] = xn
    if o_refs:
        xhat = xn * lax.rsqrt(jnp.mean(xn * xn, axis=-1, keepdims=True) + EPS)
        for k, o_ref in enumerate(o_refs):
            o_ref[...] = (xhat * g_ref[k:k + 1, :]).astype(o_ref.dtype)


def _resnorm_call(y, x, g_post, gains):
    m, d = x.shape
    tm = _tile(m, 256)
    ng = len(gains)
    g = jnp.stack(gains) if gains else jnp.ones((1, d), F32)
    row = pl.BlockSpec((tm, d), lambda i: (i, 0))
    outs = pl.pallas_call(
        _resnorm_kernel,
        grid=(m // tm,),
        in_specs=[row, row, pl.BlockSpec((1, d), lambda i: (0, 0)),
                  pl.BlockSpec((g.shape[0], d), lambda i: (0, 0))],
        out_specs=[row] * (1 + ng),
        out_shape=[jax.ShapeDtypeStruct((m, d), F32)] + [jax.ShapeDtypeStruct((m, d), BF16)] * ng,
        compiler_params=_params("parallel"),
        name="residual_rmsnorm",
    )(y, x, g_post.reshape(1, d), g)
    return outs[0], list(outs[1:])


def _mm_kernel(a_ref, w_ref, a2_ref, *refs, natural, heads, rounds):
    if rounds:
        src_ref, *refs = refs
    *o_refs, wb_ref = refs
    if rounds:
        *o_refs, dst_ref = o_refs
        dst_ref[...] = src_ref[...].astype(BF16)
    *o_refs, o2_ref = o_refs

    @pl.when(pl.program_id(1) == 0)
    def _():
        wb_ref[...] = w_ref[...].astype(BF16)
        o2_ref[...] = jnp.dot(a2_ref[...], wb_ref[...], preferred_element_type=F32)

    y = jnp.dot(a_ref[...], wb_ref[...], preferred_element_type=F32)
    k = 0
    if natural:
        o_refs[k][...] = y
        k += 1
    if heads:
        for h in range(y.shape[1] // LANES):
            o_refs[k][h] = y[:, h * LANES:(h + 1) * LANES]


def _mm_call(a, a2, w, *, layer=0, cols=None, natural=True, heads=False, rounds=None, tm=1024, tn=512,
             name="matmul"):
    if w.ndim == 2:
        w = w[None]
    m, kdim = a.shape
    m2 = a2.shape[0]
    col0, n = cols if cols is not None else (0, w.shape[2])
    tm = _tile(m, tm)
    tn = _tile(n, tn)
    assert col0 % tn == 0
    out_specs, out_shape = [], []
    if natural:
        out_specs.append(pl.BlockSpec((tm, tn), lambda j, i: (i, j)))
        out_shape.append(jax.ShapeDtypeStruct((m, n), F32))
    if heads:
        out_specs.append(pl.BlockSpec((tn // LANES, tm, LANES), lambda j, i: (j, i, 0)))
        out_shape.append(jax.ShapeDtypeStruct((n // LANES, m, LANES), F32))
    out_specs.append(pl.BlockSpec((m2, tn), lambda j, i: (0, j)))
    out_shape.append(jax.ShapeDtypeStruct((m2, n), F32))
    n_i = m // tm
    in_specs = [pl.BlockSpec((tm, kdim), lambda j, i: (i, 0)),
                pl.BlockSpec((None, kdim, tn), lambda j, i: (layer, 0, col0 // tn + j)),
                pl.BlockSpec((m2, kdim), lambda j, i: (0, 0))]
    operands = [a, w, a2]
    if rounds is not None:
        w3, r_layer = rounds
        slab = _round_slab(w3.shape[1], (n // tn) * n_i)
        in_specs.append(pl.BlockSpec((None, slab, w3.shape[2]), lambda j, i: (r_layer, j * n_i + i, 0)))
        out_specs.append(pl.BlockSpec((slab, w3.shape[2]), lambda j, i: (j * n_i + i, 0)))
        out_shape.append(jax.ShapeDtypeStruct(w3.shape[1:], BF16))
        operands.append(w3)
    return pl.pallas_call(
        functools.partial(_mm_kernel, natural=natural, heads=heads, rounds=rounds is not None),
        grid=(n // tn, n_i),
        in_specs=in_specs,
        out_specs=out_specs,
        out_shape=out_shape,
        scratch_shapes=[pltpu.VMEM((kdim, tn), BF16)],
        compiler_params=_params("parallel", "arbitrary"),
        name=name,
    )(*operands)


def _lane_fold_sq(v):
    return sum(v[:, s:s + LANES] * v[:, s:s + LANES] for s in range(0, v.shape[1], LANES))


def _times_rows(v, scale):
    return jnp.concatenate([v[:, s:s + LANES] * scale for s in range(0, v.shape[1], LANES)], axis=1)


def _out_res_kernel(a_ref, w_ref, a2_ref, x_ref, gp_ref, g_ref, xo_ref, *refs, n_row, n_gain, width):
    xn_refs = refs[:n_gain]
    o2_ref, y_scr, xnew_scr, ssq_y, ssq_x, sc_y, sc_x = refs[n_gain:]
    i = pl.program_id(0)
    j = pl.program_id(1)
    tn = w_ref.shape[1]
    cols = pl.ds(pl.multiple_of(j * tn, tn), tn)

    @pl.when((i == 0) & (j == 0))
    def _():
        ssq_x[...] = jnp.zeros(ssq_x.shape, F32)
        ssq_y[...] = jnp.zeros(ssq_y.shape, F32)

    @pl.when(j == 0)
    def _():
        for ssq, sc in ((ssq_x, sc_x), (ssq_y, sc_y)):
            mean = jnp.sum(ssq[...], axis=-1, keepdims=True) * (1.0 / width)
            sc[...] = jnp.broadcast_to(lax.rsqrt(mean + EPS), sc.shape)
            ssq[...] = jnp.zeros(ssq.shape, F32)

    if n_gain:
        @pl.when(i >= 2)
        def _():
            xhat = _times_rows(xnew_scr[:, cols], sc_x[...])
            for k in range(n_gain):
                xn_refs[k][...] = (xhat * g_ref[k:k + 1, :]).astype(xn_refs[k].dtype)

    @pl.when((i >= 1) & (i <= n_row))
    def _():
        xnew = x_ref[...] + _times_rows(y_scr[:, cols], sc_y[...]) * gp_ref[...]
        xo_ref[...] = xnew
        xnew_scr[:, cols] = xnew
        ssq_x[...] += _lane_fold_sq(xnew)

    @pl.when(i < n_row)
    def _():
        y = jnp.dot(a_ref[...], w_ref[...], preferred_element_type=F32)
        y_scr[:, cols] = y
        ssq_y[...] += _lane_fold_sq(y)

    @pl.when(i == 0)
    def _():
        o2_ref[...] = jnp.dot(a2_ref[...], w_ref[...], preferred_element_type=F32)


def _out_res_vmem_bytes(tm, tn, kdim, n, ng):
    return 2 * tm * n * 4 + 2 * (tm * kdim * 2 + kdim * tn * 2 + 2 * tm * tn * 4 + ng * tm * tn * 2)


def _out_res_call(a, a2, w, x, g_post, gains, *, tm=512, name="out_proj_residual"):
    m, kdim = a.shape
    n = w.shape[1]
    m2 = a2.shape[0]
    ng = len(gains)
    tn = next(c for c in (1024, 512, 256, 128) if n % c == 0
              and _out_res_vmem_bytes(tm, c, kdim, n, ng) <= VMEM_LIMIT * 4 // 5)
    assert m % tm == 0
    n_row, nj = m // tm, n // tn
    g = jnp.stack(gains) if gains else jnp.ones((1, n), F32)
    last = nj - 1
    lag1 = lambda i, j: (jnp.clip(i - 1, 0, n_row - 1), jnp.where(i < 1, 0, jnp.where(i <= n_row, j, last)))
    lag2 = lambda i, j: (jnp.clip(i - 2, 0, n_row - 1), jnp.where(i < 2, 0, j))
    outs = pl.pallas_call(
        functools.partial(_out_res_kernel, n_row=n_row, n_gain=ng, width=n),
        grid=(n_row + 2, nj),
        in_specs=[
            pl.BlockSpec((tm, kdim), lambda i, j: (jnp.minimum(i, n_row - 1), 0)),
            pl.BlockSpec((kdim, tn), lambda i, j: (0, jnp.where(i < n_row, j, last))),
            pl.BlockSpec((m2, kdim), lambda i, j: (0, 0)),
            pl.BlockSpec((tm, tn), lag1),
            pl.BlockSpec((1, tn), lambda i, j: (0, j)),
            pl.BlockSpec((g.shape[0], tn), lambda i, j: (0, j)),
        ],
        out_specs=[pl.BlockSpec((tm, tn), lag1)] + [pl.BlockSpec((tm, tn), lag2)] * ng + [
            pl.BlockSpec((m2, tn), lambda i, j: (0, jnp.where(i == 0, j, last)))],
        out_shape=[jax.ShapeDtypeStruct((m, n), F32)] + [jax.ShapeDtypeStruct((m, n), BF16)] * ng + [
            jax.ShapeDtypeStruct((m2, n), F32)],
        scratch_shapes=[pltpu.VMEM((tm, n), F32), pltpu.VMEM((tm, n), F32)] + [pltpu.VMEM((tm, LANES), F32)] * 4,
        compiler_params=_params("arbitrary", "arbitrary"),
        name=name,
    )(a, w, a2, x, g_post.reshape(1, n), g)
    return outs[0], list(outs[1:1 + ng]), outs[-1]


def _rglru_kernel(xb_ref, gate_ref, cs_ref, h0_ref, cw_ref, cb_ref, wga_ref, bga_ref, wgx_ref, bgx_ref,
                  lam_ref, hg_ref, nc_ref, nh_ref, *, blk):
    tail = CONV_W - 1
    xb = xb_ref[...]
    xconv = cb_ref[...]
    for k in range(tail):
        xconv = xconv + cs_ref[:, k, :] * cw_ref[k:k + 1, :]
        if k:
            nc_ref[:, k - 1, :] = cs_ref[:, k, :]
    xconv = xconv + xb * cw_ref[tail:tail + 1, :]
    nc_ref[:, tail - 1, :] = xb

    r_parts, i_parts = [], []
    for hh in range(xconv.shape[1] // blk):
        xh = xconv[:, hh * blk:(hh + 1) * blk].astype(BF16)
        r_parts.append(jnp.dot(xh, wga_ref[hh], preferred_element_type=F32))
        i_parts.append(jnp.dot(xh, wgx_ref[hh], preferred_element_type=F32))
    r = jax.nn.sigmoid(jnp.concatenate(r_parts, axis=1) + bga_ref[...])
    i = jax.nn.sigmoid(jnp.concatenate(i_parts, axis=1) + bgx_ref[...])

    neg_lam = -lam_ref[...]
    softplus = jnp.maximum(neg_lam, 0.0) + jnp.log1p(jnp.exp(-jnp.abs(neg_lam)))
    a = jnp.exp((-LRU_C * softplus) * r)
    h = a * h0_ref[...] + _input_scale(a) * (i * xconv)
    nh_ref[...] = h
    hg_ref[...] = (h * jax.nn.silu(gate_ref[...])).astype(hg_ref.dtype)


def _rglru_step_call(proj, conv_state, h0, conv_w, conv_b, w_ga, b_ga, w_gx, b_gx, lam):
    _, n, c = proj.shape
    heads, blk, _ = w_ga.shape
    cb = _tile(c, max(1024, blk))
    assert cb % blk == 0 and blk % LANES == 0
    ncb = c // cb
    row = lambda a: a.reshape(1, c)
    vec = pl.BlockSpec((1, cb), lambda j: (0, j))
    rows = pl.BlockSpec((n, cb), lambda j: (0, j))
    conv_rows = pl.BlockSpec((n, CONV_W - 1, cb), lambda j: (0, 0, j))
    wspec = pl.BlockSpec((cb // blk, blk, blk), lambda j: (j, 0, 0))
    return pl.pallas_call(
        functools.partial(_rglru_kernel, blk=blk),
        grid=(ncb,),
        in_specs=[pl.BlockSpec((None, n, cb), lambda j: (0, 0, j)), pl.BlockSpec((None, n, cb), lambda j: (1, 0, j)),
                  conv_rows, rows,
                  pl.BlockSpec((CONV_W, cb), lambda j: (0, j)), vec, wspec, vec, wspec, vec, vec],
        out_specs=[rows, conv_rows, rows],
        out_shape=[
            jax.ShapeDtypeStruct((n, c), BF16),
            jax.ShapeDtypeStruct((n, CONV_W - 1, c), F32),
            jax.ShapeDtypeStruct((n, c), F32),
        ],
        compiler_params=_params("parallel"),
        name="rglru_step",
    )(proj, proj, conv_state, h0, conv_w, row(conv_b), w_ga.astype(BF16), row(b_ga),
      w_gx.astype(BF16), row(b_gx), row(lam))


def _scan_rows(a, b, h):
    t, c = a.shape
    a3 = a.reshape(t // 8, 8, c)
    b3 = b.reshape(t // 8, 8, c)
    row = lax.broadcasted_iota(jnp.int32, a3.shape, 1)
    for d in (1, 2, 4):
        keep = row >= d
        a_prev = jnp.where(keep, pltpu.roll(a3, d, axis=1), 1.0)
        b_prev = jnp.where(keep, pltpu.roll(b3, d, axis=1), 0.0)
        b3 = a3 * b_prev + b3
        a3 = a3 * a_prev
    out = []
    for g in range(t // 8):
        hg = a3[g] * h + b3[g]
        out.append(hg)
        h = hg[7:8, :]
    return jnp.concatenate(out, axis=0), h


def _a_front_kernel(xn_ref, wx_ref, wg_ref, xs_ref, cs_ref, h0_ref, cw_ref, cb_ref, wga_ref, bga_ref, wgx_ref,
                    bgx_ref, lam_ref, wo_ref, hg_ref, nc_ref, nh_ref, ps_ref, wob_ref, wb_ref, tail_ref, hc_ref,
                    pr_ref, *, tt, sub, blk):
    b = pl.program_id(1)
    ti = pl.program_id(2)
    tail = CONV_W - 1
    n_sub = tt // sub
    cb = cs_ref.shape[1]

    wob_ref[...] = wo_ref[...].astype(BF16)

    @pl.when((b == 0) & (ti == 0))
    def _():
        wb_ref[:, :cb] = wx_ref[...].astype(BF16)
        wb_ref[:, cb:] = wg_ref[...].astype(BF16)
        ps = jnp.dot(xs_ref[...], wb_ref[...], preferred_element_type=F32)
        ps_ref[0] = ps[:, :cb]
        ps_ref[1] = ps[:, cb:]

    @pl.when(ti == 0)
    def _():
        tail_ref[...] = jnp.concatenate([jnp.zeros((8 - tail, cb), F32), cs_ref[...]], axis=0)
        hc_ref[...] = h0_ref[...]

    neg_lam = -lam_ref[...]
    softplus = jnp.maximum(neg_lam, 0.0) + jnp.log1p(jnp.exp(-jnp.abs(neg_lam)))
    log_a_scale = -LRU_C * softplus
    prev8 = tail_ref[...]
    h = hc_ref[...]

    def project(k):
        pr_ref[k] = jnp.dot(xn_ref[k * sub:(k + 1) * sub, :], wb_ref[...], preferred_element_type=F32)

    project(0)
    for k in range(n_sub):
        if k + 1 < n_sub:
            project(k + 1)
        xb = pr_ref[k, :, :cb]
        ext = jnp.concatenate([prev8, xb], axis=0)
        prev8 = xb[sub - 8:, :]
        ext1 = pltpu.roll(ext, 1, axis=0)
        pair = ext * cw_ref[1:2, :] + ext1 * cw_ref[0:1, :]
        xconv = (cb_ref[...] + pltpu.roll(pair, 2, axis=0)[8:, :]) + (ext1[8:, :] * cw_ref[2:3, :] + xb * cw_ref[3:4, :])
        r_parts, i_parts = [], []
        for hh in range(cb // blk):
            xh = xconv[:, hh * blk:(hh + 1) * blk].astype(BF16)
            r_parts.append(jnp.dot(xh, wga_ref[hh], preferred_element_type=F32))
            i_parts.append(jnp.dot(xh, wgx_ref[hh], preferred_element_type=F32))
        r = jax.nn.sigmoid(jnp.concatenate(r_parts, axis=1) + bga_ref[...])
        i = jax.nn.sigmoid(jnp.concatenate(i_parts, axis=1) + bgx_ref[...])
        a = jnp.exp(log_a_scale * r)
        bb = _input_scale(a) * (i * xconv)
        h_all, h = _scan_rows(a, bb, h)
        hg_ref[k * sub:(k + 1) * sub, :] = (h_all * jax.nn.silu(pr_ref[k, :, cb:])).astype(hg_ref.dtype)
    tail_ref[...] = prev8
    hc_ref[...] = h

    @pl.when(ti == pl.num_programs(2) - 1)
    def _():
        nc_ref[...] = prev8[8 - tail:, :]
        nh_ref[...] = h


def _a_front_call(xn, xs, w_in, w_out, layer, conv_state, h0, conv_w, conv_b, w_ga, b_ga, w_gx, b_gx, lam, n, t):
    d = xn.shape[1]
    c = w_in.shape[2] // 2
    heads, blk, _ = w_ga.shape
    cb = max(512, blk)
    tt, sub = 512, 128
    assert c % cb == 0 and cb % blk == 0 and blk % LANES == 0 and t % tt == 0
    ncb = c // cb
    ns = xs.shape[0]
    nt = t // tt
    slab = _round_slab(w_out.shape[1], ncb * n * nt)
    step = lambda j, b, i: (j * n + b) * nt + i
    row = lambda a: a.reshape(1, c)
    vec = pl.BlockSpec((1, cb), lambda j, b, i: (0, j))
    wspec = pl.BlockSpec((cb // blk, blk, blk), lambda j, b, i: (j, 0, 0))
    hg, nc, nh, ps, w_out_b = pl.pallas_call(
        functools.partial(_a_front_kernel, tt=tt, sub=sub, blk=blk),
        grid=(ncb, n, nt),
        in_specs=[
            pl.BlockSpec((None, tt, d), lambda j, b, i: (b, i, 0)),
            pl.BlockSpec((None, d, cb), lambda j, b, i: (layer, 0, j)),
            pl.BlockSpec((None, d, cb), lambda j, b, i: (layer, 0, ncb + j)),
            pl.BlockSpec((ns, d), lambda j, b, i: (0, 0)),
            pl.BlockSpec((None, CONV_W - 1, cb), lambda j, b, i: (b, 0, j)),
            pl.BlockSpec((None, 1, cb), lambda j, b, i: (b, 0, j)),
            pl.BlockSpec((CONV_W, cb), lambda j, b, i: (0, j)),
            vec, wspec, vec, wspec, vec, vec,
            pl.BlockSpec((None, slab, w_out.shape[2]), lambda j, b, i: (layer, step(j, b, i), 0)),
        ],
        out_specs=[
            pl.BlockSpec((None, tt, cb), lambda j, b, i: (b, i, j)),
            pl.BlockSpec((None, CONV_W - 1, cb), lambda j, b, i: (b, 0, j)),
            pl.BlockSpec((None, 1, cb), lambda j, b, i: (b, 0, j)),
            pl.BlockSpec((2, ns, cb), lambda j, b, i: (0, 0, j)),
            pl.BlockSpec((slab, w_out.shape[2]), lambda j, b, i: (step(j, b, i), 0)),
        ],
        out_shape=[
            jax.ShapeDtypeStruct((n, t, c), BF16),
            jax.ShapeDtypeStruct((n, CONV_W - 1, c), F32),
            jax.ShapeDtypeStruct((n, 1, c), F32),
            jax.ShapeDtypeStruct((2, ns, c), F32),
            jax.ShapeDtypeStruct(w_out.shape[1:], BF16),
        ],
        scratch_shapes=[
            pltpu.VMEM((d, 2 * cb), BF16),
            pltpu.VMEM((8, cb), F32),
            pltpu.VMEM((1, cb), F32),
            pltpu.VMEM((tt // sub, sub, 2 * cb), F32),
        ],
        compiler_params=_params("parallel", "arbitrary", "arbitrary"),
        name="a_in_rglru",
    )(xn.reshape(n, t, d), w_in, w_in, xs, conv_state, h0.reshape(n, 1, c), conv_w, row(conv_b),
      w_ga.astype(BF16), row(b_ga), w_gx.astype(BF16), row(b_gx), row(lam), w_out)
    return hg.reshape(n * t, c), nc, nh.reshape(n, c), ps, w_out_b


def _rel_bucket(dist):
    dist = np.asarray(dist)
    d = np.maximum(dist, 1).astype(np.float32)
    large = MAX_EXACT + (np.log(d / MAX_EXACT) / np.log(MAX_DIST / MAX_EXACT)
                         * (N_BUCKETS - MAX_EXACT)).astype(np.int32)
    large = np.minimum(large, N_BUCKETS - 1)
    return np.where(dist < MAX_EXACT, dist, large).astype(np.int32)


def _prompt_bias(rel_bias):
    span = 3 * BAND - 1
    neg = jnp.full((N_SLOTS, BAND - 1), -jnp.inf, F32)
    tables = []
    for g, r in enumerate(DILATIONS):
        tg = rel_bias[:, g * N_SLOTS:(g + 1) * N_SLOTS].astype(F32)
        tb = tg[_rel_bucket(np.arange(BAND + 1) * r)].T
        wr = jnp.concatenate([neg, tb[:, ::-1], neg, neg[:, :1]], axis=1)
        skew = jnp.tile(wr, (1, BAND))[:, :BAND * span].reshape(N_SLOTS, BAND, span)
        tables.append(skew[:, :, BAND - 1:])
    return jnp.stack(tables)


def _sample_bias(rel_bias):
    tables = []
    for g, r in enumerate(DILATIONS):
        steps = np.concatenate([BAND - np.arange(BAND), np.zeros(BAND, np.int64)])
        tg = rel_bias[:, g * N_SLOTS:(g + 1) * N_SLOTS].astype(F32)
        tables.append(tg[_rel_bucket(steps * r)].T)
    return jnp.stack(tables)


def _attn_prompt_kernel(q_ref, k_ref, v_ref, bias_ref, gate_ref, o_ref, acc_ref, m_ref, l_ref, *, t):
    g = pl.program_id(2)

    def unit(start, gi, r, has_prev):
        nk = 2 * BAND if has_prev else BAND
        kstart = start - BAND * r if has_prev else start
        rows = lambda s0, cnt: pl.ds(s0, cnt, stride=r) if r > 1 else pl.ds(s0, cnt)
        k = k_ref[rows(kstart, nk), :].astype(BF16)
        v = v_ref[rows(kstart, nk), :].astype(BF16)
        q = jnp.concatenate([q_ref[s, rows(start, BAND), :] for s in range(GQA)], axis=0).astype(BF16)
        sc = lax.dot_general(q, k, (((1,), (1,)), ((), ())), preferred_element_type=F32) * SM_SCALE
        sc = sc + bias_ref[:, :, 2 * BAND - nk:].reshape(GQA * BAND, nk)
        m = jnp.max(sc, axis=-1, keepdims=True)
        p = jnp.exp(sc - m)
        l = jnp.sum(p, axis=-1, keepdims=True)
        pv = jnp.dot(p.astype(BF16), v, preferred_element_type=F32)
        lane = lax.broadcasted_iota(jnp.int32, (BAND, LANES), 1)
        m_tile = jnp.zeros((BAND, LANES), F32)
        l_tile = jnp.ones((BAND, LANES), F32)
        for s in range(GQA):
            blk = slice(s * BAND, (s + 1) * BAND)
            acc_ref[gi, s, rows(start, BAND), :] = pv[blk]
            m_tile = jnp.where(lane == s, m[blk], m_tile)
            l_tile = jnp.where(lane == s, l[blk], l_tile)
        m_ref[gi, rows(start, BAND), :] = m_tile
        l_ref[gi, rows(start, BAND), :] = l_tile

    def group_pass(gi, r):
        nb = t // (BAND * r)

        def head_block(cls, carry):
            unit(cls, gi, r, False)
            return carry

        lax.fori_loop(0, r, head_block, 0, unroll=min(r, ATTN_UNROLL))

        def later_block(u, carry):
            cls = u % r
            blk = 1 + u // r
            unit(blk * (BAND * r) + cls, gi, r, True)
            return carry

        if nb > 1:
            lax.fori_loop(0, r * (nb - 1), later_block, 0, unroll=ATTN_UNROLL)

    for gi, r in enumerate(DILATIONS):
        pl.when(g == gi)(functools.partial(group_pass, gi, r))

    @pl.when(g == N_GROUPS - 1)
    def _():
        def combine(c, carry):
            rows = pl.ds(pl.multiple_of(c * BAND, BAND), BAND)
            ms = [m_ref[gi, rows, :] for gi in range(N_GROUPS)]
            mx = functools.reduce(jnp.maximum, ms)
            es = [jnp.exp(mg - mx) for mg in ms]
            den = sum(es[gi] * l_ref[gi, rows, :] for gi in range(N_GROUPS))
            ws = [e / den for e in es]
            for s in range(GQA):
                o = sum(ws[gi][:, s:s + 1] * acc_ref[gi, s, rows, :] for gi in range(N_GROUPS))
                cols = slice(s * HEAD_DIM, (s + 1) * HEAD_DIM)
                o_ref[rows, cols] = (o * jax.nn.silu(gate_ref[rows, cols])).astype(o_ref.dtype)
            return carry

        lax.fori_loop(0, t // BAND, combine, 0, unroll=2)


def _attn_prompt_call(q_heads, kv_heads, bias, gate, n, t):
    assert all(t % (BAND * r) == 0 for r in DILATIONS)
    q5 = q_heads.reshape(N_GROUPS, KV_HEADS, GQA, n, t, HEAD_DIM)
    kv6 = kv_heads.reshape(N_GROUPS, 2, KV_HEADS, n, t, HEAD_DIM)
    gate3 = gate.reshape(n, t, ATTN_W)
    kv_spec = lambda which: pl.BlockSpec((None, None, None, None, t, HEAD_DIM),
                                         lambda b, h, g: (g, which, h, b, 0, 0))
    out = pl.pallas_call(
        functools.partial(_attn_prompt_kernel, t=t),
        grid=(n, KV_HEADS, N_GROUPS),
        in_specs=[
            pl.BlockSpec((None, None, GQA, None, t, HEAD_DIM), lambda b, h, g: (g, h, 0, b, 0, 0)),
            kv_spec(0), kv_spec(1),
            pl.BlockSpec((None, GQA, BAND, 2 * BAND), lambda b, h, g: (g, h, 0, 0)),
            pl.BlockSpec((None, t, GQA * HEAD_DIM), lambda b, h, g: (b, 0, h)),
        ],
        out_specs=pl.BlockSpec((None, t, GQA * HEAD_DIM), lambda b, h, g: (b, 0, h)),
        out_shape=jax.ShapeDtypeStruct((n, t, ATTN_W), BF16),
        scratch_shapes=[pltpu.VMEM((N_GROUPS, GQA, t, HEAD_DIM), F32),
                        pltpu.VMEM((N_GROUPS, t, LANES), F32),
                        pltpu.VMEM((N_GROUPS, t, LANES), F32)],
        compiler_params=_params("parallel", "parallel", "arbitrary"),
        name="attn_prompt",
    )(q5, kv6, kv6, bias, gate3)
    return out.reshape(n * t, ATTN_W)


def _attn_sample_kernel(q_ref, kvn_ref, s0_ref, s1_ref, s2_ref, bias_ref, gate_ref, o_ref):
    st_refs = (s0_ref, s1_ref, s2_ref)
    bf = lambda x: x.astype(BF16)
    rnd = lambda x: x.astype(BF16).astype(F32)
    for h in range(KV_HEADS):
        sl = slice(h * GQA, (h + 1) * GQA)
        s_old, s_new, v_old, v_new = [], [], [], []
        for g in range(N_GROUPS):
            q = q_ref[g * N_SLOTS + h * GQA:g * N_SLOTS + (h + 1) * GQA, :]
            k_old = st_refs[g][:, 0, h, :]
            v_old.append(st_refs[g][:, 1, h, :])
            row = g * 2 * KV_HEADS + h
            k_new = kvn_ref[row:row + 1, :]
            v_new.append(kvn_ref[row + KV_HEADS:row + KV_HEADS + 1, :])
            so = lax.dot_general(bf(q), bf(k_old), (((1,), (1,)), ((), ())), preferred_element_type=F32)
            s_old.append(so * SM_SCALE + bias_ref[g, sl, :BAND])
            sn = jnp.sum(rnd(q) * rnd(k_new), axis=-1, keepdims=True)
            s_new.append(sn * SM_SCALE + bias_ref[g, sl, BAND:BAND + 1])
        m = functools.reduce(jnp.maximum, [jnp.max(x, axis=-1, keepdims=True) for x in s_old] + s_new)
        num = jnp.zeros((GQA, HEAD_DIM), F32)
        den = jnp.zeros((GQA, 1), F32)
        for g in range(N_GROUPS):
            p_old = jnp.exp(s_old[g] - m)
            p_new = jnp.exp(s_new[g] - m)
            den = den + jnp.sum(p_old, axis=-1, keepdims=True) + p_new
            num = num + jnp.dot(bf(p_old), bf(v_old[g]), preferred_element_type=F32) + rnd(p_new) * rnd(v_new[g])
        o_ref[sl, :] = ((num / den) * jax.nn.silu(gate_ref[sl, :])).astype(o_ref.dtype)


def _attn_sample_call(q, kv_new, states, bias, gate):
    n = q.shape[0]
    st_views, st_specs = [], []
    for g, r in enumerate(DILATIONS):
        assert states[g].shape[1] == BAND * r
        st_views.append(states[g].reshape(n, BAND, r, 2, KV_HEADS, HEAD_DIM))
        st_specs.append(pl.BlockSpec((None, BAND, None, 2, KV_HEADS, HEAD_DIM), lambda b: (b, 0, 0, 0, 0, 0)))
    per_b = lambda rows: pl.BlockSpec((None, rows, HEAD_DIM), lambda b: (b, 0, 0))
    out = pl.pallas_call(
        _attn_sample_kernel,
        grid=(n,),
        in_specs=[per_b(N_GROUPS * N_SLOTS), per_b(N_GROUPS * 2 * KV_HEADS)] + st_specs + [
            pl.BlockSpec((N_GROUPS, N_SLOTS, 2 * BAND), lambda b: (0, 0, 0)), per_b(N_SLOTS)],
        out_specs=per_b(N_SLOTS),
        out_shape=jax.ShapeDtypeStruct((n, N_SLOTS, HEAD_DIM), BF16),
        compiler_params=_params("parallel"),
        name="attn_sample",
    )(q.reshape(n, N_GROUPS * N_SLOTS, HEAD_DIM), kv_new.reshape(n, N_GROUPS * 2 * KV_HEADS, HEAD_DIM),
      *st_views, bias, gate.reshape(n, N_SLOTS, HEAD_DIM))
    return out.reshape(n, ATTN_W)


def _trunks(xp, xs, conv_p, h_p, conv_s, h_s, kv_bufs, w):
    n, t, d = xp.shape
    ns = xs.shape[0]
    assert xs.shape[1] == 1
    n_a = w["a_w_in"].shape[0]
    n_b = w["b_w_in"].shape[0]
    xp2 = xp.reshape(n * t, d)
    xs2 = xs.reshape(ns, d)
    (xnp,) = _norm_call(xp2, [w["a_pre_g"][0]])
    (xns,) = _norm_call(xs2, [w["a_pre_g"][0]])
    conv_out_p, h_out_p, conv_out_s, h_out_s = [], [], [], []
    for l in range(n_a):
        mixer = (w["a_conv_w"][l], w["a_conv_b"][l], w["a_w_gate_a"][l], w["a_b_gate_a"][l],
                 w["a_w_gate_x"][l], w["a_b_gate_x"][l], w["a_lambda"][l])
        hgp, nc, nh, proj_s, w_out_b = _a_front_call(xnp, xns, w["a_w_in"], w["a_w_out"], l, conv_p[l], h_p[l],
                                                     *mixer, n, t)
        conv_out_p.append(nc)
        h_out_p.append(nh)
        hgs, nc, nh = _rglru_step_call(proj_s, conv_s[l], h_s[l], *mixer)
        conv_out_s.append(nc)
        h_out_s.append(nh)
        gains = [w["a_pre_g"][l + 1]] if l + 1 < n_a else [w["kv_norm_g"], w["b_pre_g"][0]]
        xp2, xnps, ys = _out_res_call(hgp, hgs, w_out_b, xp2, w["a_post_g"][l], gains,
                                      name="a_out_proj")
        xs2, xnss = _resnorm_call(ys, xs2, w["a_post_g"][l], gains)
        xnp, xns = xnps[0], xnss[0]

    kvp, kvp_heads, kvs = _mm_call(xnp, xns, w["w_kv"], heads=True, name="kv_proj")
    kv_out_p = []
    for g in range(N_GROUPS):
        keep = min(WINDOWS[g], t)
        kept = kvp.reshape(n, t, N_GROUPS * KV_GROUP_W)[:, t - keep:, g * KV_GROUP_W:(g + 1) * KV_GROUP_W]
        kv_out_p.append(kept.reshape(n, keep, 2, KV_HEADS, HEAD_DIM))
    kvs4 = kvs.reshape(ns, 1, N_GROUPS, 2, KV_HEADS, HEAD_DIM)
    kv_out_s = [jnp.concatenate([kv_bufs[g][:, 1:], kvs4[:, :, g]], axis=1) for g in range(N_GROUPS)]
    bias_p = _prompt_bias(w["rel_bias"])
    bias_s = _sample_bias(w["rel_bias"])

    xnp, xns = xnps[1], xnss[1]
    for l in range(n_b):
        gate_p, gate_s, w_out_b = _mm_call(xnp, xns, w["b_w_in"], layer=l, cols=(Q_W, ATTN_W),
                                           rounds=(w["b_w_out"], l), name="gate_proj")
        q_heads, q_s = _mm_call(xnp, xns, w["b_w_in"], layer=l, cols=(0, Q_W), natural=False, heads=True,
                                name="q_proj")
        ogp = _attn_prompt_call(q_heads, kvp_heads, bias_p, gate_p, n, t)
        ogs = _attn_sample_call(q_s, kvs, kv_bufs, bias_s, gate_s)
        gains = [w["b_pre_g"][l + 1]] if l + 1 < n_b else []
        xp2, xnps, ys = _out_res_call(ogp, ogs, w_out_b, xp2, w["b_post_g"][l], gains,
                                      name="b_out_proj")
        xs2, xnss = _resnorm_call(ys, xs2, w["b_post_g"][l], gains)
        if gains:
            xnp, xns = xnps[0], xnss[0]
    prompt_out = (xp2.reshape(n, t, d), jnp.stack(conv_out_p), jnp.stack(h_out_p), *kv_out_p)
    sample_out = (xs2.reshape(ns, 1, d), jnp.stack(conv_out_s), jnp.stack(h_out_s), *kv_out_s)
    return prompt_out, sample_out


def kernel(x_prompt, x_sample, state_conv, state_h, state_kv_w128, state_kv_w512, state_kv_w2048,
           a_pre_g, a_w_in, a_conv_w, a_conv_b, a_w_gate_a, a_b_gate_a, a_w_gate_x, a_b_gate_x,
           a_lambda, a_w_out, a_post_g, kv_norm_g, w_kv, rel_bias, b_pre_g, b_w_in, b_w_out, b_post_g):
    w = dict(
        a_pre_g=a_pre_g, a_w_in=a_w_in, a_conv_w=a_conv_w, a_conv_b=a_conv_b,
        a_w_gate_a=a_w_gate_a, a_b_gate_a=a_b_gate_a, a_w_gate_x=a_w_gate_x, a_b_gate_x=a_b_gate_x,
        a_lambda=a_lambda, a_w_out=a_w_out, a_post_g=a_post_g, kv_norm_g=kv_norm_g,
        w_kv=w_kv, rel_bias=rel_bias, b_pre_g=b_pre_g, b_w_in=b_w_in, b_w_out=b_w_out, b_post_g=b_post_g)
    nb_p = x_prompt.shape[0]
    n_a = a_w_in.shape[0]
    d_rnn = a_conv_w.shape[-1]
    conv0 = jnp.zeros((n_a, nb_p, CONV_W - 1, d_rnn), F32)
    h0 = jnp.zeros((n_a, nb_p, d_rnn), F32)
    prompt_out, sample_out = _trunks(x_prompt, x_sample, conv0, h0, state_conv, state_h,
                                     (state_kv_w128, state_kv_w512, state_kv_w2048), w)
    return (prompt_out[0], sample_out[0], *prompt_out[1:], *sample_out[1:])
```

```python
import functools

import numpy as np
import jax
import jax.numpy as jnp
from jax import lax
from jax.experimental import pallas as pl
from jax.experimental.pallas import tpu as pltpu

F32 = jnp.float32
BF16 = jnp.bfloat16

EPS = 1e-6
LRU_C = 8.0
CONV_W = 4
HEAD_DIM = 128
N_SLOTS = 16
KV_HEADS = 4
GQA = N_SLOTS // KV_HEADS
WINDOWS = (128, 512, 2048)
DILATIONS = (1, 4, 16)
N_GROUPS = 3
BAND = 128
ATTN_W = N_SLOTS * HEAD_DIM
Q_W = N_GROUPS * ATTN_W
KV_GROUP_W = 2 * KV_HEADS * HEAD_DIM
N_BUCKETS = 32
MAX_EXACT = 16
MAX_DIST = 2048
SM_SCALE = HEAD_DIM ** -0.5

ATTN_UNROLL = 8
LANES = 128
VMEM_LIMIT = 56 * 1024 * 1024


def _params(*sem):
    return pltpu.CompilerParams(dimension_semantics=sem, vmem_limit_bytes=VMEM_LIMIT)


def _tile(dim, pref):
    if dim <= pref:
        return dim
    t = pref
    while dim % t:
        t //= 2
    return t


def _round_slab(rows, steps):
    slab = rows // steps
    assert slab * steps == rows and slab % 16 == 0
    return slab


def _input_scale(a):
    y = 1.0 - a * a
    return jnp.where(y > 0.0, y * lax.rsqrt(y), 0.0)


def _rms(x, g):
    return x * lax.rsqrt(jnp.mean(x * x, axis=-1, keepdims=True) + EPS) * g


def _norm_kernel(x_ref, g_ref, *o_refs):
    x = x_ref[...]
    xhat = x * lax.rsqrt(jnp.mean(x * x, axis=-1, keepdims=True) + EPS)
    for k, o_ref in enumerate(o_refs):
        o_ref[...] = (xhat * g_ref[k:k + 1, :]).astype(o_ref.dtype)


def _norm_call(x, gains):
    m, d = x.shape
    tm = _tile(m, 256)
    g = jnp.stack(gains)
    ng = len(gains)
    row = pl.BlockSpec((tm, d), lambda i: (i, 0))
    return pl.pallas_call(
        _norm_kernel,
        grid=(m // tm,),
        in_specs=[row, pl.BlockSpec((ng, d), lambda i: (0, 0))],
        out_specs=[row] * ng,
        out_shape=[jax.ShapeDtypeStruct((m, d), BF16)] * ng,
        compiler_params=_params("parallel"),
        name="rmsnorm",
    )(x, g)


def _resnorm_kernel(y_ref, x_ref, gp_ref, g_ref, xo_ref, *o_refs):
    xn = x_ref[...] + _rms(y_ref[...], gp_ref[...])
    xo_ref[...] = xn
    if o_refs:
        xhat = xn * lax.rsqrt(jnp.mean(xn * xn, axis=-1, keepdims=True) + EPS)
        for k, o_ref in enumerate(o_refs):
            o_ref[...] = (xhat * g_ref[k:k + 1, :]).astype(o_ref.dtype)


def _resnorm_call(y, x, g_post, gains):
    m, d = x.shape
    tm = _tile(m, 256)
    ng = len(gains)
    g = jnp.stack(gains) if gains else jnp.ones((1, d), F32)
    row = pl.BlockSpec((tm, d), lambda i: (i, 0))
    outs = pl.pallas_call(
        _resnorm_kernel,
        grid=(m // tm,),
        in_specs=[row, row, pl.BlockSpec((1, d), lambda i: (0, 0)),
                  pl.BlockSpec((g.shape[0], d), lambda i: (0, 0))],
        out_specs=[row] * (1 + ng),
        out_shape=[jax.ShapeDtypeStruct((m, d), F32)] + [jax.ShapeDtypeStruct((m, d), BF16)] * ng,
        compiler_params=_params("parallel"),
        name="residual_rmsnorm",
    )(y, x, g_post.reshape(1, d), g)
    return outs[0], list(outs[1:])


def _mm_kernel(a_ref, w_ref, a2_ref, *refs, natural, heads, rounds):
    if rounds:
        src_ref, *refs = refs
    *o_refs, wb_ref = refs
    if rounds:
        *o_refs, dst_ref = o_refs
        dst_ref[...] = src_ref[...].astype(BF16)
    *o_refs, o2_ref = o_refs

    @pl.when(pl.program_id(1) == 0)
    def _():
        wb_ref[...] = w_ref[...].astype(BF16)
        o2_ref[...] = jnp.dot(a2_ref[...], wb_ref[...], preferred_element_type=F32)

    y = jnp.dot(a_ref[...], wb_ref[...], preferred_element_type=F32)
    k = 0
    if natural:
        o_refs[k][...] = y
        k += 1
    if heads:
        for h in range(y.shape[1] // LANES):
            o_refs[k][h] = y[:, h * LANES:(h + 1) * LANES]


def _mm_call(a, a2, w, *, layer=0, cols=None, natural=True, heads=False, rounds=None, tm=1024, tn=512,
             name="matmul"):
    if w.ndim == 2:
        w = w[None]
    m, kdim = a.shape
    m2 = a2.shape[0]
    col0, n = cols if cols is not None else (0, w.shape[2])
    tm = _tile(m, tm)
    tn = _tile(n, tn)
    assert col0 % tn == 0
    out_specs, out_shape = [], []
    if natural:
        out_specs.append(pl.BlockSpec((tm, tn), lambda j, i: (i, j)))
        out_shape.append(jax.ShapeDtypeStruct((m, n), F32))
    if heads:
        out_specs.append(pl.BlockSpec((tn // LANES, tm, LANES), lambda j, i: (j, i, 0)))
        out_shape.append(jax.ShapeDtypeStruct((n // LANES, m, LANES), F32))
    out_specs.append(pl.BlockSpec((m2, tn), lambda j, i: (0, j)))
    out_shape.append(jax.ShapeDtypeStruct((m2, n), F32))
    n_i = m // tm
    in_specs = [pl.BlockSpec((tm, kdim), lambda j, i: (i, 0)),
                pl.BlockSpec((None, kdim, tn), lambda j, i: (layer, 0, col0 // tn + j)),
                pl.BlockSpec((m2, kdim), lambda j, i: (0, 0))]
    operands = [a, w, a2]
    if rounds is not None:
        w3, r_layer = rounds
        slab = _round_slab(w3.shape[1], (n // tn) * n_i)
        in_specs.append(pl.BlockSpec((None, slab, w3.shape[2]), lambda j, i: (r_layer, j * n_i + i, 0)))
        out_specs.append(pl.BlockSpec((slab, w3.shape[2]), lambda j, i: (j * n_i + i, 0)))
        out_shape.append(jax.ShapeDtypeStruct(w3.shape[1:], BF16))
        operands.append(w3)
    return pl.pallas_call(
        functools.partial(_mm_kernel, natural=natural, heads=heads, rounds=rounds is not None),
        grid=(n // tn, n_i),
        in_specs=in_specs,
        out_specs=out_specs,
        out_shape=out_shape,
        scratch_shapes=[pltpu.VMEM((kdim, tn), BF16)],
        compiler_params=_params("parallel", "arbitrary"),
        name=name,
    )(*operands)


def _lane_fold_sq(v):
    return sum(v[:, s:s + LANES] * v[:, s:s + LANES] for s in range(0, v.shape[1], LANES))


def _times_rows(v, scale):
    return jnp.concatenate([v[:, s:s + LANES] * scale for s in range(0, v.shape[1], LANES)], axis=1)


def _out_res_kernel(a_ref, w_ref, a2_ref, x_ref, gp_ref, g_ref, xo_ref, *refs, n_row, n_gain, width):
    xn_refs = refs[:n_gain]
    o2_ref, y_scr, xnew_scr, ssq_y, ssq_x, sc_y, sc_x = refs[n_gain:]
    i = pl.program_id(0)
    j = pl.program_id(1)
    tn = w_ref.shape[1]
    cols = pl.ds(pl.multiple_of(j * tn, tn), tn)

    @pl.when((i == 0) & (j == 0))
    def _():
        ssq_x[...] = jnp.zeros(ssq_x.shape, F32)
        ssq_y[...] = jnp.zeros(ssq_y.shape, F32)

    @pl.when(j == 0)
    def _():
        for ssq, sc in ((ssq_x, sc_x), (ssq_y, sc_y)):
            mean = jnp.sum(ssq[...], axis=-1, keepdims=True) * (1.0 / width)
            sc[...] = jnp.broadcast_to(lax.rsqrt(mean + EPS), sc.shape)
            ssq[...] = jnp.zeros(ssq.shape, F32)

    if n_gain:
        @pl.when(i >= 2)
        def _():
            xhat = _times_rows(xnew_scr[:, cols], sc_x[...])
            for k in range(n_gain):
                xn_refs[k][...] = (xhat * g_ref[k:k + 1, :]).astype(xn_refs[k].dtype)

    @pl.when((i >= 1) & (i <= n_row))
    def _():
        xnew = x_ref[...] + _times_rows(y_scr[:, cols], sc_y[...]) * gp_ref[...]
        xo_ref[...] = xnew
        xnew_scr[:, cols] = xnew
        ssq_x[...] += _lane_fold_sq(xnew)

    @pl.when(i < n_row)
    def _():
        y = jnp.dot(a_ref[...], w_ref[...], preferred_element_type=F32)
        y_scr[:, cols] = y
        ssq_y[...] += _lane_fold_sq(y)

    @pl.when(i == 0)
    def _():
        o2_ref[...] = jnp.dot(a2_ref[...], w_ref[...], preferred_element_type=F32)


def _out_res_vmem_bytes(tm, tn, kdim, n, ng):
    return 2 * tm * n * 4 + 2 * (tm * kdim * 2 + kdim * tn * 2 + 2 * tm * tn * 4 + ng * tm * tn * 2)


def _out_res_call(a, a2, w, x, g_post, gains, *, tm=512, name="out_proj_residual"):
    m, kdim = a.shape
    n = w.shape[1]
    m2 = a2.shape[0]
    ng = len(gains)
    tn = next(c for c in (1024, 512, 256, 128) if n % c == 0
              and _out_res_vmem_bytes(tm, c, kdim, n, ng) <= VMEM_LIMIT * 4 // 5)
    assert m % tm == 0
    n_row, nj = m // tm, n // tn
    g = jnp.stack(gains) if gains else jnp.ones((1, n), F32)
    last = nj - 1
    lag1 = lambda i, j: (jnp.clip(i - 1, 0, n_row - 1), jnp.where(i < 1, 0, jnp.where(i <= n_row, j, last)))
    lag2 = lambda i, j: (jnp.clip(i - 2, 0, n_row - 1), jnp.where(i < 2, 0, j))
    outs = pl.pallas_call(
        functools.partial(_out_res_kernel, n_row=n_row, n_gain=ng, width=n),
        grid=(n_row + 2, nj),
        in_specs=[
            pl.BlockSpec((tm, kdim), lambda i, j: (jnp.minimum(i, n_row - 1), 0)),
            pl.BlockSpec((kdim, tn), lambda i, j: (0, jnp.where(i < n_row, j, last))),
            pl.BlockSpec((m2, kdim), lambda i, j: (0, 0)),
            pl.BlockSpec((tm, tn), lag1),
            pl.BlockSpec((1, tn), lambda i, j: (0, j)),
            pl.BlockSpec((g.shape[0], tn), lambda i, j: (0, j)),
        ],
        out_specs=[pl.BlockSpec((tm, tn), lag1)] + [pl.BlockSpec((tm, tn), lag2)] * ng + [
            pl.BlockSpec((m2, tn), lambda i, j: (0, jnp.where(i == 0, j, last)))],
        out_shape=[jax.ShapeDtypeStruct((m, n), F32)] + [jax.ShapeDtypeStruct((m, n), BF16)] * ng + [
            jax.ShapeDtypeStruct((m2, n), F32)],
        scratch_shapes=[pltpu.VMEM((tm, n), F32), pltpu.VMEM((tm, n), F32)] + [pltpu.VMEM((tm, LANES), F32)] * 4,
        compiler_params=_params("arbitrary", "arbitrary"),
        name=name,
    )(a, w, a2, x, g_post.reshape(1, n), g)
    return outs[0], list(outs[1:1 + ng]), outs[-1]


def _rglru_kernel(xb_ref, gate_ref, cs_ref, h0_ref, cw_ref, cb_ref, wga_ref, bga_ref, wgx_ref, bgx_ref,
                  lam_ref, hg_ref, nc_ref, nh_ref, *, blk):
    tail = CONV_W - 1
    xb = xb_ref[...]
    xconv = cb_ref[...]
    for k in range(tail):
        xconv = xconv + cs_ref[:, k, :] * cw_ref[k:k + 1, :]
        if k:
            nc_ref[:, k - 1, :] = cs_ref[:, k, :]
    xconv = xconv + xb * cw_ref[tail:tail + 1, :]
    nc_ref[:, tail - 1, :] = xb

    r_parts, i_parts = [], []
    for hh in range(xconv.shape[1] // blk):
        xh = xconv[:, hh * blk:(hh + 1) * blk].astype(BF16)
        r_parts.append(jnp.dot(xh, wga_ref[hh], preferred_element_type=F32))
        i_parts.append(jnp.dot(xh, wgx_ref[hh], preferred_element_type=F32))
    r = jax.nn.sigmoid(jnp.concatenate(r_parts, axis=1) + bga_ref[...])
    i = jax.nn.sigmoid(jnp.concatenate(i_parts, axis=1) + bgx_ref[...])

    neg_lam = -lam_ref[...]
    softplus = jnp.maximum(neg_lam, 0.0) + jnp.log1p(jnp.exp(-jnp.abs(neg_lam)))
    a = jnp.exp((-LRU_C * softplus) * r)
    h = a * h0_ref[...] + _input_scale(a) * (i * xconv)
    nh_ref[...] = h
    hg_ref[...] = (h * jax.nn.silu(gate_ref[...])).astype(hg_ref.dtype)


def _rglru_step_call(proj, conv_state, h0, conv_w, conv_b, w_ga, b_ga, w_gx, b_gx, lam):
    _, n, c = proj.shape
    heads, blk, _ = w_ga.shape
    cb = _tile(c, max(1024, blk))
    assert cb % blk == 0 and blk % LANES == 0
    ncb = c // cb
    row = lambda a: a.reshape(1, c)
    vec = pl.BlockSpec((1, cb), lambda j: (0, j))
    rows = pl.BlockSpec((n, cb), lambda j: (0, j))
    conv_rows = pl.BlockSpec((n, CONV_W - 1, cb), lambda j: (0, 0, j))
    wspec = pl.BlockSpec((cb // blk, blk, blk), lambda j: (j, 0, 0))
    return pl.pallas_call(
        functools.partial(_rglru_kernel, blk=blk),
        grid=(ncb,),
        in_specs=[pl.BlockSpec((None, n, cb), lambda j: (0, 0, j)), pl.BlockSpec((None, n, cb), lambda j: (1, 0, j)),
                  conv_rows, rows,
                  pl.BlockSpec((CONV_W, cb), lambda j: (0, j)), vec, wspec, vec, wspec, vec, vec],
        out_specs=[rows, conv_rows, rows],
        out_shape=[
            jax.ShapeDtypeStruct((n, c), BF16),
            jax.ShapeDtypeStruct((n, CONV_W - 1, c), F32),
            jax.ShapeDtypeStruct((n, c), F32),
        ],
        compiler_params=_params("parallel"),
        name="rglru_step",
    )(proj, proj, conv_state, h0, conv_w, row(conv_b), w_ga.astype(BF16), row(b_ga),
      w_gx.astype(BF16), row(b_gx), row(lam))


def _scan_rows(a, b, h):
    t, c = a.shape
    a3 = a.reshape(t // 8, 8, c)
    b3 = b.reshape(t // 8, 8, c)
    row = lax.broadcasted_iota(jnp.int32, a3.shape, 1)
    for d in (1, 2, 4):
        keep = row >= d
        a_prev = jnp.where(keep, pltpu.roll(a3, d, axis=1), 1.0)
        b_prev = jnp.where(keep, pltpu.roll(b3, d, axis=1), 0.0)
        b3 = a3 * b_prev + b3
        a3 = a3 * a_prev
    out = []
    for g in range(t // 8):
        hg = a3[g] * h + b3[g]
        out.append(hg)
        h = hg[7:8, :]
    return jnp.concatenate(out, axis=0), h


def _a_front_kernel(xn_ref, wx_ref, wg_ref, xs_ref, cs_ref, h0_ref, cw_ref, cb_ref, wga_ref, bga_ref, wgx_ref,
                    bgx_ref, lam_ref, wo_ref, hg_ref, nc_ref, nh_ref, ps_ref, wob_ref, wb_ref, tail_ref, hc_ref,
                    pr_ref, *, tt, sub, blk):
    b = pl.program_id(1)
    ti = pl.program_id(2)
    tail = CONV_W - 1
    n_sub = tt // sub
    cb = cs_ref.shape[1]

    wob_ref[...] = wo_ref[...].astype(BF16)

    @pl.when((b == 0) & (ti == 0))
    def _():
        wb_ref[:, :cb] = wx_ref[...].astype(BF16)
        wb_ref[:, cb:] = wg_ref[...].astype(BF16)
        ps = jnp.dot(xs_ref[...], wb_ref[...], preferred_element_type=F32)
        ps_ref[0] = ps[:, :cb]
        ps_ref[1] = ps[:, cb:]

    @pl.when(ti == 0)
    def _():
        tail_ref[...] = jnp.concatenate([jnp.zeros((8 - tail, cb), F32), cs_ref[...]], axis=0)
        hc_ref[...] = h0_ref[...]

    neg_lam = -lam_ref[...]
    softplus = jnp.maximum(neg_lam, 0.0) + jnp.log1p(jnp.exp(-jnp.abs(neg_lam)))
    log_a_scale = -LRU_C * softplus
    prev8 = tail_ref[...]
    h = hc_ref[...]

    def project(k):
        pr_ref[k] = jnp.dot(xn_ref[k * sub:(k + 1) * sub, :], wb_ref[...], preferred_element_type=F32)

    project(0)
    for k in range(n_sub):
        if k + 1 < n_sub:
            project(k + 1)
        xb = pr_ref[k, :, :cb]
        ext = jnp.concatenate([prev8, xb], axis=0)
        prev8 = xb[sub - 8:, :]
        ext1 = pltpu.roll(ext, 1, axis=0)
        pair = ext * cw_ref[1:2, :] + ext1 * cw_ref[0:1, :]
        xconv = (cb_ref[...] + pltpu.roll(pair, 2, axis=0)[8:, :]) + (ext1[8:, :] * cw_ref[2:3, :] + xb * cw_ref[3:4, :])
        r_parts, i_parts = [], []
        for hh in range(cb // blk):
            xh = xconv[:, hh * blk:(hh + 1) * blk].astype(BF16)
            r_parts.append(jnp.dot(xh, wga_ref[hh], preferred_element_type=F32))
            i_parts.append(jnp.dot(xh, wgx_ref[hh], preferred_element_type=F32))
        r = jax.nn.sigmoid(jnp.concatenate(r_parts, axis=1) + bga_ref[...])
        i = jax.nn.sigmoid(jnp.concatenate(i_parts, axis=1) + bgx_ref[...])
        a = jnp.exp(log_a_scale * r)
        bb = _input_scale(a) * (i * xconv)
        h_all, h = _scan_rows(a, bb, h)
        hg_ref[k * sub:(k + 1) * sub, :] = (h_all * jax.nn.silu(pr_ref[k, :, cb:])).astype(hg_ref.dtype)
    tail_ref[...] = prev8
    hc_ref[...] = h

    @pl.when(ti == pl.num_programs(2) - 1)
    def _():
        nc_ref[...] = prev8[8 - tail:, :]
        nh_ref[...] = h


def _a_front_call(xn, xs, w_in, w_out, layer, conv_state, h0, conv_w, conv_b, w_ga, b_ga, w_gx, b_gx, lam, n, t):
    d = xn.shape[1]
    c = w_in.shape[2] // 2
    heads, blk, _ = w_ga.shape
    cb = max(512, blk)
    tt, sub = 512, 128
    assert c % cb == 0 and cb % blk == 0 and blk % LANES == 0 and t % tt == 0
    ncb = c // cb
    ns = xs.shape[0]
    nt = t // tt
    slab = _round_slab(w_out.shape[1], ncb * n * nt)
    step = lambda j, b, i: (j * n + b) * nt + i
    row = lambda a: a.reshape(1, c)
    vec = pl.BlockSpec((1, cb), lambda j, b, i: (0, j))
    wspec = pl.BlockSpec((cb // blk, blk, blk), lambda j, b, i: (j, 0, 0))
    hg, nc, nh, ps, w_out_b = pl.pallas_call(
        functools.partial(_a_front_kernel, tt=tt, sub=sub, blk=blk),
        grid=(ncb, n, nt),
        in_specs=[
            pl.BlockSpec((None, tt, d), lambda j, b, i: (b, i, 0)),
            pl.BlockSpec((None, d, cb), lambda j, b, i: (layer, 0, j)),
            pl.BlockSpec((None, d, cb), lambda j, b, i: (layer, 0, ncb + j)),
            pl.BlockSpec((ns, d), lambda j, b, i: (0, 0)),
            pl.BlockSpec((None, CONV_W - 1, cb), lambda j, b, i: (b, 0, j)),
            pl.BlockSpec((None, 1, cb), lambda j, b, i: (b, 0, j)),
            pl.BlockSpec((CONV_W, cb), lambda j, b, i: (0, j)),
            vec, wspec, vec, wspec, vec, vec,
            pl.BlockSpec((None, slab, w_out.shape[2]), lambda j, b, i: (layer, step(j, b, i), 0)),
        ],
        out_specs=[
            pl.BlockSpec((None, tt, cb), lambda j, b, i: (b, i, j)),
            pl.BlockSpec((None, CONV_W - 1, cb), lambda j, b, i: (b, 0, j)),
            pl.BlockSpec((None, 1, cb), lambda j, b, i: (b, 0, j)),
            pl.BlockSpec((2, ns, cb), lambda j, b, i: (0, 0, j)),
            pl.BlockSpec((slab, w_out.shape[2]), lambda j, b, i: (step(j, b, i), 0)),
        ],
        out_shape=[
            jax.ShapeDtypeStruct((n, t, c), BF16),
            jax.ShapeDtypeStruct((n, CONV_W - 1, c), F32),
            jax.ShapeDtypeStruct((n, 1, c), F32),
            jax.ShapeDtypeStruct((2, ns, c), F32),
            jax.ShapeDtypeStruct(w_out.shape[1:], BF16),
        ],
        scratch_shapes=[
            pltpu.VMEM((d, 2 * cb), BF16),
            pltpu.VMEM((8, cb), F32),
            pltpu.VMEM((1, cb), F32),
            pltpu.VMEM((tt // sub, sub, 2 * cb), F32),
        ],
        compiler_params=_params("parallel", "arbitrary", "arbitrary"),
        name="a_in_rglru",
    )(xn.reshape(n, t, d), w_in, w_in, xs, conv_state, h0.reshape(n, 1, c), conv_w, row(conv_b),
      w_ga.astype(BF16), row(b_ga), w_gx.astype(BF16), row(b_gx), row(lam), w_out)
    return hg.reshape(n * t, c), nc, nh.reshape(n, c), ps, w_out_b


def _rel_bucket(dist):
    dist = np.asarray(dist)
    d = np.maximum(dist, 1).astype(np.float32)
    large = MAX_EXACT + (np.log(d / MAX_EXACT) / np.log(MAX_DIST / MAX_EXACT)
                         * (N_BUCKETS - MAX_EXACT)).astype(np.int32)
    large = np.minimum(large, N_BUCKETS - 1)
    return np.where(dist < MAX_EXACT, dist, large).astype(np.int32)


def _prompt_bias(rel_bias):
    span = 3 * BAND - 1
    neg = jnp.full((N_SLOTS, BAND - 1), -jnp.inf, F32)
    tables = []
    for g, r in enumerate(DILATIONS):
        tg = rel_bias[:, g * N_SLOTS:(g + 1) * N_SLOTS].astype(F32)
        tb = tg[_rel_bucket(np.arange(BAND + 1) * r)].T
        wr = jnp.concatenate([neg, tb[:, ::-1], neg, neg[:, :1]], axis=1)
        skew = jnp.tile(wr, (1, BAND))[:, :BAND * span].reshape(N_SLOTS, BAND, span)
        tables.append(skew[:, :, BAND - 1:])
    return jnp.stack(tables)


def _sample_bias(rel_bias):
    tables = []
    for g, r in enumerate(DILATIONS):
        steps = np.concatenate([BAND - np.arange(BAND), np.zeros(BAND, np.int64)])
        tg = rel_bias[:, g * N_SLOTS:(g + 1) * N_SLOTS].astype(F32)
        tables.append(tg[_rel_bucket(steps * r)].T)
    return jnp.stack(tables)


def _attn_prompt_kernel(q_ref, k_ref, v_ref, bias_ref, gate_ref, o_ref, acc_ref, m_ref, l_ref, *, t):
    g = pl.program_id(2)

    def unit(start, gi, r, has_prev):
        nk = 2 * BAND if has_prev else BAND
        kstart = start - BAND * r if has_prev else start
        rows = lambda s0, cnt: pl.ds(s0, cnt, stride=r) if r > 1 else pl.ds(s0, cnt)
        k = k_ref[rows(kstart, nk), :].astype(BF16)
        v = v_ref[rows(kstart, nk), :].astype(BF16)
        q = jnp.concatenate([q_ref[s, rows(start, BAND), :] for s in range(GQA)], axis=0).astype(BF16)
        sc = lax.dot_general(q, k, (((1,), (1,)), ((), ())), preferred_element_type=F32) * SM_SCALE
        sc = sc + bias_ref[:, :, 2 * BAND - nk:].reshape(GQA * BAND, nk)
        m = jnp.max(sc, axis=-1, keepdims=True)
        p = jnp.exp(sc - m)
        l = jnp.sum(p, axis=-1, keepdims=True)
        pv = jnp.dot(p.astype(BF16), v, preferred_element_type=F32)
        lane = lax.broadcasted_iota(jnp.int32, (BAND, LANES), 1)
        m_tile = jnp.zeros((BAND, LANES), F32)
        l_tile = jnp.ones((BAND, LANES), F32)
        for s in range(GQA):
            blk = slice(s * BAND, (s + 1) * BAND)
            acc_ref[gi, s, rows(start, BAND), :] = pv[blk]
            m_tile = jnp.where(lane == s, m[blk], m_tile)
            l_tile = jnp.where(lane == s, l[blk], l_tile)
        m_ref[gi, rows(start, BAND), :] = m_tile
        l_ref[gi, rows(start, BAND), :] = l_tile

    def group_pass(gi, r):
        nb = t // (BAND * r)

        def head_block(cls, carry):
            unit(cls, gi, r, False)
            return carry

        lax.fori_loop(0, r, head_block, 0, unroll=min(r, ATTN_UNROLL))

        def later_block(u, carry):
            cls = u % r
            blk = 1 + u // r
            unit(blk * (BAND * r) + cls, gi, r, True)
            return carry

        if nb > 1:
            lax.fori_loop(0, r * (nb - 1), later_block, 0, unroll=ATTN_UNROLL)

    for gi, r in enumerate(DILATIONS):
        pl.when(g == gi)(functools.partial(group_pass, gi, r))

    @pl.when(g == N_GROUPS - 1)
    def _():
        def combine(c, carry):
            rows = pl.ds(pl.multiple_of(c * BAND, BAND), BAND)
            ms = [m_ref[gi, rows, :] for gi in range(N_GROUPS)]
            mx = functools.reduce(jnp.maximum, ms)
            es = [jnp.exp(mg - mx) for mg in ms]
            den = sum(es[gi] * l_ref[gi, rows, :] for gi in range(N_GROUPS))
            ws = [e / den for e in es]
            for s in range(GQA):
                o = sum(ws[gi][:, s:s + 1] * acc_ref[gi, s, rows, :] for gi in range(N_GROUPS))
                cols = slice(s * HEAD_DIM, (s + 1) * HEAD_DIM)
                o_ref[rows, cols] = (o * jax.nn.silu(gate_ref[rows, cols])).astype(o_ref.dtype)
            return carry

        lax.fori_loop(0, t // BAND, combine, 0, unroll=2)


def _attn_prompt_call(q_heads, kv_heads, bias, gate, n, t):
    assert all(t % (BAND * r) == 0 for r in DILATIONS)
    q5 = q_heads.reshape(N_GROUPS, KV_HEADS, GQA, n, t, HEAD_DIM)
    kv6 = kv_heads.reshape(N_GROUPS, 2, KV_HEADS, n, t, HEAD_DIM)
    gate3 = gate.reshape(n, t, ATTN_W)
    kv_spec = lambda which: pl.BlockSpec((None, None, None, None, t, HEAD_DIM),
                                         lambda b, h, g: (g, which, h, b, 0, 0))
    out = pl.pallas_call(
        functools.partial(_attn_prompt_kernel, t=t),
        grid=(n, KV_HEADS, N_GROUPS),
        in_specs=[
            pl.BlockSpec((None, None, GQA, None, t, HEAD_DIM), lambda b, h, g: (g, h, 0, b, 0, 0)),
            kv_spec(0), kv_spec(1),
            pl.BlockSpec((None, GQA, BAND, 2 * BAND), lambda b, h, g: (g, h, 0, 0)),
            pl.BlockSpec((None, t, GQA * HEAD_DIM), lambda b, h, g: (b, 0, h)),
        ],
        out_specs=pl.BlockSpec((None, t, GQA * HEAD_DIM), lambda b, h, g: (b, 0, h)),
        out_shape=jax.ShapeDtypeStruct((n, t, ATTN_W), BF16),
        scratch_shapes=[pltpu.VMEM((N_GROUPS, GQA, t, HEAD_DIM), F32),
                        pltpu.VMEM((N_GROUPS, t, LANES), F32),
                        pltpu.VMEM((N_GROUPS, t, LANES), F32)],
        compiler_params=_params("parallel", "parallel", "arbitrary"),
        name="attn_prompt",
    )(q5, kv6, kv6, bias, gate3)
    return out.reshape(n * t, ATTN_W)


def _attn_sample_kernel(q_ref, kvn_ref, s0_ref, s1_ref, s2_ref, bias_ref, gate_ref, o_ref):
    st_refs = (s0_ref, s1_ref, s2_ref)
    bf = lambda x: x.astype(BF16)
    rnd = lambda x: x.astype(BF16).astype(F32)
    for h in range(KV_HEADS):
        sl = slice(h * GQA, (h + 1) * GQA)
        s_old, s_new, v_old, v_new = [], [], [], []
        for g in range(N_GROUPS):
            q = q_ref[g * N_SLOTS + h * GQA:g * N_SLOTS + (h + 1) * GQA, :]
            k_old = st_refs[g][:, 0, h, :]
            v_old.append(st_refs[g][:, 1, h, :])
            row = g * 2 * KV_HEADS + h
            k_new = kvn_ref[row:row + 1, :]
            v_new.append(kvn_ref[row + KV_HEADS:row + KV_HEADS + 1, :])
            so = lax.dot_general(bf(q), bf(k_old), (((1,), (1,)), ((), ())), preferred_element_type=F32)
            s_old.append(so * SM_SCALE + bias_ref[g, sl, :BAND])
            sn = jnp.sum(rnd(q) * rnd(k_new), axis=-1, keepdims=True)
            s_new.append(sn * SM_SCALE + bias_ref[g, sl, BAND:BAND + 1])
        m = functools.reduce(jnp.maximum, [jnp.max(x, axis=-1, keepdims=True) for x in s_old] + s_new)
        num = jnp.zeros((GQA, HEAD_DIM), F32)
        den = jnp.zeros((GQA, 1), F32)
        for g in range(N_GROUPS):
            p_old = jnp.exp(s_old[g] - m)
            p_new = jnp.exp(s_new[g] - m)
            den = den + jnp.sum(p_old, axis=-1, keepdims=True) + p_new
            num = num + jnp.dot(bf(p_old), bf(v_old[g]), preferred_element_type=F32) + rnd(p_new) * rnd(v_new[g])
        o_ref[sl, :] = ((num / den) * jax.nn.silu(gate_ref[sl, :])).astype(o_ref.dtype)


def _attn_sample_call(q, kv_new, states, bias, gate):
    n = q.shape[0]
    st_views, st_specs = [], []
    for g, r in enumerate(DILATIONS):
        assert states[g].shape[1] == BAND * r
        st_views.append(states[g].reshape(n, BAND, r, 2, KV_HEADS, HEAD_DIM))
        st_specs.append(pl.BlockSpec((None, BAND, None, 2, KV_HEADS, HEAD_DIM), lambda b: (b, 0, 0, 0, 0, 0)))
    per_b = lambda rows: pl.BlockSpec((None, rows, HEAD_DIM), lambda b: (b, 0, 0))
    out = pl.pallas_call(
        _attn_sample_kernel,
        grid=(n,),
        in_specs=[per_b(N_GROUPS * N_SLOTS), per_b(N_GROUPS * 2 * KV_HEADS)] + st_specs + [
            pl.BlockSpec((N_GROUPS, N_SLOTS, 2 * BAND), lambda b: (0, 0, 0)), per_b(N_SLOTS)],
        out_specs=per_b(N_SLOTS),
        out_shape=jax.ShapeDtypeStruct((n, N_SLOTS, HEAD_DIM), BF16),
        compiler_params=_params("parallel"),
        name="attn_sample",
    )(q.reshape(n, N_GROUPS * N_SLOTS, HEAD_DIM), kv_new.reshape(n, N_GROUPS * 2 * KV_HEADS, HEAD_DIM),
      *st_views, bias, gate.reshape(n, N_SLOTS, HEAD_DIM))
    return out.reshape(n, ATTN_W)


def _trunks(xp, xs, conv_p, h_p, conv_s, h_s, kv_bufs, w):
    n, t, d = xp.shape
    ns = xs.shape[0]
    assert xs.shape[1] == 1
    n_a = w["a_w_in"].shape[0]
    n_b = w["b_w_in"].shape[0]
    xp2 = xp.reshape(n * t, d)
    xs2 = xs.reshape(ns, d)
    (xnp,) = _norm_call(xp2, [w["a_pre_g"][0]])
    (xns,) = _norm_call(xs2, [w["a_pre_g"][0]])
    conv_out_p, h_out_p, conv_out_s, h_out_s = [], [], [], []
    for l in range(n_a):
        mixer = (w["a_conv_w"][l], w["a_conv_b"][l], w["a_w_gate_a"][l], w["a_b_gate_a"][l],
                 w["a_w_gate_x"][l], w["a_b_gate_x"][l], w["a_lambda"][l])
        hgp, nc, nh, proj_s, w_out_b = _a_front_call(xnp, xns, w["a_w_in"], w["a_w_out"], l, conv_p[l], h_p[l],
                                                     *mixer, n, t)
        conv_out_p.append(nc)
        h_out_p.append(nh)
        hgs, nc, nh = _rglru_step_call(proj_s, conv_s[l], h_s[l], *mixer)
        conv_out_s.append(nc)
        h_out_s.append(nh)
        gains = [w["a_pre_g"][l + 1]] if l + 1 < n_a else [w["kv_norm_g"], w["b_pre_g"][0]]
        xp2, xnps, ys = _out_res_call(hgp, hgs, w_out_b, xp2, w["a_post_g"][l], gains,
                                      name="a_out_proj")
        xs2, xnss = _resnorm_call(ys, xs2, w["a_post_g"][l], gains)
        xnp, xns = xnps[0], xnss[0]

    kvp, kvp_heads, kvs = _mm_call(xnp, xns, w["w_kv"], heads=True, name="kv_proj")
    kv_out_p = []
    for g in range(N_GROUPS):
        keep = min(WINDOWS[g], t)
        kept = kvp.reshape(n, t, N_GROUPS * KV_GROUP_W)[:, t - keep:, g * KV_GROUP_W:(g + 1) * KV_GROUP_W]
        kv_out_p.append(kept.reshape(n, keep, 2, KV_HEADS, HEAD_DIM))
    kvs4 = kvs.reshape(ns, 1, N_GROUPS, 2, KV_HEADS, HEAD_DIM)
    kv_out_s = [jnp.concatenate([kv_bufs[g][:, 1:], kvs4[:, :, g]], axis=1) for g in range(N_GROUPS)]
    bias_p = _prompt_bias(w["rel_bias"])
    bias_s = _sample_bias(w["rel_bias"])

    xnp, xns = xnps[1], xnss[1]
    for l in range(n_b):
        gate_p, gate_s, w_out_b = _mm_call(xnp, xns, w["b_w_in"], layer=l, cols=(Q_W, ATTN_W),
                                           rounds=(w["b_w_out"], l), name="gate_proj")
        q_heads, q_s = _mm_call(xnp, xns, w["b_w_in"], layer=l, cols=(0, Q_W), natural=False, heads=True,
                                name="q_proj")
        ogp = _attn_prompt_call(q_heads, kvp_heads, bias_p, gate_p, n, t)
        ogs = _attn_sample_call(q_s, kvs, kv_bufs, bias_s, gate_s)
        gains = [w["b_pre_g"][l + 1]] if l + 1 < n_b else []
        xp2, xnps, ys = _out_res_call(ogp, ogs, w_out_b, xp2, w["b_post_g"][l], gains,
                                      name="b_out_proj")
        xs2, xnss = _resnorm_call(ys, xs2, w["b_post_g"][l], gains)
        if gains:
            xnp, xns = xnps[0], xnss[0]
    prompt_out = (xp2.reshape(n, t, d), jnp.stack(conv_out_p), jnp.stack(h_out_p), *kv_out_p)
    sample_out = (xs2.reshape(ns, 1, d), jnp.stack(conv_out_s), jnp.stack(h_out_s), *kv_out_s)
    return prompt_out, sample_out


def kernel(x_prompt, x_sample, state_conv, state_h, state_kv_w128, state_kv_w512, state_kv_w2048,
           a_pre_g, a_w_in, a_conv_w, a_conv_b, a_w_gate_a, a_b_gate_a, a_w_gate_x, a_b_gate_x,
           a_lambda, a_w_out, a_post_g, kv_norm_g, w_kv, rel_bias, b_pre_g, b_w_in, b_w_out, b_post_g):
    w = dict(
        a_pre_g=a_pre_g, a_w_in=a_w_in, a_conv_w=a_conv_w, a_conv_b=a_conv_b,
        a_w_gate_a=a_w_gate_a, a_b_gate_a=a_b_gate_a, a_w_gate_x=a_w_gate_x, a_b_gate_x=a_b_gate_x,
        a_lambda=a_lambda, a_w_out=a_w_out, a_post_g=a_post_g, kv_norm_g=kv_norm_g,
        w_kv=w_kv, rel_bias=rel_bias, b_pre_g=b_pre_g, b_w_in=b_w_in, b_w_out=b_w_out, b_post_g=b_post_g)
    nb_p = x_prompt.shape[0]
    n_a = a_w_in.shape[0]
    d_rnn = a_conv_w.shape[-1]
    conv0 = jnp.zeros((n_a, nb_p, CONV_W - 1, d_rnn), F32)
    h0 = jnp.zeros((n_a, nb_p, d_rnn), F32)
    prompt_out, sample_out = _trunks(x_prompt, x_sample, conv0, h0, state_conv, state_h,
                                     (state_kv_w128, state_kv_w512, state_kv_w2048), w)
    return (prompt_out[0], sample_out[0], *prompt_out[1:], *sample_out[1:])
```

```python
import functools

import numpy as np
import jax
import jax.numpy as jnp
from jax import lax
from jax.experimental import pallas as pl
from jax.experimental.pallas import tpu as pltpu

F32 = jnp.float32
BF16 = jnp.bfloat16

EPS = 1e-6
LRU_C = 8.0
LOG2_E = 1.4426950408889634
CONV_W = 4
HEAD_DIM = 128
N_SLOTS = 16
KV_HEADS = 4
GQA = N_SLOTS // KV_HEADS
WINDOWS = (128, 512, 2048)
DILATIONS = (1, 4, 16)
N_GROUPS = 3
BAND = 128
ATTN_W = N_SLOTS * HEAD_DIM
Q_W = N_GROUPS * ATTN_W
KV_GROUP_W = 2 * KV_HEADS * HEAD_DIM
N_BUCKETS = 32
MAX_EXACT = 16
MAX_DIST = 2048
SM_SCALE = HEAD_DIM ** -0.5

ATTN_UNROLL = 8
LANES = 128
VMEM_LIMIT = 56 * 1024 * 1024


def _params(*sem):
    return pltpu.CompilerParams(dimension_semantics=sem, vmem_limit_bytes=VMEM_LIMIT)


def _tile(dim, pref):
    if dim <= pref:
        return dim
    t = pref
    while dim % t:
        t //= 2
    return t


def _round_slab(rows, steps):
    slab = rows // steps
    assert slab * steps == rows and slab % 16 == 0
    return slab


def _input_scale(a):
    y = 1.0 - a * a
    return jnp.where(y > 0.0, y * lax.rsqrt(y), 0.0)


def _rms(x, g):
    return x * lax.rsqrt(jnp.mean(x * x, axis=-1, keepdims=True) + EPS) * g


def _norm_kernel(x_ref, g_ref, *o_refs):
    x = x_ref[...]
    xhat = x * lax.rsqrt(jnp.mean(x * x, axis=-1, keepdims=True) + EPS)
    for k, o_ref in enumerate(o_refs):
        o_ref[...] = (xhat * g_ref[k:k + 1, :]).astype(o_ref.dtype)


def _norm_call(x, gains):
    m, d = x.shape
    tm = _tile(m, 256)
    g = jnp.stack(gains)
    ng = len(gains)
    row = pl.BlockSpec((tm, d), lambda i: (i, 0))
    return pl.pallas_call(
        _norm_kernel,
        grid=(m // tm,),
        in_specs=[row, pl.BlockSpec((ng, d), lambda i: (0, 0))],
        out_specs=[row] * ng,
        out_shape=[jax.ShapeDtypeStruct((m, d), BF16)] * ng,
        compiler_params=_params("parallel"),
        name="rmsnorm",
    )(x, g)


def _resnorm_kernel(y_ref, x_ref, gp_ref, g_ref, xo_ref, *o_refs):
    xn = x_ref[...] + _rms(y_ref[...], gp_ref[...])
    xo_ref[...] = xn
    if o_refs:
        xhat = xn * lax.rsqrt(jnp.mean(xn * xn, axis=-1, keepdims=True) + EPS)
        for k, o_ref in enumerate(o_refs):
            o_ref[...] = (xhat * g_ref[k:k + 1, :]).astype(o_ref.dtype)


def _resnorm_call(y, x, g_post, gains):
    m, d = x.shape
    tm = _tile(m, 256)
    ng = len(gains)
    g = jnp.stack(gains) if gains else jnp.ones((1, d), F32)
    row = pl.BlockSpec((tm, d), lambda i: (i, 0))
    outs = pl.pallas_call(
        _resnorm_kernel,
        grid=(m // tm,),
        in_specs=[row, row, pl.BlockSpec((1, d), lambda i: (0, 0)),
                  pl.BlockSpec((g.shape[0], d), lambda i: (0, 0))],
        out_specs=[row] * (1 + ng),
        out_shape=[jax.ShapeDtypeStruct((m, d), F32)] + [jax.ShapeDtypeStruct((m, d), BF16)] * ng,
        compiler_params=_params("parallel"),
        name="residual_rmsnorm",
    )(y, x, g_post.reshape(1, d), g)
    return outs[0], list(outs[1:])


def _mm_kernel(a_ref, w_ref, a2_ref, *refs, natural, heads, rounds):
    if rounds:
        src_ref, *refs = refs
    *o_refs, wb_ref = refs
    if rounds:
        *o_refs, dst_ref = o_refs
        dst_ref[...] = src_ref[...].astype(BF16)
    *o_refs, o2_ref = o_refs

    @pl.when(pl.program_id(1) == 0)
    def _():
        wb_ref[...] = w_ref[...].astype(BF16)
        o2_ref[...] = jnp.dot(a2_ref[...], wb_ref[...], preferred_element_type=F32)

    y = jnp.dot(a_ref[...], wb_ref[...], preferred_element_type=F32)
    k = 0
    if natural:
        o_refs[k][...] = y
        k += 1
    if heads:
        for h in range(y.shape[1] // LANES):
            o_refs[k][h] = y[:, h * LANES:(h + 1) * LANES]


def _mm_call(a, a2, w, *, layer=0, cols=None, natural=True, heads=False, rounds=None, tm=1024, tn=512,
             name="matmul"):
    if w.ndim == 2:
        w = w[None]
    m, kdim = a.shape
    m2 = a2.shape[0]
    col0, n = cols if cols is not None else (0, w.shape[2])
    tm = _tile(m, tm)
    tn = _tile(n, tn)
    assert col0 % tn == 0
    out_specs, out_shape = [], []
    if natural:
        out_specs.append(pl.BlockSpec((tm, tn), lambda j, i: (i, j)))
        out_shape.append(jax.ShapeDtypeStruct((m, n), F32))
    if heads:
        out_specs.append(pl.BlockSpec((tn // LANES, tm, LANES), lambda j, i: (j, i, 0)))
        out_shape.append(jax.ShapeDtypeStruct((n // LANES, m, LANES), F32))
    out_specs.append(pl.BlockSpec((m2, tn), lambda j, i: (0, j)))
    out_shape.append(jax.ShapeDtypeStruct((m2, n), F32))
    n_i = m // tm
    in_specs = [pl.BlockSpec((tm, kdim), lambda j, i: (i, 0)),
                pl.BlockSpec((None, kdim, tn), lambda j, i: (layer, 0, col0 // tn + j)),
                pl.BlockSpec((m2, kdim), lambda j, i: (0, 0))]
    operands = [a, w, a2]
    if rounds is not None:
        w3, r_layer = rounds
        slab = _round_slab(w3.shape[1], (n // tn) * n_i)
        in_specs.append(pl.BlockSpec((None, slab, w3.shape[2]), lambda j, i: (r_layer, j * n_i + i, 0)))
        out_specs.append(pl.BlockSpec((slab, w3.shape[2]), lambda j, i: (j * n_i + i, 0)))
        out_shape.append(jax.ShapeDtypeStruct(w3.shape[1:], BF16))
        operands.append(w3)
    return pl.pallas_call(
        functools.partial(_mm_kernel, natural=natural, heads=heads, rounds=rounds is not None),
        grid=(n // tn, n_i),
        in_specs=in_specs,
        out_specs=out_specs,
        out_shape=out_shape,
        scratch_shapes=[pltpu.VMEM((kdim, tn), BF16)],
        compiler_params=_params("parallel", "arbitrary"),
        name=name,
    )(*operands)


def _lane_fold_sq(v):
    return sum(v[:, s:s + LANES] * v[:, s:s + LANES] for s in range(0, v.shape[1], LANES))


def _times_rows(v, scale):
    return jnp.concatenate([v[:, s:s + LANES] * scale for s in range(0, v.shape[1], LANES)], axis=1)


def _out_res_kernel(a_ref, w_ref, a2_ref, x_ref, gp_ref, g_ref, xo_ref, *refs, n_row, n_gain, width):
    xn_refs = refs[:n_gain]
    o2_ref, y_scr, xnew_scr, ssq_y, ssq_x, sc_y, sc_x = refs[n_gain:]
    i = pl.program_id(0)
    j = pl.program_id(1)
    tn = w_ref.shape[1]
    cols = pl.ds(pl.multiple_of(j * tn, tn), tn)

    @pl.when((i == 0) & (j == 0))
    def _():
        ssq_x[...] = jnp.zeros(ssq_x.shape, F32)
        ssq_y[...] = jnp.zeros(ssq_y.shape, F32)

    @pl.when(j == 0)
    def _():
        for ssq, sc in ((ssq_x, sc_x), (ssq_y, sc_y)):
            mean = jnp.sum(ssq[...], axis=-1, keepdims=True) * (1.0 / width)
            sc[...] = jnp.broadcast_to(lax.rsqrt(mean + EPS), sc.shape)
            ssq[...] = jnp.zeros(ssq.shape, F32)

    if n_gain:
        @pl.when(i >= 2)
        def _():
            xhat = _times_rows(xnew_scr[:, cols], sc_x[...])
            for k in range(n_gain):
                xn_refs[k][...] = (xhat * g_ref[k:k + 1, :]).astype(xn_refs[k].dtype)

    @pl.when((i >= 1) & (i <= n_row))
    def _():
        xnew = x_ref[...] + _times_rows(y_scr[:, cols], sc_y[...]) * gp_ref[...]
        xo_ref[...] = xnew
        xnew_scr[:, cols] = xnew
        ssq_x[...] += _lane_fold_sq(xnew)

    @pl.when(i < n_row)
    def _():
        y = jnp.dot(a_ref[...], w_ref[...], preferred_element_type=F32)
        y_scr[:, cols] = y
        ssq_y[...] += _lane_fold_sq(y)

    @pl.when(i == 0)
    def _():
        o2_ref[...] = jnp.dot(a2_ref[...], w_ref[...], preferred_element_type=F32)


def _out_res_vmem_bytes(tm, tn, kdim, n, ng):
    return 2 * tm * n * 4 + 2 * (tm * kdim * 2 + kdim * tn * 2 + 2 * tm * tn * 4 + ng * tm * tn * 2)


def _out_res_call(a, a2, w, x, g_post, gains, *, tm=512, name="out_proj_residual"):
    m, kdim = a.shape
    n = w.shape[1]
    m2 = a2.shape[0]
    ng = len(gains)
    tn = next(c for c in (1024, 512, 256, 128) if n % c == 0
              and _out_res_vmem_bytes(tm, c, kdim, n, ng) <= VMEM_LIMIT * 4 // 5)
    assert m % tm == 0
    n_row, nj = m // tm, n // tn
    g = jnp.stack(gains) if gains else jnp.ones((1, n), F32)
    last = nj - 1
    lag1 = lambda i, j: (jnp.clip(i - 1, 0, n_row - 1), jnp.where(i < 1, 0, jnp.where(i <= n_row, j, last)))
    lag2 = lambda i, j: (jnp.clip(i - 2, 0, n_row - 1), jnp.where(i < 2, 0, j))
    outs = pl.pallas_call(
        functools.partial(_out_res_kernel, n_row=n_row, n_gain=ng, width=n),
        grid=(n_row + 2, nj),
        in_specs=[
            pl.BlockSpec((tm, kdim), lambda i, j: (jnp.minimum(i, n_row - 1), 0)),
            pl.BlockSpec((kdim, tn), lambda i, j: (0, jnp.where(i < n_row, j, last))),
            pl.BlockSpec((m2, kdim), lambda i, j: (0, 0)),
            pl.BlockSpec((tm, tn), lag1),
            pl.BlockSpec((1, tn), lambda i, j: (0, j)),
            pl.BlockSpec((g.shape[0], tn), lambda i, j: (0, j)),
        ],
        out_specs=[pl.BlockSpec((tm, tn), lag1)] + [pl.BlockSpec((tm, tn), lag2)] * ng + [
            pl.BlockSpec((m2, tn), lambda i, j: (0, jnp.where(i == 0, j, last)))],
        out_shape=[jax.ShapeDtypeStruct((m, n), F32)] + [jax.ShapeDtypeStruct((m, n), BF16)] * ng + [
            jax.ShapeDtypeStruct((m2, n), F32)],
        scratch_shapes=[pltpu.VMEM((tm, n), F32), pltpu.VMEM((tm, n), F32)] + [pltpu.VMEM((tm, LANES), F32)] * 4,
        compiler_params=_params("arbitrary", "arbitrary"),
        name=name,
    )(a, w, a2, x, g_post.reshape(1, n), g)
    return outs[0], list(outs[1:1 + ng]), outs[-1]


def _rglru_kernel(xb_ref, gate_ref, cs_ref, h0_ref, cw_ref, cb_ref, wga_ref, bga_ref, wgx_ref, bgx_ref,
                  lam_ref, hg_ref, nc_ref, nh_ref, *, blk):
    tail = CONV_W - 1
    xb = xb_ref[...]
    xconv = cb_ref[...]
    for k in range(tail):
        xconv = xconv + cs_ref[:, k, :] * cw_ref[k:k + 1, :]
        if k:
            nc_ref[:, k - 1, :] = cs_ref[:, k, :]
    xconv = xconv + xb * cw_ref[tail:tail + 1, :]
    nc_ref[:, tail - 1, :] = xb

    r_parts, i_parts = [], []
    for hh in range(xconv.shape[1] // blk):
        xh = xconv[:, hh * blk:(hh + 1) * blk].astype(BF16)
        r_parts.append(jnp.dot(xh, wga_ref[hh], preferred_element_type=F32))
        i_parts.append(jnp.dot(xh, wgx_ref[hh], preferred_element_type=F32))
    r = jax.nn.sigmoid(jnp.concatenate(r_parts, axis=1) + bga_ref[...])
    i = jax.nn.sigmoid(jnp.concatenate(i_parts, axis=1) + bgx_ref[...])

    neg_lam = -lam_ref[...]
    softplus = jnp.maximum(neg_lam, 0.0) + jnp.log1p(jnp.exp(-jnp.abs(neg_lam)))
    a = jnp.exp2((-LRU_C * LOG2_E * softplus) * r)
    h = a * h0_ref[...] + _input_scale(a) * (i * xconv)
    nh_ref[...] = h
    hg_ref[...] = (h * jax.nn.silu(gate_ref[...])).astype(hg_ref.dtype)


def _rglru_step_call(proj, conv_state, h0, conv_w, conv_b, w_ga, b_ga, w_gx, b_gx, lam):
    _, n, c = proj.shape
    heads, blk, _ = w_ga.shape
    cb = _tile(c, max(1024, blk))
    assert cb % blk == 0 and blk % LANES == 0
    ncb = c // cb
    row = lambda a: a.reshape(1, c)
    vec = pl.BlockSpec((1, cb), lambda j: (0, j))
    rows = pl.BlockSpec((n, cb), lambda j: (0, j))
    conv_rows = pl.BlockSpec((n, CONV_W - 1, cb), lambda j: (0, 0, j))
    wspec = pl.BlockSpec((cb // blk, blk, blk), lambda j: (j, 0, 0))
    return pl.pallas_call(
        functools.partial(_rglru_kernel, blk=blk),
        grid=(ncb,),
        in_specs=[pl.BlockSpec((None, n, cb), lambda j: (0, 0, j)), pl.BlockSpec((None, n, cb), lambda j: (1, 0, j)),
                  conv_rows, rows,
                  pl.BlockSpec((CONV_W, cb), lambda j: (0, j)), vec, wspec, vec, wspec, vec, vec],
        out_specs=[rows, conv_rows, rows],
        out_shape=[
            jax.ShapeDtypeStruct((n, c), BF16),
            jax.ShapeDtypeStruct((n, CONV_W - 1, c), F32),
            jax.ShapeDtypeStruct((n, c), F32),
        ],
        compiler_params=_params("parallel"),
        name="rglru_step",
    )(proj, proj, conv_state, h0, conv_w, row(conv_b), w_ga.astype(BF16), row(b_ga),
      w_gx.astype(BF16), row(b_gx), row(lam))


def _scan_rows(a, b, h):
    t, c = a.shape
    first = lax.broadcasted_iota(jnp.int32, (8, c), 0) == 0
    out = []
    for g in range(t // 8):
        ag = a[g * 8:(g + 1) * 8, :]
        bg = b[g * 8:(g + 1) * 8, :]
        bg = jnp.where(first, bg + ag * h, bg)
        ag = jnp.where(first, 0.0, ag)
        for d in (1, 2, 4):
            bg = ag * pltpu.roll(bg, d, axis=0) + bg
            if d < 4:
                ag = ag * pltpu.roll(ag, d, axis=0)
        out.append(bg)
        h = bg[7:8, :]
    return jnp.concatenate(out, axis=0), h


def _a_front_kernel(xn_ref, wx_ref, wg_ref, xs_ref, cs_ref, h0_ref, cw_ref, cb_ref, wga_ref, bga_ref, wgx_ref,
                    bgx_ref, lam_ref, wo_ref, hg_ref, nc_ref, nh_ref, ps_ref, wob_ref, wb_ref, tail_ref, hc_ref,
                    pr_ref, *, tt, sub, blk):
    b = pl.program_id(1)
    ti = pl.program_id(2)
    tail = CONV_W - 1
    n_sub = tt // sub
    cb = cs_ref.shape[1]

    wob_ref[...] = wo_ref[...].astype(BF16)

    @pl.when((b == 0) & (ti == 0))
    def _():
        wb_ref[:, :cb] = wx_ref[...].astype(BF16)
        wb_ref[:, cb:] = wg_ref[...].astype(BF16)
        ps = jnp.dot(xs_ref[...], wb_ref[...], preferred_element_type=F32)
        ps_ref[0] = ps[:, :cb]
        ps_ref[1] = ps[:, cb:]

    @pl.when(ti == 0)
    def _():
        tail_ref[...] = jnp.concatenate([jnp.zeros((8 - tail, cb), F32), cs_ref[...]], axis=0)
        hc_ref[...] = h0_ref[...]

    neg_lam = -lam_ref[...]
    softplus = jnp.maximum(neg_lam, 0.0) + jnp.log1p(jnp.exp(-jnp.abs(neg_lam)))
    log2_a_scale = -LRU_C * LOG2_E * softplus
    prev8 = tail_ref[...]
    h = hc_ref[...]

    def project(k):
        pr_ref[k] = jnp.dot(xn_ref[k * sub:(k + 1) * sub, :], wb_ref[...], preferred_element_type=F32)

    project(0)
    for k in range(n_sub):
        if k + 1 < n_sub:
            project(k + 1)
        xb = pr_ref[k, :, :cb]
        ext = jnp.concatenate([prev8, xb], axis=0)
        prev8 = xb[sub - 8:, :]
        ext1 = pltpu.roll(ext, 1, axis=0)
        pair = ext * cw_ref[1:2, :] + ext1 * cw_ref[0:1, :]
        xconv = (cb_ref[...] + pltpu.roll(pair, 2, axis=0)[8:, :]) + (ext1[8:, :] * cw_ref[2:3, :] + xb * cw_ref[3:4, :])
        r_parts, i_parts = [], []
        for hh in range(cb // blk):
            xh = xconv[:, hh * blk:(hh + 1) * blk].astype(BF16)
            r_parts.append(jnp.dot(xh, wga_ref[hh], preferred_element_type=F32))
            i_parts.append(jnp.dot(xh, wgx_ref[hh], preferred_element_type=F32))
        r = jax.nn.sigmoid(jnp.concatenate(r_parts, axis=1) + bga_ref[...])
        i = jax.nn.sigmoid(jnp.concatenate(i_parts, axis=1) + bgx_ref[...])
        a = jnp.exp2(log2_a_scale * r)
        bb = _input_scale(a) * (i * xconv)
        h_all, h = _scan_rows(a, bb, h)
        hg_ref[k * sub:(k + 1) * sub, :] = (h_all * jax.nn.silu(pr_ref[k, :, cb:])).astype(hg_ref.dtype)
    tail_ref[...] = prev8
    hc_ref[...] = h

    @pl.when(ti == pl.num_programs(2) - 1)
    def _():
        nc_ref[...] = prev8[8 - tail:, :]
        nh_ref[...] = h


def _a_front_call(xn, xs, w_in, w_out, layer, conv_state, h0, conv_w, conv_b, w_ga, b_ga, w_gx, b_gx, lam, n, t):
    d = xn.shape[1]
    c = w_in.shape[2] // 2
    heads, blk, _ = w_ga.shape
    cb = max(512, blk)
    tt, sub = 512, 128
    assert c % cb == 0 and cb % blk == 0 and blk % LANES == 0 and t % tt == 0
    ncb = c // cb
    ns = xs.shape[0]
    nt = t // tt
    slab = _round_slab(w_out.shape[1], ncb * n * nt)
    step = lambda j, b, i: (j * n + b) * nt + i
    row = lambda a: a.reshape(1, c)
    vec = pl.BlockSpec((1, cb), lambda j, b, i: (0, j))
    wspec = pl.BlockSpec((cb // blk, blk, blk), lambda j, b, i: (j, 0, 0))
    hg, nc, nh, ps, w_out_b = pl.pallas_call(
        functools.partial(_a_front_kernel, tt=tt, sub=sub, blk=blk),
        grid=(ncb, n, nt),
        in_specs=[
            pl.BlockSpec((None, tt, d), lambda j, b, i: (b, i, 0)),
            pl.BlockSpec((None, d, cb), lambda j, b, i: (layer, 0, j)),
            pl.BlockSpec((None, d, cb), lambda j, b, i: (layer, 0, ncb + j)),
            pl.BlockSpec((ns, d), lambda j, b, i: (0, 0)),
            pl.BlockSpec((None, CONV_W - 1, cb), lambda j, b, i: (b, 0, j)),
            pl.BlockSpec((None, 1, cb), lambda j, b, i: (b, 0, j)),
            pl.BlockSpec((CONV_W, cb), lambda j, b, i: (0, j)),
            vec, wspec, vec, wspec, vec, vec,
            pl.BlockSpec((None, slab, w_out.shape[2]), lambda j, b, i: (layer, step(j, b, i), 0)),
        ],
        out_specs=[
            pl.BlockSpec((None, tt, cb), lambda j, b, i: (b, i, j)),
            pl.BlockSpec((None, CONV_W - 1, cb), lambda j, b, i: (b, 0, j)),
            pl.BlockSpec((None, 1, cb), lambda j, b, i: (b, 0, j)),
            pl.BlockSpec((2, ns, cb), lambda j, b, i: (0, 0, j)),
            pl.BlockSpec((slab, w_out.shape[2]), lambda j, b, i: (step(j, b, i), 0)),
        ],
        out_shape=[
            jax.ShapeDtypeStruct((n, t, c), BF16),
            jax.ShapeDtypeStruct((n, CONV_W - 1, c), F32),
            jax.ShapeDtypeStruct((n, 1, c), F32),
            jax.ShapeDtypeStruct((2, ns, c), F32),
            jax.ShapeDtypeStruct(w_out.shape[1:], BF16),
        ],
        scratch_shapes=[
            pltpu.VMEM((d, 2 * cb), BF16),
            pltpu.VMEM((8, cb), F32),
            pltpu.VMEM((1, cb), F32),
            pltpu.VMEM((tt // sub, sub, 2 * cb), F32),
        ],
        compiler_params=_params("parallel", "arbitrary", "arbitrary"),
        name="a_in_rglru",
    )(xn.reshape(n, t, d), w_in, w_in, xs, conv_state, h0.reshape(n, 1, c), conv_w, row(conv_b),
      w_ga.astype(BF16), row(b_ga), w_gx.astype(BF16), row(b_gx), row(lam), w_out)
    return hg.reshape(n * t, c), nc, nh.reshape(n, c), ps, w_out_b


def _rel_bucket(dist):
    dist = np.asarray(dist)
    d = np.maximum(dist, 1).astype(np.float32)
    large = MAX_EXACT + (np.log(d / MAX_EXACT) / np.log(MAX_DIST / MAX_EXACT)
                         * (N_BUCKETS - MAX_EXACT)).astype(np.int32)
    large = np.minimum(large, N_BUCKETS - 1)
    return np.where(dist < MAX_EXACT, dist, large).astype(np.int32)


def _prompt_bias(rel_bias):
    span = 3 * BAND - 1
    neg = jnp.full((N_SLOTS, BAND - 1), -jnp.inf, F32)
    tables = []
    for g, r in enumerate(DILATIONS):
        tg = rel_bias[:, g * N_SLOTS:(g + 1) * N_SLOTS].astype(F32)
        tb = tg[_rel_bucket(np.arange(BAND + 1) * r)].T
        wr = jnp.concatenate([neg, tb[:, ::-1], neg, neg[:, :1]], axis=1)
        skew = jnp.tile(wr, (1, BAND))[:, :BAND * span].reshape(N_SLOTS, BAND, span)
        tables.append(skew[:, :, BAND - 1:])
    return jnp.stack(tables)


def _sample_bias(rel_bias):
    tables = []
    for g, r in enumerate(DILATIONS):
        steps = np.concatenate([BAND - np.arange(BAND), np.zeros(BAND, np.int64)])
        tg = rel_bias[:, g * N_SLOTS:(g + 1) * N_SLOTS].astype(F32)
        tables.append(tg[_rel_bucket(steps * r)].T)
    return jnp.stack(tables)


def _attn_prompt_kernel(q_ref, k_ref, v_ref, bias_ref, gate_ref, o_ref, acc_ref, m_ref, l_ref, *, t):
    g = pl.program_id(2)

    def unit(start, gi, r, has_prev):
        nk = 2 * BAND if has_prev else BAND
        kstart = start - BAND * r if has_prev else start
        rows = lambda s0, cnt: pl.ds(s0, cnt, stride=r) if r > 1 else pl.ds(s0, cnt)
        k = k_ref[rows(kstart, nk), :].astype(BF16)
        v = v_ref[rows(kstart, nk), :].astype(BF16)
        q = jnp.concatenate([q_ref[s, rows(start, BAND), :] for s in range(GQA)], axis=0).astype(BF16)
        sc = lax.dot_general(q, k, (((1,), (1,)), ((), ())), preferred_element_type=F32) * SM_SCALE
        sc = sc + bias_ref[:, :, 2 * BAND - nk:].reshape(GQA * BAND, nk)
        m = jnp.max(sc, axis=-1, keepdims=True)
        p = jnp.exp(sc - m)
        l = jnp.sum(p, axis=-1, keepdims=True)
        pv = jnp.dot(p.astype(BF16), v, preferred_element_type=F32)
        lane = lax.broadcasted_iota(jnp.int32, (BAND, LANES), 1)
        m_tile = jnp.zeros((BAND, LANES), F32)
        l_tile = jnp.ones((BAND, LANES), F32)
        for s in range(GQA):
            blk = slice(s * BAND, (s + 1) * BAND)
            acc_ref[gi, s, rows(start, BAND), :] = pv[blk]
            m_tile = jnp.where(lane == s, m[blk], m_tile)
            l_tile = jnp.where(lane == s, l[blk], l_tile)
        m_ref[gi, rows(start, BAND), :] = m_tile
        l_ref[gi, rows(start, BAND), :] = l_tile

    def group_pass(gi, r):
        nb = t // (BAND * r)

        def head_block(cls, carry):
            unit(cls, gi, r, False)
            return carry

        lax.fori_loop(0, r, head_block, 0, unroll=min(r, ATTN_UNROLL))

        def later_block(u, carry):
            cls = u % r
            blk = 1 + u // r
            unit(blk * (BAND * r) + cls, gi, r, True)
            return carry

        if nb > 1:
            lax.fori_loop(0, r * (nb - 1), later_block, 0, unroll=ATTN_UNROLL)

    for gi, r in enumerate(DILATIONS):
        pl.when(g == gi)(functools.partial(group_pass, gi, r))

    @pl.when(g == N_GROUPS - 1)
    def _():
        def combine(c, carry):
            rows = pl.ds(pl.multiple_of(c * BAND, BAND), BAND)
            ms = [m_ref[gi, rows, :] for gi in range(N_GROUPS)]
            mx = functools.reduce(jnp.maximum, ms)
            es = [jnp.exp(mg - mx) for mg in ms]
            den = sum(es[gi] * l_ref[gi, rows, :] for gi in range(N_GROUPS))
            ws = [e / den for e in es]
            for s in range(GQA):
                o = sum(ws[gi][:, s:s + 1] * acc_ref[gi, s, rows, :] for gi in range(N_GROUPS))
                cols = slice(s * HEAD_DIM, (s + 1) * HEAD_DIM)
                o_ref[rows, cols] = (o * jax.nn.silu(gate_ref[rows, cols])).astype(o_ref.dtype)
            return carry

        lax.fori_loop(0, t // BAND, combine, 0, unroll=2)


def _attn_prompt_call(q_heads, kv_heads, bias, gate, n, t):
    assert all(t % (BAND * r) == 0 for r in DILATIONS)
    q5 = q_heads.reshape(N_GROUPS, KV_HEADS, GQA, n, t, HEAD_DIM)
    kv6 = kv_heads.reshape(N_GROUPS, 2, KV_HEADS, n, t, HEAD_DIM)
    gate3 = gate.reshape(n, t, ATTN_W)
    kv_spec = lambda which: pl.BlockSpec((None, None, None, None, t, HEAD_DIM),
                                         lambda b, h, g: (g, which, h, b, 0, 0))
    out = pl.pallas_call(
        functools.partial(_attn_prompt_kernel, t=t),
        grid=(n, KV_HEADS, N_GROUPS),
        in_specs=[
            pl.BlockSpec((None, None, GQA, None, t, HEAD_DIM), lambda b, h, g: (g, h, 0, b, 0, 0)),
            kv_spec(0), kv_spec(1),
            pl.BlockSpec((None, GQA, BAND, 2 * BAND), lambda b, h, g: (g, h, 0, 0)),
            pl.BlockSpec((None, t, GQA * HEAD_DIM), lambda b, h, g: (b, 0, h)),
        ],
        out_specs=pl.BlockSpec((None, t, GQA * HEAD_DIM), lambda b, h, g: (b, 0, h)),
        out_shape=jax.ShapeDtypeStruct((n, t, ATTN_W), BF16),
        scratch_shapes=[pltpu.VMEM((N_GROUPS, GQA, t, HEAD_DIM), F32),
                        pltpu.VMEM((N_GROUPS, t, LANES), F32),
                        pltpu.VMEM((N_GROUPS, t, LANES), F32)],
        compiler_params=_params("parallel", "parallel", "arbitrary"),
        name="attn_prompt",
    )(q5, kv6, kv6, bias, gate3)
    return out.reshape(n * t, ATTN_W)


def _attn_sample_kernel(q_ref, kvn_ref, s0_ref, s1_ref, s2_ref, bias_ref, gate_ref, o_ref):
    st_refs = (s0_ref, s1_ref, s2_ref)
    bf = lambda x: x.astype(BF16)
    rnd = lambda x: x.astype(BF16).astype(F32)
    for h in range(KV_HEADS):
        sl = slice(h * GQA, (h + 1) * GQA)
        s_old, s_new, v_old, v_new = [], [], [], []
        for g in range(N_GROUPS):
            q = q_ref[g * N_SLOTS + h * GQA:g * N_SLOTS + (h + 1) * GQA, :]
            k_old = st_refs[g][:, 0, h, :]
            v_old.append(st_refs[g][:, 1, h, :])
            row = g * 2 * KV_HEADS + h
            k_new = kvn_ref[row:row + 1, :]
            v_new.append(kvn_ref[row + KV_HEADS:row + KV_HEADS + 1, :])
            so = lax.dot_general(bf(q), bf(k_old), (((1,), (1,)), ((), ())), preferred_element_type=F32)
            s_old.append(so * SM_SCALE + bias_ref[g, sl, :BAND])
            sn = jnp.sum(rnd(q) * rnd(k_new), axis=-1, keepdims=True)
            s_new.append(sn * SM_SCALE + bias_ref[g, sl, BAND:BAND + 1])
        m = functools.reduce(jnp.maximum, [jnp.max(x, axis=-1, keepdims=True) for x in s_old] + s_new)
        num = jnp.zeros((GQA, HEAD_DIM), F32)
        den = jnp.zeros((GQA, 1), F32)
        for g in range(N_GROUPS):
            p_old = jnp.exp(s_old[g] - m)
            p_new = jnp.exp(s_new[g] - m)
            den = den + jnp.sum(p_old, axis=-1, keepdims=True) + p_new
            num = num + jnp.dot(bf(p_old), bf(v_old[g]), preferred_element_type=F32) + rnd(p_new) * rnd(v_new[g])
        o_ref[sl, :] = ((num / den) * jax.nn.silu(gate_ref[sl, :])).astype(o_ref.dtype)


def _attn_sample_call(q, kv_new, states, bias, gate):
    n = q.shape[0]
    st_views, st_specs = [], []
    for g, r in enumerate(DILATIONS):
        assert states[g].shape[1] == BAND * r
        st_views.append(states[g].reshape(n, BAND, r, 2, KV_HEADS, HEAD_DIM))
        st_specs.append(pl.BlockSpec((None, BAND, None, 2, KV_HEADS, HEAD_DIM), lambda b: (b, 0, 0, 0, 0, 0)))
    per_b = lambda rows: pl.BlockSpec((None, rows, HEAD_DIM), lambda b: (b, 0, 0))
    out = pl.pallas_call(
        _attn_sample_kernel,
        grid=(n,),
        in_specs=[per_b(N_GROUPS * N_SLOTS), per_b(N_GROUPS * 2 * KV_HEADS)] + st_specs + [
            pl.BlockSpec((N_GROUPS, N_SLOTS, 2 * BAND), lambda b: (0, 0, 0)), per_b(N_SLOTS)],
        out_specs=per_b(N_SLOTS),
        out_shape=jax.ShapeDtypeStruct((n, N_SLOTS, HEAD_DIM), BF16),
        compiler_params=_params("parallel"),
        name="attn_sample",
    )(q.reshape(n, N_GROUPS * N_SLOTS, HEAD_DIM), kv_new.reshape(n, N_GROUPS * 2 * KV_HEADS, HEAD_DIM),
      *st_views, bias, gate.reshape(n, N_SLOTS, HEAD_DIM))
    return out.reshape(n, ATTN_W)


def _trunks(xp, xs, conv_p, h_p, conv_s, h_s, kv_bufs, w):
    n, t, d = xp.shape
    ns = xs.shape[0]
    assert xs.shape[1] == 1
    n_a = w["a_w_in"].shape[0]
    n_b = w["b_w_in"].shape[0]
    xp2 = xp.reshape(n * t, d)
    xs2 = xs.reshape(ns, d)
    (xnp,) = _norm_call(xp2, [w["a_pre_g"][0]])
    (xns,) = _norm_call(xs2, [w["a_pre_g"][0]])
    conv_out_p, h_out_p, conv_out_s, h_out_s = [], [], [], []
    for l in range(n_a):
        mixer = (w["a_conv_w"][l], w["a_conv_b"][l], w["a_w_gate_a"][l], w["a_b_gate_a"][l],
                 w["a_w_gate_x"][l], w["a_b_gate_x"][l], w["a_lambda"][l])
        hgp, nc, nh, proj_s, w_out_b = _a_front_call(xnp, xns, w["a_w_in"], w["a_w_out"], l, conv_p[l], h_p[l],
                                                     *mixer, n, t)
        conv_out_p.append(nc)
        h_out_p.append(nh)
        hgs, nc, nh = _rglru_step_call(proj_s, conv_s[l], h_s[l], *mixer)
        conv_out_s.append(nc)
        h_out_s.append(nh)
        gains = [w["a_pre_g"][l + 1]] if l + 1 < n_a else [w["kv_norm_g"], w["b_pre_g"][0]]
        xp2, xnps, ys = _out_res_call(hgp, hgs, w_out_b, xp2, w["a_post_g"][l], gains,
                                      name="a_out_proj")
        xs2, xnss = _resnorm_call(ys, xs2, w["a_post_g"][l], gains)
        xnp, xns = xnps[0], xnss[0]

    kvp, kvp_heads, kvs = _mm_call(xnp, xns, w["w_kv"], heads=True, name="kv_proj")
    kv_out_p = []
    for g in range(N_GROUPS):
        keep = min(WINDOWS[g], t)
        kept = kvp.reshape(n, t, N_GROUPS * KV_GROUP_W)[:, t - keep:, g * KV_GROUP_W:(g + 1) * KV_GROUP_W]
        kv_out_p.append(kept.reshape(n, keep, 2, KV_HEADS, HEAD_DIM))
    kvs4 = kvs.reshape(ns, 1, N_GROUPS, 2, KV_HEADS, HEAD_DIM)
    kv_out_s = [jnp.concatenate([kv_bufs[g][:, 1:], kvs4[:, :, g]], axis=1) for g in range(N_GROUPS)]
    bias_p = _prompt_bias(w["rel_bias"])
    bias_s = _sample_bias(w["rel_bias"])

    xnp, xns = xnps[1], xnss[1]
    for l in range(n_b):
        gate_p, gate_s, w_out_b = _mm_call(xnp, xns, w["b_w_in"], layer=l, cols=(Q_W, ATTN_W),
                                           rounds=(w["b_w_out"], l), name="gate_proj")
        q_heads, q_s = _mm_call(xnp, xns, w["b_w_in"], layer=l, cols=(0, Q_W), natural=False, heads=True,
                                name="q_proj")
        ogp = _attn_prompt_call(q_heads, kvp_heads, bias_p, gate_p, n, t)
        ogs = _attn_sample_call(q_s, kvs, kv_bufs, bias_s, gate_s)
        gains = [w["b_pre_g"][l + 1]] if l + 1 < n_b else []
        xp2, xnps, ys = _out_res_call(ogp, ogs, w_out_b, xp2, w["b_post_g"][l], gains,
                                      name="b_out_proj")
        xs2, xnss = _resnorm_call(ys, xs2, w["b_post_g"][l], gains)
        if gains:
            xnp, xns = xnps[0], xnss[0]
    prompt_out = (xp2.reshape(n, t, d), jnp.stack(conv_out_p), jnp.stack(h_out_p), *kv_out_p)
    sample_out = (xs2.reshape(ns, 1, d), jnp.stack(conv_out_s), jnp.stack(h_out_s), *kv_out_s)
    return prompt_out, sample_out


def kernel(x_prompt, x_sample, state_conv, state_h, state_kv_w128, state_kv_w512, state_kv_w2048,
           a_pre_g, a_w_in, a_conv_w, a_conv_b, a_w_gate_a, a_b_gate_a, a_w_gate_x, a_b_gate_x,
           a_lambda, a_w_out, a_post_g, kv_norm_g, w_kv, rel_bias, b_pre_g, b_w_in, b_w_out, b_post_g):
    w = dict(
        a_pre_g=a_pre_g, a_w_in=a_w_in, a_conv_w=a_conv_w, a_conv_b=a_conv_b,
        a_w_gate_a=a_w_gate_a, a_b_gate_a=a_b_gate_a, a_w_gate_x=a_w_gate_x, a_b_gate_x=a_b_gate_x,
        a_lambda=a_lambda, a_w_out=a_w_out, a_post_g=a_post_g, kv_norm_g=kv_norm_g,
        w_kv=w_kv, rel_bias=rel_bias, b_pre_g=b_pre_g, b_w_in=b_w_in, b_w_out=b_w_out, b_post_g=b_post_g)
    nb_p = x_prompt.shape[0]
    n_a = a_w_in.shape[0]
    d_rnn = a_conv_w.shape[-1]
    conv0 = jnp.zeros((n_a, nb_p, CONV_W - 1, d_rnn), F32)
    h0 = jnp.zeros((n_a, nb_p, d_rnn), F32)
    prompt_out, sample_out = _trunks(x_prompt, x_sample, conv0, h0, state_conv, state_h,
                                     (state_kv_w128, state_kv_w512, state_kv_w2048), w)
    return (prompt_out[0], sample_out[0], *prompt_out[1:], *sample_out[1:])
```

```python
import functools

import numpy as np
import jax
import jax.numpy as jnp
from jax import lax
from jax.experimental import pallas as pl
from jax.experimental.pallas import tpu as pltpu

F32 = jnp.float32
BF16 = jnp.bfloat16

EPS = 1e-6
LRU_C = 8.0
LOG2_E = 1.4426950408889634
CONV_W = 4
HEAD_DIM = 128
N_SLOTS = 16
KV_HEADS = 4
GQA = N_SLOTS // KV_HEADS
WINDOWS = (128, 512, 2048)
DILATIONS = (1, 4, 16)
N_GROUPS = 3
BAND = 128
ATTN_W = N_SLOTS * HEAD_DIM
Q_W = N_GROUPS * ATTN_W
KV_GROUP_W = 2 * KV_HEADS * HEAD_DIM
N_BUCKETS = 32
MAX_EXACT = 16
MAX_DIST = 2048
SM_SCALE = HEAD_DIM ** -0.5

ATTN_UNROLL = 8
LANES = 128
VMEM_LIMIT = 56 * 1024 * 1024


def _params(*sem):
    return pltpu.CompilerParams(dimension_semantics=sem, vmem_limit_bytes=VMEM_LIMIT)


def _tile(dim, pref):
    if dim <= pref:
        return dim
    t = pref
    while dim % t:
        t //= 2
    return t


def _round_slab(rows, steps):
    slab = rows // steps
    assert slab * steps == rows and slab % 16 == 0
    return slab


def _input_scale(a):
    y = 1.0 - a * a
    return jnp.where(y > 0.0, y * lax.rsqrt(y), 0.0)


def _rms(x, g):
    return x * lax.rsqrt(jnp.mean(x * x, axis=-1, keepdims=True) + EPS) * g


def _norm_kernel(x_ref, g_ref, *o_refs):
    x = x_ref[...]
    xhat = x * lax.rsqrt(jnp.mean(x * x, axis=-1, keepdims=True) + EPS)
    for k, o_ref in enumerate(o_refs):
        o_ref[...] = (xhat * g_ref[k:k + 1, :]).astype(o_ref.dtype)


def _norm_call(x, gains):
    m, d = x.shape
    tm = _tile(m, 256)
    g = jnp.stack(gains)
    ng = len(gains)
    row = pl.BlockSpec((tm, d), lambda i: (i, 0))
    return pl.pallas_call(
        _norm_kernel,
        grid=(m // tm,),
        in_specs=[row, pl.BlockSpec((ng, d), lambda i: (0, 0))],
        out_specs=[row] * ng,
        out_shape=[jax.ShapeDtypeStruct((m, d), BF16)] * ng,
        compiler_params=_params("parallel"),
        name="rmsnorm",
    )(x, g)


def _resnorm_kernel(y_ref, x_ref, gp_ref, g_ref, xo_ref, *o_refs):
    xn = x_ref[...] + _rms(y_ref[...], gp_ref[...])
    xo_ref[...] = xn
    if o_refs:
        xhat = xn * lax.rsqrt(jnp.mean(xn * xn, axis=-1, keepdims=True) + EPS)
        for k, o_ref in enumerate(o_refs):
            o_ref[...] = (xhat * g_ref[k:k + 1, :]).astype(o_ref.dtype)


def _resnorm_call(y, x, g_post, gains):
    m, d = x.shape
    tm = _tile(m, 256)
    ng = len(gains)
    g = jnp.stack(gains) if gains else jnp.ones((1, d), F32)
    row = pl.BlockSpec((tm, d), lambda i: (i, 0))
    outs = pl.pallas_call(
        _resnorm_kernel,
        grid=(m // tm,),
        in_specs=[row, row, pl.BlockSpec((1, d), lambda i: (0, 0)),
                  pl.BlockSpec((g.shape[0], d), lambda i: (0, 0))],
        out_specs=[row] * (1 + ng),
        out_shape=[jax.ShapeDtypeStruct((m, d), F32)] + [jax.ShapeDtypeStruct((m, d), BF16)] * ng,
        compiler_params=_params("parallel"),
        name="residual_rmsnorm",
    )(y, x, g_post.reshape(1, d), g)
    return outs[0], list(outs[1:])


def _mm_kernel(a_ref, w_ref, a2_ref, *refs, natural, heads, rounds):
    if rounds:
        src_ref, *refs = refs
    *o_refs, wb_ref = refs
    if rounds:
        *o_refs, dst_ref = o_refs
        dst_ref[...] = src_ref[...].astype(BF16)
    *o_refs, o2_ref = o_refs

    @pl.when(pl.program_id(1) == 0)
    def _():
        wb_ref[...] = w_ref[...].astype(BF16)
        o2_ref[...] = jnp.dot(a2_ref[...], wb_ref[...], preferred_element_type=F32)

    y = jnp.dot(a_ref[...], wb_ref[...], preferred_element_type=F32)
    k = 0
    if natural:
        o_refs[k][...] = y
        k += 1
    if heads:
        for h in range(y.shape[1] // LANES):
            o_refs[k][h] = y[:, h * LANES:(h + 1) * LANES]


def _mm_call(a, a2, w, *, layer=0, cols=None, natural=True, heads=False, rounds=None, tm=1024, tn=512,
             name="matmul"):
    if w.ndim == 2:
        w = w[None]
    m, kdim = a.shape
    m2 = a2.shape[0]
    col0, n = cols if cols is not None else (0, w.shape[2])
    tm = _tile(m, tm)
    tn = _tile(n, tn)
    assert col0 % tn == 0
    out_specs, out_shape = [], []
    if natural:
        out_specs.append(pl.BlockSpec((tm, tn), lambda j, i: (i, j)))
        out_shape.append(jax.ShapeDtypeStruct((m, n), F32))
    if heads:
        out_specs.append(pl.BlockSpec((tn // LANES, tm, LANES), lambda j, i: (j, i, 0)))
        out_shape.append(jax.ShapeDtypeStruct((n // LANES, m, LANES), F32))
    out_specs.append(pl.BlockSpec((m2, tn), lambda j, i: (0, j)))
    out_shape.append(jax.ShapeDtypeStruct((m2, n), F32))
    n_i = m // tm
    in_specs = [pl.BlockSpec((tm, kdim), lambda j, i: (i, 0)),
                pl.BlockSpec((None, kdim, tn), lambda j, i: (layer, 0, col0 // tn + j)),
                pl.BlockSpec((m2, kdim), lambda j, i: (0, 0))]
    operands = [a, w, a2]
    if rounds is not None:
        w3, r_layer = rounds
        slab = _round_slab(w3.shape[1], (n // tn) * n_i)
        in_specs.append(pl.BlockSpec((None, slab, w3.shape[2]), lambda j, i: (r_layer, j * n_i + i, 0)))
        out_specs.append(pl.BlockSpec((slab, w3.shape[2]), lambda j, i: (j * n_i + i, 0)))
        out_shape.append(jax.ShapeDtypeStruct(w3.shape[1:], BF16))
        operands.append(w3)
    return pl.pallas_call(
        functools.partial(_mm_kernel, natural=natural, heads=heads, rounds=rounds is not None),
        grid=(n // tn, n_i),
        in_specs=in_specs,
        out_specs=out_specs,
        out_shape=out_shape,
        scratch_shapes=[pltpu.VMEM((kdim, tn), BF16)],
        compiler_params=_params("parallel", "arbitrary"),
        name=name,
    )(*operands)


def _lane_fold_sq(v):
    return sum(v[:, s:s + LANES] * v[:, s:s + LANES] for s in range(0, v.shape[1], LANES))


def _times_rows(v, scale):
    return jnp.concatenate([v[:, s:s + LANES] * scale for s in range(0, v.shape[1], LANES)], axis=1)


def _out_res_kernel(a_ref, w_ref, a2_ref, x_ref, gp_ref, g_ref, xo_ref, *refs, n_row, n_gain, width):
    xn_refs = refs[:n_gain]
    o2_ref, y_scr, xnew_scr, ssq_y, ssq_x, sc_y, sc_x = refs[n_gain:]
    i = pl.program_id(0)
    j = pl.program_id(1)
    tn = w_ref.shape[1]
    cols = pl.ds(pl.multiple_of(j * tn, tn), tn)

    @pl.when((i == 0) & (j == 0))
    def _():
        ssq_x[...] = jnp.zeros(ssq_x.shape, F32)
        ssq_y[...] = jnp.zeros(ssq_y.shape, F32)

    @pl.when(j == 0)
    def _():
        for ssq, sc in ((ssq_x, sc_x), (ssq_y, sc_y)):
            mean = jnp.sum(ssq[...], axis=-1, keepdims=True) * (1.0 / width)
            sc[...] = jnp.broadcast_to(lax.rsqrt(mean + EPS), sc.shape)
            ssq[...] = jnp.zeros(ssq.shape, F32)

    if n_gain:
        @pl.when(i >= 2)
        def _():
            xhat = _times_rows(xnew_scr[:, cols], sc_x[...])
            for k in range(n_gain):
                xn_refs[k][...] = (xhat * g_ref[k:k + 1, :]).astype(xn_refs[k].dtype)

    @pl.when((i >= 1) & (i <= n_row))
    def _():
        xnew = x_ref[...] + _times_rows(y_scr[:, cols], sc_y[...]) * gp_ref[...]
        xo_ref[...] = xnew
        xnew_scr[:, cols] = xnew
        ssq_x[...] += _lane_fold_sq(xnew)

    @pl.when(i < n_row)
    def _():
        y = jnp.dot(a_ref[...], w_ref[...], preferred_element_type=F32)
        y_scr[:, cols] = y
        ssq_y[...] += _lane_fold_sq(y)

    @pl.when(i == 0)
    def _():
        o2_ref[...] = jnp.dot(a2_ref[...], w_ref[...], preferred_element_type=F32)


def _out_res_vmem_bytes(tm, tn, kdim, n, ng):
    return 2 * tm * n * 4 + 2 * (tm * kdim * 2 + kdim * tn * 2 + 2 * tm * tn * 4 + ng * tm * tn * 2)


def _out_res_call(a, a2, w, x, g_post, gains, *, tm=512, name="out_proj_residual"):
    m, kdim = a.shape
    n = w.shape[1]
    m2 = a2.shape[0]
    ng = len(gains)
    tn = next(c for c in (1024, 512, 256, 128) if n % c == 0
              and _out_res_vmem_bytes(tm, c, kdim, n, ng) <= VMEM_LIMIT * 4 // 5)
    assert m % tm == 0
    n_row, nj = m // tm, n // tn
    g = jnp.stack(gains) if gains else jnp.ones((1, n), F32)
    last = nj - 1
    lag1 = lambda i, j: (jnp.clip(i - 1, 0, n_row - 1), jnp.where(i < 1, 0, jnp.where(i <= n_row, j, last)))
    lag2 = lambda i, j: (jnp.clip(i - 2, 0, n_row - 1), jnp.where(i < 2, 0, j))
    outs = pl.pallas_call(
        functools.partial(_out_res_kernel, n_row=n_row, n_gain=ng, width=n),
        grid=(n_row + 2, nj),
        in_specs=[
            pl.BlockSpec((tm, kdim), lambda i, j: (jnp.minimum(i, n_row - 1), 0)),
            pl.BlockSpec((kdim, tn), lambda i, j: (0, jnp.where(i < n_row, j, last))),
            pl.BlockSpec((m2, kdim), lambda i, j: (0, 0)),
            pl.BlockSpec((tm, tn), lag1),
            pl.BlockSpec((1, tn), lambda i, j: (0, j)),
            pl.BlockSpec((g.shape[0], tn), lambda i, j: (0, j)),
        ],
        out_specs=[pl.BlockSpec((tm, tn), lag1)] + [pl.BlockSpec((tm, tn), lag2)] * ng + [
            pl.BlockSpec((m2, tn), lambda i, j: (0, jnp.where(i == 0, j, last)))],
        out_shape=[jax.ShapeDtypeStruct((m, n), F32)] + [jax.ShapeDtypeStruct((m, n), BF16)] * ng + [
            jax.ShapeDtypeStruct((m2, n), F32)],
        scratch_shapes=[pltpu.VMEM((tm, n), F32), pltpu.VMEM((tm, n), F32)] + [pltpu.VMEM((tm, LANES), F32)] * 4,
        compiler_params=_params("arbitrary", "arbitrary"),
        name=name,
    )(a, w, a2, x, g_post.reshape(1, n), g)
    return outs[0], list(outs[1:1 + ng]), outs[-1]


def _rglru_kernel(xb_ref, gate_ref, cs_ref, h0_ref, cw_ref, cb_ref, wga_ref, bga_ref, wgx_ref, bgx_ref,
                  lam_ref, hg_ref, nc_ref, nh_ref, *, blk):
    tail = CONV_W - 1
    xb = xb_ref[...]
    xconv = cb_ref[...]
    for k in range(tail):
        xconv = xconv + cs_ref[:, k, :] * cw_ref[k:k + 1, :]
        if k:
            nc_ref[:, k - 1, :] = cs_ref[:, k, :]
    xconv = xconv + xb * cw_ref[tail:tail + 1, :]
    nc_ref[:, tail - 1, :] = xb

    r_parts, i_parts = [], []
    for hh in range(xconv.shape[1] // blk):
        xh = xconv[:, hh * blk:(hh + 1) * blk].astype(BF16)
        r_parts.append(jnp.dot(xh, wga_ref[hh], preferred_element_type=F32))
        i_parts.append(jnp.dot(xh, wgx_ref[hh], preferred_element_type=F32))
    r = jax.nn.sigmoid(jnp.concatenate(r_parts, axis=1) + bga_ref[...])
    i = jax.nn.sigmoid(jnp.concatenate(i_parts, axis=1) + bgx_ref[...])

    neg_lam = -lam_ref[...]
    softplus = jnp.maximum(neg_lam, 0.0) + jnp.log1p(jnp.exp(-jnp.abs(neg_lam)))
    a = jnp.exp2((-LRU_C * LOG2_E * softplus) * r)
    h = a * h0_ref[...] + _input_scale(a) * (i * xconv)
    nh_ref[...] = h
    hg_ref[...] = (h * jax.nn.silu(gate_ref[...])).astype(hg_ref.dtype)


def _rglru_step_call(proj, conv_state, h0, conv_w, conv_b, w_ga, b_ga, w_gx, b_gx, lam):
    _, n, c = proj.shape
    heads, blk, _ = w_ga.shape
    cb = _tile(c, max(1024, blk))
    assert cb % blk == 0 and blk % LANES == 0
    ncb = c // cb
    row = lambda a: a.reshape(1, c)
    vec = pl.BlockSpec((1, cb), lambda j: (0, j))
    rows = pl.BlockSpec((n, cb), lambda j: (0, j))
    conv_rows = pl.BlockSpec((n, CONV_W - 1, cb), lambda j: (0, 0, j))
    wspec = pl.BlockSpec((cb // blk, blk, blk), lambda j: (j, 0, 0))
    return pl.pallas_call(
        functools.partial(_rglru_kernel, blk=blk),
        grid=(ncb,),
        in_specs=[pl.BlockSpec((None, n, cb), lambda j: (0, 0, j)), pl.BlockSpec((None, n, cb), lambda j: (1, 0, j)),
                  conv_rows, rows,
                  pl.BlockSpec((CONV_W, cb), lambda j: (0, j)), vec, wspec, vec, wspec, vec, vec],
        out_specs=[rows, conv_rows, rows],
        out_shape=[
            jax.ShapeDtypeStruct((n, c), BF16),
            jax.ShapeDtypeStruct((n, CONV_W - 1, c), F32),
            jax.ShapeDtypeStruct((n, c), F32),
        ],
        compiler_params=_params("parallel"),
        name="rglru_step",
    )(proj, proj, conv_state, h0, conv_w, row(conv_b), w_ga.astype(BF16), row(b_ga),
      w_gx.astype(BF16), row(b_gx), row(lam))


def _scan_rows(a, b, h):
    t, c = a.shape
    first = lax.broadcasted_iota(jnp.int32, (8, c), 0) == 0
    out = []
    for g in range(t // 8):
        ag = a[g * 8:(g + 1) * 8, :]
        bg = b[g * 8:(g + 1) * 8, :]
        bg = jnp.where(first, bg + ag * h, bg)
        ag = jnp.where(first, 0.0, ag)
        for d in (1, 2, 4):
            bg = ag * pltpu.roll(bg, d, axis=0) + bg
            if d < 4:
                ag = ag * pltpu.roll(ag, d, axis=0)
        out.append(bg)
        h = bg[7:8, :]
    return jnp.concatenate(out, axis=0), h


def _a_front_kernel(xn_ref, wx_ref, wg_ref, xs_ref, cs_ref, h0_ref, cw_ref, cb_ref, wga_ref, bga_ref, wgx_ref,
                    bgx_ref, lam_ref, wo_ref, hg_ref, nc_ref, nh_ref, ps_ref, wob_ref, wb_ref, tail_ref, hc_ref,
                    pr_ref, *, tt, sub, blk):
    b = pl.program_id(1)
    ti = pl.program_id(2)
    tail = CONV_W - 1
    n_sub = tt // sub
    cb = cs_ref.shape[1]

    wob_ref[...] = wo_ref[...].astype(BF16)

    @pl.when((b == 0) & (ti == 0))
    def _():
        wb_ref[:, :cb] = wx_ref[...].astype(BF16)
        wb_ref[:, cb:] = wg_ref[...].astype(BF16)
        ps = jnp.dot(xs_ref[...], wb_ref[...], preferred_element_type=F32)
        ps_ref[0] = ps[:, :cb]
        ps_ref[1] = ps[:, cb:]

    @pl.when(ti == 0)
    def _():
        tail_ref[...] = jnp.concatenate([jnp.zeros((8 - tail, cb), F32), cs_ref[...]], axis=0)
        hc_ref[...] = h0_ref[...]

    neg_lam = -lam_ref[...]
    softplus = jnp.maximum(neg_lam, 0.0) + jnp.log1p(jnp.exp(-jnp.abs(neg_lam)))
    log2_a_scale = -LRU_C * LOG2_E * softplus
    prev8 = tail_ref[...]
    h = hc_ref[...]

    def project(k):
        pr_ref[k] = jnp.dot(xn_ref[k * sub:(k + 1) * sub, :], wb_ref[...], preferred_element_type=F32)

    project(0)
    for k in range(n_sub):
        if k + 1 < n_sub:
            project(k + 1)
        xb = pr_ref[k, :, :cb]
        ext = jnp.concatenate([prev8, xb], axis=0)
        prev8 = xb[sub - 8:, :]
        ext1 = pltpu.roll(ext, 1, axis=0)
        pair = ext * cw_ref[1:2, :] + ext1 * cw_ref[0:1, :]
        xconv = (cb_ref[...] + pltpu.roll(pair, 2, axis=0)[8:, :]) + (ext1[8:, :] * cw_ref[2:3, :] + xb * cw_ref[3:4, :])
        r_parts, i_parts = [], []
        for hh in range(cb // blk):
            xh = xconv[:, hh * blk:(hh + 1) * blk].astype(BF16)
            r_parts.append(jnp.dot(xh, wga_ref[hh], preferred_element_type=F32))
            i_parts.append(jnp.dot(xh, wgx_ref[hh], preferred_element_type=F32))
        r = jax.nn.sigmoid(jnp.concatenate(r_parts, axis=1) + bga_ref[...])
        i = jax.nn.sigmoid(jnp.concatenate(i_parts, axis=1) + bgx_ref[...])
        a = jnp.exp2(log2_a_scale * r)
        bb = _input_scale(a) * (i * xconv)
        h_all, h = _scan_rows(a, bb, h)
        hg_ref[k * sub:(k + 1) * sub, :] = (h_all * jax.nn.silu(pr_ref[k, :, cb:])).astype(hg_ref.dtype)
    tail_ref[...] = prev8
    hc_ref[...] = h

    @pl.when(ti == pl.num_programs(2) - 1)
    def _():
        nc_ref[...] = prev8[8 - tail:, :]
        nh_ref[...] = h


def _a_front_call(xn, xs, w_in, w_out, layer, conv_state, h0, conv_w, conv_b, w_ga, b_ga, w_gx, b_gx, lam, n, t):
    d = xn.shape[1]
    c = w_in.shape[2] // 2
    heads, blk, _ = w_ga.shape
    cb = max(512, blk)
    tt, sub = 512, 128
    assert c % cb == 0 and cb % blk == 0 and blk % LANES == 0 and t % tt == 0
    ncb = c // cb
    ns = xs.shape[0]
    nt = t // tt
    slab = _round_slab(w_out.shape[1], ncb * n * nt)
    step = lambda j, b, i: (j * n + b) * nt + i
    row = lambda a: a.reshape(1, c)
    vec = pl.BlockSpec((1, cb), lambda j, b, i: (0, j))
    wspec = pl.BlockSpec((cb // blk, blk, blk), lambda j, b, i: (j, 0, 0))
    hg, nc, nh, ps, w_out_b = pl.pallas_call(
        functools.partial(_a_front_kernel, tt=tt, sub=sub, blk=blk),
        grid=(ncb, n, nt),
        in_specs=[
            pl.BlockSpec((None, tt, d), lambda j, b, i: (b, i, 0)),
            pl.BlockSpec((None, d, cb), lambda j, b, i: (layer, 0, j)),
            pl.BlockSpec((None, d, cb), lambda j, b, i: (layer, 0, ncb + j)),
            pl.BlockSpec((ns, d), lambda j, b, i: (0, 0)),
            pl.BlockSpec((None, CONV_W - 1, cb), lambda j, b, i: (b, 0, j)),
            pl.BlockSpec((None, 1, cb), lambda j, b, i: (b, 0, j)),
            pl.BlockSpec((CONV_W, cb), lambda j, b, i: (0, j)),
            vec, wspec, vec, wspec, vec, vec,
            pl.BlockSpec((None, slab, w_out.shape[2]), lambda j, b, i: (layer, step(j, b, i), 0)),
        ],
        out_specs=[
            pl.BlockSpec((None, tt, cb), lambda j, b, i: (b, i, j)),
            pl.BlockSpec((None, CONV_W - 1, cb), lambda j, b, i: (b, 0, j)),
            pl.BlockSpec((None, 1, cb), lambda j, b, i: (b, 0, j)),
            pl.BlockSpec((2, ns, cb), lambda j, b, i: (0, 0, j)),
            pl.BlockSpec((slab, w_out.shape[2]), lambda j, b, i: (step(j, b, i), 0)),
        ],
        out_shape=[
            jax.ShapeDtypeStruct((n, t, c), BF16),
            jax.ShapeDtypeStruct((n, CONV_W - 1, c), F32),
            jax.ShapeDtypeStruct((n, 1, c), F32),
            jax.ShapeDtypeStruct((2, ns, c), F32),
            jax.ShapeDtypeStruct(w_out.shape[1:], BF16),
        ],
        scratch_shapes=[
            pltpu.VMEM((d, 2 * cb), BF16),
            pltpu.VMEM((8, cb), F32),
            pltpu.VMEM((1, cb), F32),
            pltpu.VMEM((tt // sub, sub, 2 * cb), F32),
        ],
        compiler_params=_params("parallel", "arbitrary", "arbitrary"),
        name="a_in_rglru",
    )(xn.reshape(n, t, d), w_in, w_in, xs, conv_state, h0.reshape(n, 1, c), conv_w, row(conv_b),
      w_ga.astype(BF16), row(b_ga), w_gx.astype(BF16), row(b_gx), row(lam), w_out)
    return hg.reshape(n * t, c), nc, nh.reshape(n, c), ps, w_out_b


def _rel_bucket(dist):
    dist = np.asarray(dist)
    d = np.maximum(dist, 1).astype(np.float32)
    large = MAX_EXACT + (np.log(d / MAX_EXACT) / np.log(MAX_DIST / MAX_EXACT)
                         * (N_BUCKETS - MAX_EXACT)).astype(np.int32)
    large = np.minimum(large, N_BUCKETS - 1)
    return np.where(dist < MAX_EXACT, dist, large).astype(np.int32)


def _prompt_bias(rel_bias):
    span = 3 * BAND - 1
    neg = jnp.full((N_SLOTS, BAND - 1), -jnp.inf, F32)
    tables = []
    for g, r in enumerate(DILATIONS):
        tg = rel_bias[:, g * N_SLOTS:(g + 1) * N_SLOTS].astype(F32)
        tb = tg[_rel_bucket(np.arange(BAND + 1) * r)].T
        wr = jnp.concatenate([neg, tb[:, ::-1], neg, neg[:, :1]], axis=1)
        skew = jnp.tile(wr, (1, BAND))[:, :BAND * span].reshape(N_SLOTS, BAND, span)
        tables.append(skew[:, :, BAND - 1:])
    return jnp.stack(tables)


def _sample_bias(rel_bias):
    tables = []
    for g, r in enumerate(DILATIONS):
        steps = np.concatenate([BAND - np.arange(BAND), np.zeros(BAND, np.int64)])
        tg = rel_bias[:, g * N_SLOTS:(g + 1) * N_SLOTS].astype(F32)
        tables.append(tg[_rel_bucket(steps * r)].T)
    return jnp.stack(tables)


def _attn_prompt_kernel(q_ref, k_ref, v_ref, bias_ref, gate_ref, o_ref, acc_ref, m_ref, l_ref, *, t):
    g = pl.program_id(2)

    def unit(start, gi, r, has_prev):
        nk = 2 * BAND if has_prev else BAND
        kstart = start - BAND * r if has_prev else start
        rows = lambda s0, cnt: pl.ds(s0, cnt, stride=r) if r > 1 else pl.ds(s0, cnt)
        k = k_ref[rows(kstart, nk), :].astype(BF16)
        v = v_ref[rows(kstart, nk), :].astype(BF16)
        q = jnp.concatenate([q_ref[s, rows(start, BAND), :] for s in range(GQA)], axis=0).astype(BF16)
        sc = lax.dot_general(q, k, (((1,), (1,)), ((), ())), preferred_element_type=F32) * SM_SCALE
        sc = sc + bias_ref[:, :, 2 * BAND - nk:].reshape(GQA * BAND, nk)
        m = jnp.max(sc, axis=-1, keepdims=True)
        p = jnp.exp(sc - m)
        l = jnp.sum(p, axis=-1, keepdims=True)
        pv = jnp.dot(p.astype(BF16), v, preferred_element_type=F32)
        lane = lax.broadcasted_iota(jnp.int32, (BAND, LANES), 1)
        m_tile = jnp.zeros((BAND, LANES), F32)
        l_tile = jnp.ones((BAND, LANES), F32)
        for s in range(GQA):
            blk = slice(s * BAND, (s + 1) * BAND)
            acc_ref[gi, s, rows(start, BAND), :] = pv[blk]
            m_tile = jnp.where(lane == s, m[blk], m_tile)
            l_tile = jnp.where(lane == s, l[blk], l_tile)
        m_ref[gi, rows(start, BAND), :] = m_tile
        l_ref[gi, rows(start, BAND), :] = l_tile

    def group_pass(gi, r):
        nb = t // (BAND * r)

        def head_block(cls, carry):
            unit(cls, gi, r, False)
            return carry

        lax.fori_loop(0, r, head_block, 0, unroll=min(r, ATTN_UNROLL))

        def later_block(u, carry):
            cls = u % r
            blk = 1 + u // r
            unit(blk * (BAND * r) + cls, gi, r, True)
            return carry

        if nb > 1:
            lax.fori_loop(0, r * (nb - 1), later_block, 0, unroll=ATTN_UNROLL)

    for gi, r in enumerate(DILATIONS):
        pl.when(g == gi)(functools.partial(group_pass, gi, r))

    @pl.when(g == N_GROUPS - 1)
    def _():
        def combine(c, carry):
            rows = pl.ds(pl.multiple_of(c * BAND, BAND), BAND)
            ms = [m_ref[gi, rows, :] for gi in range(N_GROUPS)]
            mx = functools.reduce(jnp.maximum, ms)
            es = [jnp.exp(mg - mx) for mg in ms]
            den = sum(es[gi] * l_ref[gi, rows, :] for gi in range(N_GROUPS))
            ws = [e / den for e in es]
            for s in range(GQA):
                o = sum(ws[gi][:, s:s + 1] * acc_ref[gi, s, rows, :] for gi in range(N_GROUPS))
                cols = slice(s * HEAD_DIM, (s + 1) * HEAD_DIM)
                o_ref[rows, cols] = (o * jax.nn.silu(gate_ref[rows, cols])).astype(o_ref.dtype)
            return carry

        lax.fori_loop(0, t // BAND, combine, 0, unroll=2)


def _attn_prompt_call(q_heads, kv_heads, bias, gate, n, t):
    assert all(t % (BAND * r) == 0 for r in DILATIONS)
    q5 = q_heads.reshape(N_GROUPS, KV_HEADS, GQA, n, t, HEAD_DIM)
    kv6 = kv_heads.reshape(N_GROUPS, 2, KV_HEADS, n, t, HEAD_DIM)
    gate3 = gate.reshape(n, t, ATTN_W)
    kv_spec = lambda which: pl.BlockSpec((None, None, None, None, t, HEAD_DIM),
                                         lambda b, h, g: (g, which, h, b, 0, 0))
    out = pl.pallas_call(
        functools.partial(_attn_prompt_kernel, t=t),
        grid=(n, KV_HEADS, N_GROUPS),
        in_specs=[
            pl.BlockSpec((None, None, GQA, None, t, HEAD_DIM), lambda b, h, g: (g, h, 0, b, 0, 0)),
            kv_spec(0), kv_spec(1),
            pl.BlockSpec((None, GQA, BAND, 2 * BAND), lambda b, h, g: (g, h, 0, 0)),
            pl.BlockSpec((None, t, GQA * HEAD_DIM), lambda b, h, g: (b, 0, h)),
        ],
        out_specs=pl.BlockSpec((None, t, GQA * HEAD_DIM), lambda b, h, g: (b, 0, h)),
        out_shape=jax.ShapeDtypeStruct((n, t, ATTN_W), BF16),
        scratch_shapes=[pltpu.VMEM((N_GROUPS, GQA, t, HEAD_DIM), F32),
                        pltpu.VMEM((N_GROUPS, t, LANES), F32),
                        pltpu.VMEM((N_GROUPS, t, LANES), F32)],
        compiler_params=_params("parallel", "parallel", "arbitrary"),
        name="attn_prompt",
    )(q5, kv6, kv6, bias, gate3)
    return out.reshape(n * t, ATTN_W)


def _attn_sample_kernel(q_ref, kvn_ref, s0_ref, s1_ref, s2_ref, bias_ref, gate_ref, o_ref):
    st_refs = (s0_ref, s1_ref, s2_ref)
    bf = lambda x: x.astype(BF16)
    rnd = lambda x: x.astype(BF16).astype(F32)
    for h in range(KV_HEADS):
        sl = slice(h * GQA, (h + 1) * GQA)
        s_old, s_new, v_old, v_new = [], [], [], []
        for g in range(N_GROUPS):
            q = q_ref[g * N_SLOTS + h * GQA:g * N_SLOTS + (h + 1) * GQA, :]
            k_old = st_refs[g][:, 0, h, :]
            v_old.append(st_refs[g][:, 1, h, :])
            row = g * 2 * KV_HEADS + h
            k_new = kvn_ref[row:row + 1, :]
            v_new.append(kvn_ref[row + KV_HEADS:row + KV_HEADS + 1, :])
            so = lax.dot_general(bf(q), bf(k_old), (((1,), (1,)), ((), ())), preferred_element_type=F32)
            s_old.append(so * SM_SCALE + bias_ref[g, sl, :BAND])
            sn = jnp.sum(rnd(q) * rnd(k_new), axis=-1, keepdims=True)
            s_new.append(sn * SM_SCALE + bias_ref[g, sl, BAND:BAND + 1])
        m = functools.reduce(jnp.maximum, [jnp.max(x, axis=-1, keepdims=True) for x in s_old] + s_new)
        num = jnp.zeros((GQA, HEAD_DIM), F32)
        den = jnp.zeros((GQA, 1), F32)
        for g in range(N_GROUPS):
            p_old = jnp.exp(s_old[g] - m)
            p_new = jnp.exp(s_new[g] - m)
            den = den + jnp.sum(p_old, axis=-1, keepdims=True) + p_new
            num = num + jnp.dot(bf(p_old), bf(v_old[g]), preferred_element_type=F32) + rnd(p_new) * rnd(v_new[g])
        o_ref[sl, :] = ((num / den) * jax.nn.silu(gate_ref[sl, :])).astype(o_ref.dtype)


def _attn_sample_call(q, kv_new, states, bias, gate):
    n = q.shape[0]
    st_views, st_specs = [], []
    for g, r in enumerate(DILATIONS):
        assert states[g].shape[1] == BAND * r
        st_views.append(states[g].reshape(n, BAND, r, 2, KV_HEADS, HEAD_DIM))
        st_specs.append(pl.BlockSpec((None, BAND, None, 2, KV_HEADS, HEAD_DIM), lambda b: (b, 0, 0, 0, 0, 0)))
    per_b = lambda rows: pl.BlockSpec((None, rows, HEAD_DIM), lambda b: (b, 0, 0))
    out = pl.pallas_call(
        _attn_sample_kernel,
        grid=(n,),
        in_specs=[per_b(N_GROUPS * N_SLOTS), per_b(N_GROUPS * 2 * KV_HEADS)] + st_specs + [
            pl.BlockSpec((N_GROUPS, N_SLOTS, 2 * BAND), lambda b: (0, 0, 0)), per_b(N_SLOTS)],
        out_specs=per_b(N_SLOTS),
        out_shape=jax.ShapeDtypeStruct((n, N_SLOTS, HEAD_DIM), BF16),
        compiler_params=_params("parallel"),
        name="attn_sample",
    )(q.reshape(n, N_GROUPS * N_SLOTS, HEAD_DIM), kv_new.reshape(n, N_GROUPS * 2 * KV_HEADS, HEAD_DIM),
      *st_views, bias, gate.reshape(n, N_SLOTS, HEAD_DIM))
    return out.reshape(n, ATTN_W)


def _roll_kernel(cur_ref, nxt_ref, new_ref, o_ref):
    tb = cur_ref.shape[0]
    last_block = pl.program_id(1) == pl.num_programs(1) - 1
    o_ref[0:tb - 1] = cur_ref[1:tb]
    o_ref[tb - 1:tb] = jnp.where(last_block, new_ref[...], nxt_ref[...])


def _roll_window_call(state, new):
    n, w = state.shape[:2]
    tb = _tile(w, 512)
    tail = state.shape[2:]
    zeros = (0,) * len(tail)
    blk = pl.BlockSpec((None, tb) + tail, lambda b, i: (b, i) + zeros)
    one = lambda index: pl.BlockSpec((None, 1) + tail, index)
    return pl.pallas_call(
        _roll_kernel,
        grid=(n, w // tb),
        in_specs=[blk, one(lambda b, i: (b, jnp.minimum((i + 1) * tb, w - 1)) + zeros), one(lambda b, i: (b, 0) + zeros)],
        out_specs=blk,
        out_shape=jax.ShapeDtypeStruct(state.shape, state.dtype),
        compiler_params=_params("parallel", "arbitrary"),
        name="roll_window",
    )(state, state, new)


def _trunks(xp, xs, conv_p, h_p, conv_s, h_s, kv_bufs, w):
    n, t, d = xp.shape
    ns = xs.shape[0]
    assert xs.shape[1] == 1
    n_a = w["a_w_in"].shape[0]
    n_b = w["b_w_in"].shape[0]
    xp2 = xp.reshape(n * t, d)
    xs2 = xs.reshape(ns, d)
    (xnp,) = _norm_call(xp2, [w["a_pre_g"][0]])
    (xns,) = _norm_call(xs2, [w["a_pre_g"][0]])
    conv_out_p, h_out_p, conv_out_s, h_out_s = [], [], [], []
    for l in range(n_a):
        mixer = (w["a_conv_w"][l], w["a_conv_b"][l], w["a_w_gate_a"][l], w["a_b_gate_a"][l],
                 w["a_w_gate_x"][l], w["a_b_gate_x"][l], w["a_lambda"][l])
        hgp, nc, nh, proj_s, w_out_b = _a_front_call(xnp, xns, w["a_w_in"], w["a_w_out"], l, conv_p[l], h_p[l],
                                                     *mixer, n, t)
        conv_out_p.append(nc)
        h_out_p.append(nh)
        hgs, nc, nh = _rglru_step_call(proj_s, conv_s[l], h_s[l], *mixer)
        conv_out_s.append(nc)
        h_out_s.append(nh)
        gains = [w["a_pre_g"][l + 1]] if l + 1 < n_a else [w["kv_norm_g"], w["b_pre_g"][0]]
        xp2, xnps, ys = _out_res_call(hgp, hgs, w_out_b, xp2, w["a_post_g"][l], gains,
                                      name="a_out_proj")
        xs2, xnss = _resnorm_call(ys, xs2, w["a_post_g"][l], gains)
        xnp, xns = xnps[0], xnss[0]

    kvp, kvp_heads, kvs = _mm_call(xnp, xns, w["w_kv"], heads=True, name="kv_proj")
    kv_out_p = []
    for g in range(N_GROUPS):
        keep = min(WINDOWS[g], t)
        kept = kvp.reshape(n, t, N_GROUPS * KV_GROUP_W)[:, t - keep:, g * KV_GROUP_W:(g + 1) * KV_GROUP_W]
        kv_out_p.append(kept.reshape(n, keep, 2, KV_HEADS, HEAD_DIM))
    kvs4 = kvs.reshape(ns, 1, N_GROUPS, 2, KV_HEADS, HEAD_DIM)
    kv_out_s = [_roll_window_call(kv_bufs[g], kvs4[:, :, g]) for g in range(N_GROUPS)]
    bias_p = _prompt_bias(w["rel_bias"])
    bias_s = _sample_bias(w["rel_bias"])

    xnp, xns = xnps[1], xnss[1]
    for l in range(n_b):
        gate_p, gate_s, w_out_b = _mm_call(xnp, xns, w["b_w_in"], layer=l, cols=(Q_W, ATTN_W),
                                           rounds=(w["b_w_out"], l), name="gate_proj")
        q_heads, q_s = _mm_call(xnp, xns, w["b_w_in"], layer=l, cols=(0, Q_W), natural=False, heads=True,
                                name="q_proj")
        ogp = _attn_prompt_call(q_heads, kvp_heads, bias_p, gate_p, n, t)
        ogs = _attn_sample_call(q_s, kvs, kv_bufs, bias_s, gate_s)
        gains = [w["b_pre_g"][l + 1]] if l + 1 < n_b else []
        xp2, xnps, ys = _out_res_call(ogp, ogs, w_out_b, xp2, w["b_post_g"][l], gains,
                                      name="b_out_proj")
        xs2, xnss = _resnorm_call(ys, xs2, w["b_post_g"][l], gains)
        if gains:
            xnp, xns = xnps[0], xnss[0]
    prompt_out = (xp2.reshape(n, t, d), jnp.stack(conv_out_p), jnp.stack(h_out_p), *kv_out_p)
    sample_out = (xs2.reshape(ns, 1, d), jnp.stack(conv_out_s), jnp.stack(h_out_s), *kv_out_s)
    return prompt_out, sample_out


def kernel(x_prompt, x_sample, state_conv, state_h, state_kv_w128, state_kv_w512, state_kv_w2048,
           a_pre_g, a_w_in, a_conv_w, a_conv_b, a_w_gate_a, a_b_gate_a, a_w_gate_x, a_b_gate_x,
           a_lambda, a_w_out, a_post_g, kv_norm_g, w_kv, rel_bias, b_pre_g, b_w_in, b_w_out, b_post_g):
    w = dict(
        a_pre_g=a_pre_g, a_w_in=a_w_in, a_conv_w=a_conv_w, a_conv_b=a_conv_b,
        a_w_gate_a=a_w_gate_a, a_b_gate_a=a_b_gate_a, a_w_gate_x=a_w_gate_x, a_b_gate_x=a_b_gate_x,
        a_lambda=a_lambda, a_w_out=a_w_out, a_post_g=a_post_g, kv_norm_g=kv_norm_g,
        w_kv=w_kv, rel_bias=rel_bias, b_pre_g=b_pre_g, b_w_in=b_w_in, b_w_out=b_w_out, b_post_g=b_post_g)
    nb_p = x_prompt.shape[0]
    n_a = a_w_in.shape[0]
    d_rnn = a_conv_w.shape[-1]
    conv0 = jnp.zeros((n_a, nb_p, CONV_W - 1, d_rnn), F32)
    h0 = jnp.zeros((n_a, nb_p, d_rnn), F32)
    prompt_out, sample_out = _trunks(x_prompt, x_sample, conv0, h0, state_conv, state_h,
                                     (state_kv_w128, state_kv_w512, state_kv_w2048), w)
    return (prompt_out[0], sample_out[0], *prompt_out[1:], *sample_out[1:])
```

```python
import functools

import numpy as np
import jax
import jax.numpy as jnp
from jax import lax
from jax.experimental import pallas as pl
from jax.experimental.pallas import tpu as pltpu

F32 = jnp.float32
BF16 = jnp.bfloat16

EPS = 1e-6
LRU_C = 8.0
LOG2_E = 1.4426950408889634
CONV_W = 4
HEAD_DIM = 128
N_SLOTS = 16
KV_HEADS = 4
GQA = N_SLOTS // KV_HEADS
WINDOWS = (128, 512, 2048)
DILATIONS = (1, 4, 16)
N_GROUPS = 3
BAND = 128
ATTN_W = N_SLOTS * HEAD_DIM
Q_W = N_GROUPS * ATTN_W
KV_GROUP_W = 2 * KV_HEADS * HEAD_DIM
N_BUCKETS = 32
MAX_EXACT = 16
MAX_DIST = 2048
SM_SCALE = HEAD_DIM ** -0.5

ATTN_UNROLL = 8
LANES = 128
VMEM_LIMIT = 56 * 1024 * 1024


def _params(*sem):
    return pltpu.CompilerParams(dimension_semantics=sem, vmem_limit_bytes=VMEM_LIMIT)


def _tile(dim, pref):
    if dim <= pref:
        return dim
    t = pref
    while dim % t:
        t //= 2
    return t


def _round_slab(rows, steps):
    slab = rows // steps
    assert slab * steps == rows and slab % 16 == 0
    return slab


def _input_scale(a):
    y = 1.0 - a * a
    return jnp.where(y > 0.0, y * lax.rsqrt(y), 0.0)


def _rms(x, g):
    return x * lax.rsqrt(jnp.mean(x * x, axis=-1, keepdims=True) + EPS) * g


def _norm_kernel(x_ref, g_ref, *o_refs):
    x = x_ref[...]
    xhat = x * lax.rsqrt(jnp.mean(x * x, axis=-1, keepdims=True) + EPS)
    for k, o_ref in enumerate(o_refs):
        o_ref[...] = (xhat * g_ref[k:k + 1, :]).astype(o_ref.dtype)


def _norm_call(x, gains):
    m, d = x.shape
    tm = _tile(m, 256)
    g = jnp.stack(gains)
    ng = len(gains)
    row = pl.BlockSpec((tm, d), lambda i: (i, 0))
    return pl.pallas_call(
        _norm_kernel,
        grid=(m // tm,),
        in_specs=[row, pl.BlockSpec((ng, d), lambda i: (0, 0))],
        out_specs=[row] * ng,
        out_shape=[jax.ShapeDtypeStruct((m, d), BF16)] * ng,
        compiler_params=_params("parallel"),
        name="rmsnorm",
    )(x, g)


def _resnorm_kernel(y_ref, x_ref, gp_ref, g_ref, xo_ref, *o_refs):
    xn = x_ref[...] + _rms(y_ref[...], gp_ref[...])
    xo_ref[...] = xn
    if o_refs:
        xhat = xn * lax.rsqrt(jnp.mean(xn * xn, axis=-1, keepdims=True) + EPS)
        for k, o_ref in enumerate(o_refs):
            o_ref[...] = (xhat * g_ref[k:k + 1, :]).astype(o_ref.dtype)


def _resnorm_call(y, x, g_post, gains):
    m, d = x.shape
    tm = _tile(m, 256)
    ng = len(gains)
    g = jnp.stack(gains) if gains else jnp.ones((1, d), F32)
    row = pl.BlockSpec((tm, d), lambda i: (i, 0))
    outs = pl.pallas_call(
        _resnorm_kernel,
        grid=(m // tm,),
        in_specs=[row, row, pl.BlockSpec((1, d), lambda i: (0, 0)),
                  pl.BlockSpec((g.shape[0], d), lambda i: (0, 0))],
        out_specs=[row] * (1 + ng),
        out_shape=[jax.ShapeDtypeStruct((m, d), F32)] + [jax.ShapeDtypeStruct((m, d), BF16)] * ng,
        compiler_params=_params("parallel"),
        name="residual_rmsnorm",
    )(y, x, g_post.reshape(1, d), g)
    return outs[0], list(outs[1:])


def _mm_kernel(a_ref, w_ref, a2_ref, *refs, natural, heads, rounds):
    if rounds:
        src_ref, *refs = refs
    *o_refs, wb_ref = refs
    if rounds:
        *o_refs, dst_ref = o_refs
        dst_ref[...] = src_ref[...].astype(BF16)
    *o_refs, o2_ref = o_refs

    @pl.when(pl.program_id(1) == 0)
    def _():
        wb_ref[...] = w_ref[...].astype(BF16)
        o2_ref[...] = jnp.dot(a2_ref[...], wb_ref[...], preferred_element_type=F32)

    y = jnp.dot(a_ref[...], wb_ref[...], preferred_element_type=F32)
    k = 0
    if natural:
        o_refs[k][...] = y
        k += 1
    if heads:
        for h in range(y.shape[1] // LANES):
            o_refs[k][h] = y[:, h * LANES:(h + 1) * LANES]


def _mm_call(a, a2, w, *, layer=0, cols=None, natural=True, heads=False, rounds=None, tm=1024, tn=512,
             name="matmul"):
    if w.ndim == 2:
        w = w[None]
    m, kdim = a.shape
    m2 = a2.shape[0]
    col0, n = cols if cols is not None else (0, w.shape[2])
    tm = _tile(m, tm)
    tn = _tile(n, tn)
    assert col0 % tn == 0
    out_specs, out_shape = [], []
    if natural:
        out_specs.append(pl.BlockSpec((tm, tn), lambda j, i: (i, j)))
        out_shape.append(jax.ShapeDtypeStruct((m, n), F32))
    if heads:
        out_specs.append(pl.BlockSpec((tn // LANES, tm, LANES), lambda j, i: (j, i, 0)))
        out_shape.append(jax.ShapeDtypeStruct((n // LANES, m, LANES), F32))
    out_specs.append(pl.BlockSpec((m2, tn), lambda j, i: (0, j)))
    out_shape.append(jax.ShapeDtypeStruct((m2, n), F32))
    n_i = m // tm
    in_specs = [pl.BlockSpec((tm, kdim), lambda j, i: (i, 0)),
                pl.BlockSpec((None, kdim, tn), lambda j, i: (layer, 0, col0 // tn + j)),
                pl.BlockSpec((m2, kdim), lambda j, i: (0, 0))]
    operands = [a, w, a2]
    if rounds is not None:
        w3, r_layer = rounds
        slab = _round_slab(w3.shape[1], (n // tn) * n_i)
        in_specs.append(pl.BlockSpec((None, slab, w3.shape[2]), lambda j, i: (r_layer, j * n_i + i, 0)))
        out_specs.append(pl.BlockSpec((slab, w3.shape[2]), lambda j, i: (j * n_i + i, 0)))
        out_shape.append(jax.ShapeDtypeStruct(w3.shape[1:], BF16))
        operands.append(w3)
    return pl.pallas_call(
        functools.partial(_mm_kernel, natural=natural, heads=heads, rounds=rounds is not None),
        grid=(n // tn, n_i),
        in_specs=in_specs,
        out_specs=out_specs,
        out_shape=out_shape,
        scratch_shapes=[pltpu.VMEM((kdim, tn), BF16)],
        compiler_params=_params("parallel", "arbitrary"),
        name=name,
    )(*operands)


def _lane_fold_sq(v):
    return sum(v[:, s:s + LANES] * v[:, s:s + LANES] for s in range(0, v.shape[1], LANES))


def _times_rows(v, scale):
    return jnp.concatenate([v[:, s:s + LANES] * scale for s in range(0, v.shape[1], LANES)], axis=1)


def _out_res_kernel(a_ref, w_ref, a2_ref, x_ref, gp_ref, g_ref, xo_ref, *refs, n_row, n_gain, width):
    xn_refs = refs[:n_gain]
    o2_ref, y_scr, xnew_scr, ssq_y, ssq_x, sc_y, sc_x = refs[n_gain:]
    i = pl.program_id(0)
    j = pl.program_id(1)
    tn = w_ref.shape[1]
    cols = pl.ds(pl.multiple_of(j * tn, tn), tn)

    @pl.when((i == 0) & (j == 0))
    def _():
        ssq_x[...] = jnp.zeros(ssq_x.shape, F32)
        ssq_y[...] = jnp.zeros(ssq_y.shape, F32)

    @pl.when(j == 0)
    def _():
        for ssq, sc in ((ssq_x, sc_x), (ssq_y, sc_y)):
            mean = jnp.sum(ssq[...], axis=-1, keepdims=True) * (1.0 / width)
            sc[...] = jnp.broadcast_to(lax.rsqrt(mean + EPS), sc.shape)
            ssq[...] = jnp.zeros(ssq.shape, F32)

    if n_gain:
        @pl.when(i >= 2)
        def _():
            xhat = _times_rows(xnew_scr[:, cols], sc_x[...])
            for k in range(n_gain):
                xn_refs[k][...] = (xhat * g_ref[k:k + 1, :]).astype(xn_refs[k].dtype)

    @pl.when((i >= 1) & (i <= n_row))
    def _():
        xnew = x_ref[...] + _times_rows(y_scr[:, cols], sc_y[...]) * gp_ref[...]
        xo_ref[...] = xnew
        xnew_scr[:, cols] = xnew
        ssq_x[...] += _lane_fold_sq(xnew)

    @pl.when(i < n_row)
    def _():
        y = jnp.dot(a_ref[...], w_ref[...], preferred_element_type=F32)
        y_scr[:, cols] = y
        ssq_y[...] += _lane_fold_sq(y)

    @pl.when(i == 0)
    def _():
        o2_ref[...] = jnp.dot(a2_ref[...], w_ref[...], preferred_element_type=F32)


def _out_res_vmem_bytes(tm, tn, kdim, n, ng):
    return 2 * tm * n * 4 + 2 * (tm * kdim * 2 + kdim * tn * 2 + 2 * tm * tn * 4 + ng * tm * tn * 2)


def _out_res_call(a, a2, w, x, g_post, gains, *, tm=512, name="out_proj_residual"):
    m, kdim = a.shape
    n = w.shape[1]
    m2 = a2.shape[0]
    ng = len(gains)
    tn = next(c for c in (1024, 512, 256, 128) if n % c == 0
              and _out_res_vmem_bytes(tm, c, kdim, n, ng) <= VMEM_LIMIT * 4 // 5)
    assert m % tm == 0
    n_row, nj = m // tm, n // tn
    g = jnp.stack(gains) if gains else jnp.ones((1, n), F32)
    last = nj - 1
    lag1 = lambda i, j: (jnp.clip(i - 1, 0, n_row - 1), jnp.where(i < 1, 0, jnp.where(i <= n_row, j, last)))
    lag2 = lambda i, j: (jnp.clip(i - 2, 0, n_row - 1), jnp.where(i < 2, 0, j))
    outs = pl.pallas_call(
        functools.partial(_out_res_kernel, n_row=n_row, n_gain=ng, width=n),
        grid=(n_row + 2, nj),
        in_specs=[
            pl.BlockSpec((tm, kdim), lambda i, j: (jnp.minimum(i, n_row - 1), 0)),
            pl.BlockSpec((kdim, tn), lambda i, j: (0, jnp.where(i < n_row, j, last))),
            pl.BlockSpec((m2, kdim), lambda i, j: (0, 0)),
            pl.BlockSpec((tm, tn), lag1),
            pl.BlockSpec((1, tn), lambda i, j: (0, j)),
            pl.BlockSpec((g.shape[0], tn), lambda i, j: (0, j)),
        ],
        out_specs=[pl.BlockSpec((tm, tn), lag1)] + [pl.BlockSpec((tm, tn), lag2)] * ng + [
            pl.BlockSpec((m2, tn), lambda i, j: (0, jnp.where(i == 0, j, last)))],
        out_shape=[jax.ShapeDtypeStruct((m, n), F32)] + [jax.ShapeDtypeStruct((m, n), BF16)] * ng + [
            jax.ShapeDtypeStruct((m2, n), F32)],
        scratch_shapes=[pltpu.VMEM((tm, n), F32), pltpu.VMEM((tm, n), F32)] + [pltpu.VMEM((tm, LANES), F32)] * 4,
        compiler_params=_params("arbitrary", "arbitrary"),
        name=name,
    )(a, w, a2, x, g_post.reshape(1, n), g)
    return outs[0], list(outs[1:1 + ng]), outs[-1]


def _rglru_kernel(xb_ref, gate_ref, cs_ref, h0_ref, cw_ref, cb_ref, wga_ref, bga_ref, wgx_ref, bgx_ref,
                  lam_ref, hg_ref, nc_ref, nh_ref, *, blk):
    tail = CONV_W - 1
    xb = xb_ref[...]
    xconv = cb_ref[...]
    for k in range(tail):
        xconv = xconv + cs_ref[:, k, :] * cw_ref[k:k + 1, :]
        if k:
            nc_ref[:, k - 1, :] = cs_ref[:, k, :]
    xconv = xconv + xb * cw_ref[tail:tail + 1, :]
    nc_ref[:, tail - 1, :] = xb

    r_parts, i_parts = [], []
    for hh in range(xconv.shape[1] // blk):
        xh = xconv[:, hh * blk:(hh + 1) * blk].astype(BF16)
        r_parts.append(jnp.dot(xh, wga_ref[hh], preferred_element_type=F32))
        i_parts.append(jnp.dot(xh, wgx_ref[hh], preferred_element_type=F32))
    r = jax.nn.sigmoid(jnp.concatenate(r_parts, axis=1) + bga_ref[...])
    i = jax.nn.sigmoid(jnp.concatenate(i_parts, axis=1) + bgx_ref[...])

    neg_lam = -lam_ref[...]
    softplus = jnp.maximum(neg_lam, 0.0) + jnp.log1p(jnp.exp(-jnp.abs(neg_lam)))
    a = jnp.exp2((-LRU_C * LOG2_E * softplus) * r)
    h = a * h0_ref[...] + _input_scale(a) * (i * xconv)
    nh_ref[...] = h
    hg_ref[...] = (h * jax.nn.silu(gate_ref[...])).astype(hg_ref.dtype)


def _rglru_step_call(proj, conv_state, h0, conv_w, conv_b, w_ga, b_ga, w_gx, b_gx, lam):
    _, n, c = proj.shape
    heads, blk, _ = w_ga.shape
    cb = _tile(c, max(1024, blk))
    assert cb % blk == 0 and blk % LANES == 0
    ncb = c // cb
    row = lambda a: a.reshape(1, c)
    vec = pl.BlockSpec((1, cb), lambda j: (0, j))
    rows = pl.BlockSpec((n, cb), lambda j: (0, j))
    conv_rows = pl.BlockSpec((n, CONV_W - 1, cb), lambda j: (0, 0, j))
    wspec = pl.BlockSpec((cb // blk, blk, blk), lambda j: (j, 0, 0))
    return pl.pallas_call(
        functools.partial(_rglru_kernel, blk=blk),
        grid=(ncb,),
        in_specs=[pl.BlockSpec((None, n, cb), lambda j: (0, 0, j)), pl.BlockSpec((None, n, cb), lambda j: (1, 0, j)),
                  conv_rows, rows,
                  pl.BlockSpec((CONV_W, cb), lambda j: (0, j)), vec, wspec, vec, wspec, vec, vec],
        out_specs=[rows, conv_rows, rows],
        out_shape=[
            jax.ShapeDtypeStruct((n, c), BF16),
            jax.ShapeDtypeStruct((n, CONV_W - 1, c), F32),
            jax.ShapeDtypeStruct((n, c), F32),
        ],
        compiler_params=_params("parallel"),
        name="rglru_step",
    )(proj, proj, conv_state, h0, conv_w, row(conv_b), w_ga.astype(BF16), row(b_ga),
      w_gx.astype(BF16), row(b_gx), row(lam))


def _scan_rows(a, b, h):
    t, c = a.shape
    first = lax.broadcasted_iota(jnp.int32, (8, c), 0) == 0
    out = []
    for g in range(t // 8):
        ag = a[g * 8:(g + 1) * 8, :]
        bg = b[g * 8:(g + 1) * 8, :]
        bg = jnp.where(first, bg + ag * h, bg)
        ag = jnp.where(first, 0.0, ag)
        for d in (1, 2, 4):
            bg = ag * pltpu.roll(bg, d, axis=0) + bg
            if d < 4:
                ag = ag * pltpu.roll(ag, d, axis=0)
        out.append(bg)
        h = bg[7:8, :]
    return jnp.concatenate(out, axis=0), h


def _a_front_kernel(xn_ref, wx_ref, wg_ref, xs_ref, cs_ref, h0_ref, cw_ref, cb_ref, wga_ref, bga_ref, wgx_ref,
                    bgx_ref, lam_ref, wo_ref, hg_ref, nc_ref, nh_ref, ps_ref, wob_ref, wb_ref, tail_ref, hc_ref,
                    pr_ref, *, tt, sub, blk):
    b = pl.program_id(1)
    ti = pl.program_id(2)
    tail = CONV_W - 1
    n_sub = tt // sub
    cb = cs_ref.shape[1]

    wob_ref[...] = wo_ref[...].astype(BF16)

    @pl.when((b == 0) & (ti == 0))
    def _():
        wb_ref[:, :cb] = wx_ref[...].astype(BF16)
        wb_ref[:, cb:] = wg_ref[...].astype(BF16)
        ps = jnp.dot(xs_ref[...], wb_ref[...], preferred_element_type=F32)
        ps_ref[0] = ps[:, :cb]
        ps_ref[1] = ps[:, cb:]

    @pl.when(ti == 0)
    def _():
        tail_ref[...] = jnp.concatenate([jnp.zeros((8 - tail, cb), F32), cs_ref[...]], axis=0)
        hc_ref[...] = h0_ref[...]

    neg_lam = -lam_ref[...]
    softplus = jnp.maximum(neg_lam, 0.0) + jnp.log1p(jnp.exp(-jnp.abs(neg_lam)))
    log2_a_scale = -LRU_C * LOG2_E * softplus
    prev8 = tail_ref[...]
    h = hc_ref[...]

    def project(k):
        pr_ref[k] = jnp.dot(xn_ref[k * sub:(k + 1) * sub, :], wb_ref[...], preferred_element_type=F32)

    project(0)
    for k in range(n_sub):
        if k + 1 < n_sub:
            project(k + 1)
        xb = pr_ref[k, :, :cb]
        ext = jnp.concatenate([prev8, xb], axis=0)
        prev8 = xb[sub - 8:, :]
        ext1 = pltpu.roll(ext, 1, axis=0)
        pair = ext * cw_ref[1:2, :] + ext1 * cw_ref[0:1, :]
        xconv = (cb_ref[...] + pltpu.roll(pair, 2, axis=0)[8:, :]) + (ext1[8:, :] * cw_ref[2:3, :] + xb * cw_ref[3:4, :])
        r_parts, i_parts = [], []
        for hh in range(cb // blk):
            xh = xconv[:, hh * blk:(hh + 1) * blk].astype(BF16)
            r_parts.append(jnp.dot(xh, wga_ref[hh], preferred_element_type=F32))
            i_parts.append(jnp.dot(xh, wgx_ref[hh], preferred_element_type=F32))
        r = jax.nn.sigmoid(jnp.concatenate(r_parts, axis=1) + bga_ref[...])
        i = jax.nn.sigmoid(jnp.concatenate(i_parts, axis=1) + bgx_ref[...])
        a = jnp.exp2(log2_a_scale * r)
        bb = _input_scale(a) * (i * xconv)
        h_all, h = _scan_rows(a, bb, h)
        hg_ref[k * sub:(k + 1) * sub, :] = (h_all * jax.nn.silu(pr_ref[k, :, cb:])).astype(hg_ref.dtype)
    tail_ref[...] = prev8
    hc_ref[...] = h

    @pl.when(ti == pl.num_programs(2) - 1)
    def _():
        nc_ref[...] = prev8[8 - tail:, :]
        nh_ref[...] = h


def _a_front_call(xn, xs, w_in, w_out, layer, conv_state, h0, conv_w, conv_b, w_ga, b_ga, w_gx, b_gx, lam, n, t):
    d = xn.shape[1]
    c = w_in.shape[2] // 2
    heads, blk, _ = w_ga.shape
    cb = max(512, blk)
    tt, sub = 512, 128
    assert c % cb == 0 and cb % blk == 0 and blk % LANES == 0 and t % tt == 0
    ncb = c // cb
    ns = xs.shape[0]
    nt = t // tt
    slab = _round_slab(w_out.shape[1], ncb * n * nt)
    step = lambda j, b, i: (j * n + b) * nt + i
    row = lambda a: a.reshape(1, c)
    vec = pl.BlockSpec((1, cb), lambda j, b, i: (0, j))
    wspec = pl.BlockSpec((cb // blk, blk, blk), lambda j, b, i: (j, 0, 0))
    hg, nc, nh, ps, w_out_b = pl.pallas_call(
        functools.partial(_a_front_kernel, tt=tt, sub=sub, blk=blk),
        grid=(ncb, n, nt),
        in_specs=[
            pl.BlockSpec((None, tt, d), lambda j, b, i: (b, i, 0)),
            pl.BlockSpec((None, d, cb), lambda j, b, i: (layer, 0, j)),
            pl.BlockSpec((None, d, cb), lambda j, b, i: (layer, 0, ncb + j)),
            pl.BlockSpec((ns, d), lambda j, b, i: (0, 0)),
            pl.BlockSpec((None, CONV_W - 1, cb), lambda j, b, i: (b, 0, j)),
            pl.BlockSpec((None, 1, cb), lambda j, b, i: (b, 0, j)),
            pl.BlockSpec((CONV_W, cb), lambda j, b, i: (0, j)),
            vec, wspec, vec, wspec, vec, vec,
            pl.BlockSpec((None, slab, w_out.shape[2]), lambda j, b, i: (layer, step(j, b, i), 0)),
        ],
        out_specs=[
            pl.BlockSpec((None, tt, cb), lambda j, b, i: (b, i, j)),
            pl.BlockSpec((None, CONV_W - 1, cb), lambda j, b, i: (b, 0, j)),
            pl.BlockSpec((None, 1, cb), lambda j, b, i: (b, 0, j)),
            pl.BlockSpec((2, ns, cb), lambda j, b, i: (0, 0, j)),
            pl.BlockSpec((slab, w_out.shape[2]), lambda j, b, i: (step(j, b, i), 0)),
        ],
        out_shape=[
            jax.ShapeDtypeStruct((n, t, c), BF16),
            jax.ShapeDtypeStruct((n, CONV_W - 1, c), F32),
            jax.ShapeDtypeStruct((n, 1, c), F32),
            jax.ShapeDtypeStruct((2, ns, c), F32),
            jax.ShapeDtypeStruct(w_out.shape[1:], BF16),
        ],
        scratch_shapes=[
            pltpu.VMEM((d, 2 * cb), BF16),
            pltpu.VMEM((8, cb), F32),
            pltpu.VMEM((1, cb), F32),
            pltpu.VMEM((tt // sub, sub, 2 * cb), F32),
        ],
        compiler_params=_params("parallel", "arbitrary", "arbitrary"),
        name="a_in_rglru",
    )(xn.reshape(n, t, d), w_in, w_in, xs, conv_state, h0.reshape(n, 1, c), conv_w, row(conv_b),
      w_ga.astype(BF16), row(b_ga), w_gx.astype(BF16), row(b_gx), row(lam), w_out)
    return hg.reshape(n * t, c), nc, nh.reshape(n, c), ps, w_out_b


def _rel_bucket(dist):
    dist = np.asarray(dist)
    d = np.maximum(dist, 1).astype(np.float32)
    large = MAX_EXACT + (np.log(d / MAX_EXACT) / np.log(MAX_DIST / MAX_EXACT)
                         * (N_BUCKETS - MAX_EXACT)).astype(np.int32)
    large = np.minimum(large, N_BUCKETS - 1)
    return np.where(dist < MAX_EXACT, dist, large).astype(np.int32)


def _prompt_bias(rel_bias):
    span = 3 * BAND - 1
    neg = jnp.full((N_SLOTS, BAND - 1), -jnp.inf, F32)
    tables = []
    for g, r in enumerate(DILATIONS):
        tg = rel_bias[:, g * N_SLOTS:(g + 1) * N_SLOTS].astype(F32)
        tb = tg[_rel_bucket(np.arange(BAND + 1) * r)].T
        wr = jnp.concatenate([neg, tb[:, ::-1], neg, neg[:, :1]], axis=1)
        skew = jnp.tile(wr, (1, BAND))[:, :BAND * span].reshape(N_SLOTS, BAND, span)
        tables.append(skew[:, :, BAND - 1:])
    return jnp.stack(tables)


def _sample_bias(rel_bias):
    tables = []
    for g, r in enumerate(DILATIONS):
        steps = np.concatenate([BAND - np.arange(BAND), np.zeros(BAND, np.int64)])
        tg = rel_bias[:, g * N_SLOTS:(g + 1) * N_SLOTS].astype(F32)
        tables.append(tg[_rel_bucket(steps * r)].T)
    return jnp.stack(tables)


def _attn_prompt_kernel(q_ref, k_ref, v_ref, bias_ref, gate_ref, o_ref, acc_ref, m_ref, l_ref, *, t):
    g = pl.program_id(2)

    def unit(start, gi, r, has_prev):
        nk = 2 * BAND if has_prev else BAND
        kstart = start - BAND * r if has_prev else start
        rows = lambda s0, cnt: pl.ds(s0, cnt, stride=r) if r > 1 else pl.ds(s0, cnt)
        k = k_ref[rows(kstart, nk), :].astype(BF16)
        v = v_ref[rows(kstart, nk), :].astype(BF16)
        q = jnp.concatenate([q_ref[s, rows(start, BAND), :] for s in range(GQA)], axis=0).astype(BF16)
        sc = lax.dot_general(q, k, (((1,), (1,)), ((), ())), preferred_element_type=F32) * SM_SCALE
        sc = sc + bias_ref[:, :, 2 * BAND - nk:].reshape(GQA * BAND, nk)
        m = jnp.max(sc, axis=-1, keepdims=True)
        p = jnp.exp(sc - m)
        l = jnp.sum(p, axis=-1, keepdims=True)
        pv = jnp.dot(p.astype(BF16), v, preferred_element_type=F32)
        lane = lax.broadcasted_iota(jnp.int32, (BAND, LANES), 1)
        m_tile = jnp.zeros((BAND, LANES), F32)
        l_tile = jnp.ones((BAND, LANES), F32)
        for s in range(GQA):
            blk = slice(s * BAND, (s + 1) * BAND)
            acc_ref[gi, s, rows(start, BAND), :] = pv[blk]
            m_tile = jnp.where(lane == s, m[blk], m_tile)
            l_tile = jnp.where(lane == s, l[blk], l_tile)
        m_ref[gi, rows(start, BAND), :] = m_tile
        l_ref[gi, rows(start, BAND), :] = l_tile

    def group_pass(gi, r):
        nb = t // (BAND * r)

        def head_block(cls, carry):
            unit(cls, gi, r, False)
            return carry

        lax.fori_loop(0, r, head_block, 0, unroll=min(r, ATTN_UNROLL))

        def later_block(u, carry):
            cls = u % r
            blk = 1 + u // r
            unit(blk * (BAND * r) + cls, gi, r, True)
            return carry

        if nb > 1:
            lax.fori_loop(0, r * (nb - 1), later_block, 0, unroll=ATTN_UNROLL)

    for gi, r in enumerate(DILATIONS):
        pl.when(g == gi)(functools.partial(group_pass, gi, r))

    @pl.when(g == N_GROUPS - 1)
    def _():
        def combine(c, carry):
            rows = pl.ds(pl.multiple_of(c * BAND, BAND), BAND)
            ms = [m_ref[gi, rows, :] for gi in range(N_GROUPS)]
            mx = functools.reduce(jnp.maximum, ms)
            es = [jnp.exp(mg - mx) for mg in ms]
            den = sum(es[gi] * l_ref[gi, rows, :] for gi in range(N_GROUPS))
            ws = [e / den for e in es]
            for s in range(GQA):
                o = sum(ws[gi][:, s:s + 1] * acc_ref[gi, s, rows, :] for gi in range(N_GROUPS))
                cols = slice(s * HEAD_DIM, (s + 1) * HEAD_DIM)
                o_ref[rows, cols] = (o * jax.nn.silu(gate_ref[rows, cols])).astype(o_ref.dtype)
            return carry

        lax.fori_loop(0, t // BAND, combine, 0, unroll=2)


def _attn_prompt_call(q_heads, kv_heads, bias, gate, n, t):
    assert all(t % (BAND * r) == 0 for r in DILATIONS)
    q5 = q_heads.reshape(N_GROUPS, KV_HEADS, GQA, n, t, HEAD_DIM)
    kv6 = kv_heads.reshape(N_GROUPS, 2, KV_HEADS, n, t, HEAD_DIM)
    gate3 = gate.reshape(n, t, ATTN_W)
    kv_spec = lambda which: pl.BlockSpec((None, None, None, None, t, HEAD_DIM),
                                         lambda b, h, g: (g, which, h, b, 0, 0))
    out = pl.pallas_call(
        functools.partial(_attn_prompt_kernel, t=t),
        grid=(n, KV_HEADS, N_GROUPS),
        in_specs=[
            pl.BlockSpec((None, None, GQA, None, t, HEAD_DIM), lambda b, h, g: (g, h, 0, b, 0, 0)),
            kv_spec(0), kv_spec(1),
            pl.BlockSpec((None, GQA, BAND, 2 * BAND), lambda b, h, g: (g, h, 0, 0)),
            pl.BlockSpec((None, t, GQA * HEAD_DIM), lambda b, h, g: (b, 0, h)),
        ],
        out_specs=pl.BlockSpec((None, t, GQA * HEAD_DIM), lambda b, h, g: (b, 0, h)),
        out_shape=jax.ShapeDtypeStruct((n, t, ATTN_W), BF16),
        scratch_shapes=[pltpu.VMEM((N_GROUPS, GQA, t, HEAD_DIM), F32),
                        pltpu.VMEM((N_GROUPS, t, LANES), F32),
                        pltpu.VMEM((N_GROUPS, t, LANES), F32)],
        compiler_params=_params("parallel", "parallel", "arbitrary"),
        name="attn_prompt",
    )(q5, kv6, kv6, bias, gate3)
    return out.reshape(n * t, ATTN_W)


def _attn_sample_kernel(q_ref, kvn_ref, s0_ref, s1_ref, s2_ref, bias_ref, gate_ref, o_ref):
    st_refs = (s0_ref, s1_ref, s2_ref)
    bf = lambda x: x.astype(BF16)
    rnd = lambda x: x.astype(BF16).astype(F32)
    for h in range(KV_HEADS):
        sl = slice(h * GQA, (h + 1) * GQA)
        s_old, s_new, v_old, v_new = [], [], [], []
        for g in range(N_GROUPS):
            q = q_ref[g * N_SLOTS + h * GQA:g * N_SLOTS + (h + 1) * GQA, :]
            k_old = st_refs[g][:, 0, h, :]
            v_old.append(st_refs[g][:, 1, h, :])
            row = g * 2 * KV_HEADS + h
            k_new = kvn_ref[row:row + 1, :]
            v_new.append(kvn_ref[row + KV_HEADS:row + KV_HEADS + 1, :])
            so = lax.dot_general(bf(q), bf(k_old), (((1,), (1,)), ((), ())), preferred_element_type=F32)
            s_old.append(so * SM_SCALE + bias_ref[g, sl, :BAND])
            sn = jnp.sum(rnd(q) * rnd(k_new), axis=-1, keepdims=True)
            s_new.append(sn * SM_SCALE + bias_ref[g, sl, BAND:BAND + 1])
        m = functools.reduce(jnp.maximum, [jnp.max(x, axis=-1, keepdims=True) for x in s_old] + s_new)
        num = jnp.zeros((GQA, HEAD_DIM), F32)
        den = jnp.zeros((GQA, 1), F32)
        for g in range(N_GROUPS):
            p_old = jnp.exp(s_old[g] - m)
            p_new = jnp.exp(s_new[g] - m)
            den = den + jnp.sum(p_old, axis=-1, keepdims=True) + p_new
            num = num + jnp.dot(bf(p_old), bf(v_old[g]), preferred_element_type=F32) + rnd(p_new) * rnd(v_new[g])
        o_ref[sl, :] = ((num / den) * jax.nn.silu(gate_ref[sl, :])).astype(o_ref.dtype)


def _attn_sample_call(q, kv_new, states, bias, gate):
    n = q.shape[0]
    st_views, st_specs = [], []
    for g, r in enumerate(DILATIONS):
        assert states[g].shape[1] == BAND * r
        st_views.append(states[g].reshape(n, BAND, r, 2, KV_HEADS, HEAD_DIM))
        st_specs.append(pl.BlockSpec((None, BAND, None, 2, KV_HEADS, HEAD_DIM), lambda b: (b, 0, 0, 0, 0, 0)))
    per_b = lambda rows: pl.BlockSpec((None, rows, HEAD_DIM), lambda b: (b, 0, 0))
    out = pl.pallas_call(
        _attn_sample_kernel,
        grid=(n,),
        in_specs=[per_b(N_GROUPS * N_SLOTS), per_b(N_GROUPS * 2 * KV_HEADS)] + st_specs + [
            pl.BlockSpec((N_GROUPS, N_SLOTS, 2 * BAND), lambda b: (0, 0, 0)), per_b(N_SLOTS)],
        out_specs=per_b(N_SLOTS),
        out_shape=jax.ShapeDtypeStruct((n, N_SLOTS, HEAD_DIM), BF16),
        compiler_params=_params("parallel"),
        name="attn_sample",
    )(q.reshape(n, N_GROUPS * N_SLOTS, HEAD_DIM), kv_new.reshape(n, N_GROUPS * 2 * KV_HEADS, HEAD_DIM),
      *st_views, bias, gate.reshape(n, N_SLOTS, HEAD_DIM))
    return out.reshape(n, ATTN_W)


def _roll_kernel(cur_ref, nxt_ref, new_ref, o_ref):
    tb = cur_ref.shape[0]
    last_block = pl.program_id(1) == pl.num_programs(1) - 1
    o_ref[0:tb - 1] = cur_ref[1:tb]
    o_ref[tb - 1:tb] = jnp.where(last_block, new_ref[...], nxt_ref[...])


def _roll_window_call(state, new):
    n, w = state.shape[:2]
    tb = _tile(w, 512)
    tail = state.shape[2:]
    zeros = (0,) * len(tail)
    blk = pl.BlockSpec((None, tb) + tail, lambda b, i: (b, i) + zeros)
    one = lambda index: pl.BlockSpec((None, 1) + tail, index)
    return pl.pallas_call(
        _roll_kernel,
        grid=(n, w // tb),
        in_specs=[blk, one(lambda b, i: (b, jnp.minimum((i + 1) * tb, w - 1)) + zeros), one(lambda b, i: (b, 0) + zeros)],
        out_specs=blk,
        out_shape=jax.ShapeDtypeStruct(state.shape, state.dtype),
        compiler_params=_params("parallel", "arbitrary"),
        name="roll_window",
    )(state, state, new)


def _kv_window_kernel(kv_ref, o_ref):
    for s in range(2):
        for h in range(KV_HEADS):
            col = (s * KV_HEADS + h) * HEAD_DIM
            o_ref[:, s, h, :] = kv_ref[:, col:col + HEAD_DIM]


def _kv_window_call(kv, g, keep):
    n, t, _ = kv.shape
    tb = _tile(keep, 512)
    assert (t - keep) % tb == 0
    first = (t - keep) // tb
    return pl.pallas_call(
        _kv_window_kernel,
        grid=(n, keep // tb),
        in_specs=[pl.BlockSpec((None, tb, KV_GROUP_W), lambda b, i: (b, first + i, g))],
        out_specs=pl.BlockSpec((None, tb, 2, KV_HEADS, HEAD_DIM), lambda b, i: (b, i, 0, 0, 0)),
        out_shape=jax.ShapeDtypeStruct((n, keep, 2, KV_HEADS, HEAD_DIM), F32),
        compiler_params=_params("parallel", "parallel"),
        name="kv_window",
    )(kv)


def _trunks(xp, xs, conv_p, h_p, conv_s, h_s, kv_bufs, w):
    n, t, d = xp.shape
    ns = xs.shape[0]
    assert xs.shape[1] == 1
    n_a = w["a_w_in"].shape[0]
    n_b = w["b_w_in"].shape[0]
    xp2 = xp.reshape(n * t, d)
    xs2 = xs.reshape(ns, d)
    (xnp,) = _norm_call(xp2, [w["a_pre_g"][0]])
    (xns,) = _norm_call(xs2, [w["a_pre_g"][0]])
    conv_out_p, h_out_p, conv_out_s, h_out_s = [], [], [], []
    for l in range(n_a):
        mixer = (w["a_conv_w"][l], w["a_conv_b"][l], w["a_w_gate_a"][l], w["a_b_gate_a"][l],
                 w["a_w_gate_x"][l], w["a_b_gate_x"][l], w["a_lambda"][l])
        hgp, nc, nh, proj_s, w_out_b = _a_front_call(xnp, xns, w["a_w_in"], w["a_w_out"], l, conv_p[l], h_p[l],
                                                     *mixer, n, t)
        conv_out_p.append(nc)
        h_out_p.append(nh)
        hgs, nc, nh = _rglru_step_call(proj_s, conv_s[l], h_s[l], *mixer)
        conv_out_s.append(nc)
        h_out_s.append(nh)
        gains = [w["a_pre_g"][l + 1]] if l + 1 < n_a else [w["kv_norm_g"], w["b_pre_g"][0]]
        xp2, xnps, ys = _out_res_call(hgp, hgs, w_out_b, xp2, w["a_post_g"][l], gains,
                                      name="a_out_proj")
        xs2, xnss = _resnorm_call(ys, xs2, w["a_post_g"][l], gains)
        xnp, xns = xnps[0], xnss[0]

    kvp, kvp_heads, kvs = _mm_call(xnp, xns, w["w_kv"], heads=True, name="kv_proj")
    kv_out_p = [_kv_window_call(kvp.reshape(n, t, N_GROUPS * KV_GROUP_W), g, min(WINDOWS[g], t))
                for g in range(N_GROUPS)]
    kvs4 = kvs.reshape(ns, 1, N_GROUPS, 2, KV_HEADS, HEAD_DIM)
    kv_out_s = [_roll_window_call(kv_bufs[g], kvs4[:, :, g]) for g in range(N_GROUPS)]
    bias_p = _prompt_bias(w["rel_bias"])
    bias_s = _sample_bias(w["rel_bias"])

    xnp, xns = xnps[1], xnss[1]
    for l in range(n_b):
        gate_p, gate_s, w_out_b = _mm_call(xnp, xns, w["b_w_in"], layer=l, cols=(Q_W, ATTN_W),
                                           rounds=(w["b_w_out"], l), name="gate_proj")
        q_heads, q_s = _mm_call(xnp, xns, w["b_w_in"], layer=l, cols=(0, Q_W), natural=False, heads=True,
                                name="q_proj")
        ogp = _attn_prompt_call(q_heads, kvp_heads, bias_p, gate_p, n, t)
        ogs = _attn_sample_call(q_s, kvs, kv_bufs, bias_s, gate_s)
        gains = [w["b_pre_g"][l + 1]] if l + 1 < n_b else []
        xp2, xnps, ys = _out_res_call(ogp, ogs, w_out_b, xp2, w["b_post_g"][l], gains,
                                      name="b_out_proj")
        xs2, xnss = _resnorm_call(ys, xs2, w["b_post_g"][l], gains)
        if gains:
            xnp, xns = xnps[0], xnss[0]
    prompt_out = (xp2.reshape(n, t, d), jnp.stack(conv_out_p), jnp.stack(h_out_p), *kv_out_p)
    sample_out = (xs2.reshape(ns, 1, d), jnp.stack(conv_out_s), jnp.stack(h_out_s), *kv_out_s)
    return prompt_out, sample_out


def kernel(x_prompt, x_sample, state_conv, state_h, state_kv_w128, state_kv_w512, state_kv_w2048,
           a_pre_g, a_w_in, a_conv_w, a_conv_b, a_w_gate_a, a_b_gate_a, a_w_gate_x, a_b_gate_x,
           a_lambda, a_w_out, a_post_g, kv_norm_g, w_kv, rel_bias, b_pre_g, b_w_in, b_w_out, b_post_g):
    w = dict(
        a_pre_g=a_pre_g, a_w_in=a_w_in, a_conv_w=a_conv_w, a_conv_b=a_conv_b,
        a_w_gate_a=a_w_gate_a, a_b_gate_a=a_b_gate_a, a_w_gate_x=a_w_gate_x, a_b_gate_x=a_b_gate_x,
        a_lambda=a_lambda, a_w_out=a_w_out, a_post_g=a_post_g, kv_norm_g=kv_norm_g,
        w_kv=w_kv, rel_bias=rel_bias, b_pre_g=b_pre_g, b_w_in=b_w_in, b_w_out=b_w_out, b_post_g=b_post_g)
    nb_p = x_prompt.shape[0]
    n_a = a_w_in.shape[0]
    d_rnn = a_conv_w.shape[-1]
    conv0 = jnp.zeros((n_a, nb_p, CONV_W - 1, d_rnn), F32)
    h0 = jnp.zeros((n_a, nb_p, d_rnn), F32)
    prompt_out, sample_out = _trunks(x_prompt, x_sample, conv0, h0, state_conv, state_h,
                                     (state_kv_w128, state_kv_w512, state_kv_w2048), w)
    return (prompt_out[0], sample_out[0], *prompt_out[1:], *sample_out[1:])
```
